```python
import math
import jax, jax.numpy as jnp
from jax import lax
import numpy as np

D_MODEL = 1024
BATCH = 8
SEQ = 2048
DEPTH = 2
DEC_BATCH = 128
DEC_SEQ = 1
PAST_LEN = 2048
PAGE_SIZE = 128

HEAD_DIM = 64
BR_WIDTH = D_MODEL // 4
N_GROUPS = BR_WIDTH // HEAD_DIM
N_BRANCH = 5
N_SLICES = 12
N_IN = N_SLICES * BR_WIDTH
DIFF_SUB = HEAD_DIM // 2
CONV_W = 3
CHUNK = 128
Q_BLOCK = 128
N_MEM = 256
D_FF = ((8 * D_MODEL) // 3 + 127) // 128 * 128
EPS = 1e-6
NEG_INF = -1e30

kernel_name = 'hybrid_gated_branch_decoder_step'


def _rms(x, g):
    xf = x.astype(jnp.float32)
    y = xf * lax.rsqrt(jnp.mean(xf * xf, axis=-1, keepdims=True) + EPS)
    return (y * g.astype(jnp.float32)).astype(x.dtype)


def _layer_norm(x, g, b):
    xf = x.astype(jnp.float32)
    mu = jnp.mean(xf, axis=-1, keepdims=True)
    xc = xf - mu
    y = xc * lax.rsqrt(jnp.mean(xc * xc, axis=-1, keepdims=True) + EPS)
    return (y * g.astype(jnp.float32) + b.astype(jnp.float32)).astype(x.dtype)


def _alibi_slopes(n):
    return jnp.asarray(2.0 ** (-8.0 * np.arange(1, n + 1) / n), dtype=jnp.float32)


def _causal_conv(prev, x, w):
    t = x.shape[1]
    xe = jnp.concatenate([prev.astype(x.dtype), x], axis=1)
    out = w[0] * xe[:, 0:t]
    for j in range(1, CONV_W):
        out = out + w[j] * xe[:, j:j + t]
    return out, xe[:, t:]


def _sweep_queries(fn, q, q_pos):
    n, t = q.shape[0], q.shape[1]
    if t % Q_BLOCK != 0:
        return fn(q, q_pos)
    nb = t // Q_BLOCK
    qb = jnp.moveaxis(q.reshape((n, nb, Q_BLOCK) + q.shape[2:]), 1, 0)
    pb = q_pos.reshape(nb, Q_BLOCK)
    out = lax.map(lambda a: fn(a[0], a[1]), (qb, pb))
    out = jnp.moveaxis(out, 0, 1)
    return out.reshape((n, t) + out.shape[3:])


def _stick_breaking(q, k, v, q_pos, k_pos):
    z = jnp.einsum('nqhd,nkhd->nhqk', q, k).astype(jnp.float32) * (HEAD_DIM ** -0.5)
    mask = k_pos[None, :] < q_pos[:, None]
    log_beta = jax.nn.log_sigmoid(z)
    log_1m = jnp.where(mask, jax.nn.log_sigmoid(-z), 0.0)
    between = lax.cumsum(log_1m, axis=3, reverse=True) - log_1m
    a = jnp.where(mask, jnp.exp(log_beta + between), 0.0)
    return jnp.einsum('nhqk,nkhd->nqhd', a.astype(v.dtype), v)


def _diff_attention(q, k, v, q_pos, k_pos, lam, slopes):
    s = jnp.einsum('nqhcd,nkhcd->nchqk', q, k).astype(jnp.float32) * (DIFF_SUB ** -0.5)
    dist = (q_pos[:, None] - k_pos[None, :]).astype(jnp.float32)
    bias = -slopes[:, None, None] * dist
    mask = k_pos[None, :] <= q_pos[:, None]
    p = jax.nn.softmax(jnp.where(mask, s + bias, NEG_INF), axis=-1)
    w = p[:, 0] - lam * p[:, 1]
    return jnp.einsum('nhqk,nkhd->nqhd', w.astype(v.dtype), v)


def _mem_attention(q, mk, mv):
    s = jnp.einsum('nqhd,nmhd->nhqm', q, mk).astype(jnp.float32) * (HEAD_DIM ** -0.5)
    p = jax.nn.softmax(s, axis=-1)
    return jnp.einsum('nhqm,nmhd->nqhd', p.astype(mv.dtype), mv)


def _mem_kv(mem, l, W):
    n, m = mem.shape[0], mem.shape[1]
    kv = _rms(mem, W['mem_norm_g'][l]) @ W['w_mem_kv'][l]
    mk, mv = jnp.split(kv, 2, axis=-1)
    mk = _rms(mk.reshape(n, m, N_GROUPS, HEAD_DIM), W['mem_kn_g'][l])
    return mk, mv.reshape(n, m, N_GROUPS, HEAD_DIM)


def _trunk_layer(x, l, pos, conv_a_prev, ffn_prev, past_k_sb, past_v_sb, past_k_df, past_v_df,
                 mem_k, mem_v, W):
    n, t, d = x.shape
    h = _rms(x, W['norm1_g'][l])
    proj = h @ W['w_in'][l]
    (a_b, a_c, a_h, g_u, g_v, s_q, s_k, s_v, d_q, d_k, d_v, m_q) = jnp.split(proj, N_SLICES, axis=-1)
    heads = lambda z: z.reshape(n, z.shape[1], N_GROUPS, HEAD_DIM)

    conv_out, conv_a_new = _causal_conv(conv_a_prev, a_c * a_h, W['conv_a_w'][l])
    y_a = a_b * conv_out

    v_n = _layer_norm(g_v, W['gmlp_ln_g'][l], W['gmlp_ln_b'][l])
    c = min(t, CHUNK)
    tril = jnp.tril(jnp.ones((CHUNK, CHUNK), dtype=bool))
    ws = jnp.where(tril, W['gmlp_ws'][l], 0.0)[:, :c, :c]
    sp_bias = jnp.swapaxes(W['gmlp_b'][l][:, :c], 0, 1)
    vc = v_n.reshape(n, t // c, c, N_GROUPS, HEAD_DIM)
    mixed = jnp.einsum('gts,ncsgd->nctgd', ws, vc) + sp_bias[None, None, :, :, None]
    y_b = g_u * mixed.reshape(n, t, BR_WIDTH)

    q_sb, k_sb, v_sb = heads(s_q), heads(s_k), heads(s_v)
    if past_k_sb is None:
        sb_keys, sb_vals = k_sb, v_sb
    else:
        sb_keys = jnp.concatenate([past_k_sb.astype(k_sb.dtype), k_sb], axis=1)
        sb_vals = jnp.concatenate([past_v_sb.astype(v_sb.dtype), v_sb], axis=1)
    sb_pos = jnp.arange(sb_keys.shape[1])
    y_c = _sweep_queries(lambda qb, pb: _stick_breaking(qb, sb_keys, sb_vals, pb, sb_pos), q_sb, pos)

    lam_init = 0.8 - 0.6 * math.exp(-0.3 * l)
    sub = lambda z, g: _rms(z.reshape(n, z.shape[1], N_GROUPS, 2, DIFF_SUB), g.reshape(2, DIFF_SUB))
    q_df = sub(d_q, W['diff_qn_g'][l])
    k_df = sub(d_k, W['diff_kn_g'][l])
    v_df = heads(d_v)
    if past_k_df is None:
        df_keys, df_vals = k_df, v_df
    else:
        pk = past_k_df.astype(k_df.dtype).reshape(n, past_k_df.shape[1], N_GROUPS, 2, DIFF_SUB)
        df_keys = jnp.concatenate([pk, k_df], axis=1)
        df_vals = jnp.concatenate([past_v_df.astype(v_df.dtype), v_df], axis=1)
    df_pos = jnp.arange(df_keys.shape[1])
    lp = W['diff_lambda'][l].astype(jnp.float32)
    lam = jnp.exp(jnp.sum(lp[0] * lp[1])) - jnp.exp(jnp.sum(lp[2] * lp[3])) + lam_init
    slopes = _alibi_slopes(N_GROUPS)
    y_d = _sweep_queries(lambda qb, pb: _diff_attention(qb, df_keys, df_vals, pb, df_pos, lam, slopes),
                         q_df, pos)
    y_d = _rms(y_d, W['diff_out_g'][l]) * (1.0 - lam_init)

    y_m = _mem_attention(_rms(heads(m_q), W['mem_qn_g'][l]), mem_k.astype(x.dtype), mem_v.astype(x.dtype))

    ys = jnp.stack([y_a, y_b, y_c.reshape(n, t, BR_WIDTH), y_d.reshape(n, t, BR_WIDTH),
                    y_m.reshape(n, t, BR_WIDTH)], axis=2)
    br = jnp.einsum('ntbc,bcd->ntbd', ys, W['w_branch'][l])
    gates = jax.nn.sigmoid(h @ W['w_gate'][l] + W['b_gate'][l]).reshape(n, t, N_BRANCH, d)
    x = x + jnp.sum(gates * br, axis=2) @ W['w_o'][l]

    up = _rms(x, W['norm2_g'][l]) @ W['w_up'][l]
    up_c, ffn_new = _causal_conv(ffn_prev, up, W['conv_ffn_w'][l])
    gate, val = jnp.split(up_c + W['conv_ffn_b'][l], 2, axis=-1)
    x = x + (jax.nn.silu(gate) * val) @ W['w_down'][l]
    return (x, conv_a_new, v_n, k_sb, v_sb, k_df.reshape(n, t, N_GROUPS, HEAD_DIM), v_df, ffn_new)


def setup_inputs(seed: int = 0) -> dict:
    key = jax.random.key(seed)
    ks = iter(jax.random.split(key, 48))
    nrm = lambda shape, scale: jax.random.normal(next(ks), shape, jnp.float32) * scale
    gain = lambda shape: 1.0 + nrm(shape, 0.02)
    n_pages = PAST_LEN // PAGE_SIZE
    n_phys = (DEC_BATCH * n_pages * 5) // 4
    pool = (DEPTH, n_phys, PAGE_SIZE, N_GROUPS, HEAD_DIM)
    perm = jax.random.permutation(next(ks), n_phys)[:DEC_BATCH * n_pages]
    page_table = perm.reshape(DEC_BATCH, n_pages).astype(jnp.int32)
    return {
        'x_prompt': nrm((BATCH, SEQ, D_MODEL), 1.0),
        'x_sample': nrm((DEC_BATCH, DEC_SEQ, D_MODEL), 1.0),
        'state_conv_a': nrm((DEPTH, DEC_BATCH, CONV_W - 1, BR_WIDTH), 1.0),
        'cache_k_sb': nrm(pool, 1.0),
        'cache_v_sb': nrm(pool, 1.0),
        'cache_k_diff': nrm(pool, 1.0),
        'cache_v_diff': nrm(pool, 1.0),
        'cache_mem_k': nrm((DEPTH, DEC_BATCH, N_MEM, N_GROUPS, HEAD_DIM), 1.0),
        'cache_mem_v': nrm((DEPTH, DEC_BATCH, N_MEM, N_GROUPS, HEAD_DIM), 1.0),
        'state_conv_ffn': nrm((DEPTH, DEC_BATCH, CONV_W - 1, 2 * D_FF), 1.0),
        'page_table': page_table,
        'mem_prompt': nrm((BATCH, N_MEM, D_MODEL), 1.0),
        'norm1_g': gain((DEPTH, D_MODEL)),
        'w_in': nrm((DEPTH, D_MODEL, N_IN), D_MODEL ** -0.5),
        'conv_a_w': nrm((DEPTH, CONV_W, BR_WIDTH), CONV_W ** -0.5),
        'gmlp_ln_g': gain((DEPTH, BR_WIDTH)),
        'gmlp_ln_b': nrm((DEPTH, BR_WIDTH), 0.02),
        'gmlp_ws': nrm((DEPTH, N_GROUPS, CHUNK, CHUNK), CHUNK ** -0.5),
        'gmlp_b': gain((DEPTH, N_GROUPS, CHUNK)),
        'diff_qn_g': gain((DEPTH, HEAD_DIM)),
        'diff_kn_g': gain((DEPTH, HEAD_DIM)),
        'diff_lambda': nrm((DEPTH, 4, DIFF_SUB), 0.1),
        'diff_out_g': gain((DEPTH, HEAD_DIM)),
        'mem_norm_g': gain((DEPTH, D_MODEL)),
        'w_mem_kv': nrm((DEPTH, D_MODEL, 2 * BR_WIDTH), D_MODEL ** -0.5),
        'mem_qn_g': gain((DEPTH, HEAD_DIM)),
        'mem_kn_g': gain((DEPTH, HEAD_DIM)),
        'w_branch': nrm((DEPTH, N_BRANCH, BR_WIDTH, D_MODEL), BR_WIDTH ** -0.5),
        'w_gate': nrm((DEPTH, D_MODEL, N_BRANCH * D_MODEL), D_MODEL ** -0.5),
        'b_gate': nrm((DEPTH, N_BRANCH * D_MODEL), 0.02),
        'w_o': nrm((DEPTH, D_MODEL, D_MODEL), D_MODEL ** -0.5),
        'norm2_g': gain((DEPTH, D_MODEL)),
        'w_up': nrm((DEPTH, D_MODEL, 2 * D_FF), D_MODEL ** -0.5),
        'conv_ffn_w': nrm((DEPTH, CONV_W, 2 * D_FF), CONV_W ** -0.5),
        'conv_ffn_b': nrm((DEPTH, 2 * D_FF), 0.02),
        'w_down': nrm((DEPTH, D_FF, D_MODEL), D_FF ** -0.5),
    }


def reference(x_prompt, x_sample, state_conv_a, cache_k_sb, cache_v_sb, cache_k_diff, cache_v_diff,
              cache_mem_k, cache_mem_v, state_conv_ffn, page_table, mem_prompt,
              norm1_g, w_in, conv_a_w, gmlp_ln_g, gmlp_ln_b, gmlp_ws, gmlp_b,
              diff_qn_g, diff_kn_g, diff_lambda, diff_out_g,
              mem_norm_g, w_mem_kv, mem_qn_g, mem_kn_g,
              w_branch, w_gate, b_gate, w_o, norm2_g, w_up, conv_ffn_w, conv_ffn_b, w_down):
    W = dict(norm1_g=norm1_g, w_in=w_in, conv_a_w=conv_a_w, gmlp_ln_g=gmlp_ln_g, gmlp_ln_b=gmlp_ln_b,
             gmlp_ws=gmlp_ws, gmlp_b=gmlp_b, diff_qn_g=diff_qn_g, diff_kn_g=diff_kn_g,
             diff_lambda=diff_lambda, diff_out_g=diff_out_g, mem_norm_g=mem_norm_g, w_mem_kv=w_mem_kv,
             mem_qn_g=mem_qn_g, mem_kn_g=mem_kn_g, w_branch=w_branch, w_gate=w_gate, b_gate=b_gate,
             w_o=w_o, norm2_g=norm2_g, w_up=w_up, conv_ffn_w=conv_ffn_w, conv_ffn_b=conv_ffn_b,
             w_down=w_down)
    n_p, t_p = x_prompt.shape[0], x_prompt.shape[1]
    n_s, t_s = x_sample.shape[0], x_sample.shape[1]
    pos_p = jnp.arange(t_p)
    pos_s = PAST_LEN + jnp.arange(t_s)
    zeros_a = jnp.zeros((n_p, CONV_W - 1, BR_WIDTH), x_prompt.dtype)
    zeros_f = jnp.zeros((n_p, CONV_W - 1, 2 * D_FF), x_prompt.dtype)

    def gather(cache, l):
        rows = cache[l][page_table]
        return rows.reshape(n_s, -1, N_GROUPS, HEAD_DIM)

    xp, xs = x_prompt, x_sample
    ca_p, ca_s, gv_s = [], [], []
    ksb_p, vsb_p, ksb_s, vsb_s = [], [], [], []
    kdf_p, vdf_p, kdf_s, vdf_s = [], [], [], []
    mk_p, mv_p, cf_p, cf_s = [], [], [], []
    for l in range(DEPTH):
        mem_k, mem_v = _mem_kv(mem_prompt, l, W)
        xp, ca, _, ksb, vsb, kdf, vdf, cf = _trunk_layer(
            xp, l, pos_p, zeros_a, zeros_f, None, None, None, None, mem_k, mem_v, W)
        ca_p.append(ca); ksb_p.append(ksb); vsb_p.append(vsb)
        kdf_p.append(kdf); vdf_p.append(vdf); cf_p.append(cf)
        mk_p.append(mem_k); mv_p.append(mem_v)
        xs, ca, gv, ksb, vsb, kdf, vdf, cf = _trunk_layer(
            xs, l, pos_s, state_conv_a[l], state_conv_ffn[l],
            gather(cache_k_sb, l), gather(cache_v_sb, l), gather(cache_k_diff, l), gather(cache_v_diff, l),
            cache_mem_k[l], cache_mem_v[l], W)
        ca_s.append(ca); gv_s.append(gv); ksb_s.append(ksb); vsb_s.append(vsb)
        kdf_s.append(kdf); vdf_s.append(vdf); cf_s.append(cf)

    y_prompt, y_sample = xp, xs
    new_conv_a_prompt = jnp.stack(ca_p, axis=0)
    new_conv_a_sample = jnp.stack(ca_s, axis=0)
    new_gmlp_v_sample = jnp.stack(gv_s, axis=0)
    new_k_sb_prompt = jnp.stack(ksb_p, axis=0)
    new_v_sb_prompt = jnp.stack(vsb_p, axis=0)
    new_k_sb_sample = jnp.stack(ksb_s, axis=0)
    new_v_sb_sample = jnp.stack(vsb_s, axis=0)
    new_k_diff_prompt = jnp.stack(kdf_p, axis=0)
    new_v_diff_prompt = jnp.stack(vdf_p, axis=0)
    new_k_diff_sample = jnp.stack(kdf_s, axis=0)
    new_v_diff_sample = jnp.stack(vdf_s, axis=0)
    new_mem_k_prompt = jnp.stack(mk_p, axis=0)
    new_mem_v_prompt = jnp.stack(mv_p, axis=0)
    new_conv_ffn_prompt = jnp.stack(cf_p, axis=0)
    new_conv_ffn_sample = jnp.stack(cf_s, axis=0)
    return (y_prompt, y_sample, new_conv_a_prompt, new_conv_a_sample, new_gmlp_v_sample,
            new_k_sb_prompt, new_v_sb_prompt, new_k_sb_sample, new_v_sb_sample,
            new_k_diff_prompt, new_v_diff_prompt, new_k_diff_sample, new_v_diff_sample,
            new_mem_k_prompt, new_mem_v_prompt, new_conv_ffn_prompt, new_conv_ffn_sample)
```

```python
import functools
import math

import numpy as np
import jax
import jax.numpy as jnp
from jax import lax
from jax.experimental import pallas as pl
from jax.experimental.pallas import tpu as pltpu

D_MODEL = 1024
HEAD_DIM = 64
BR_WIDTH = 256
N_GROUPS = 4
DIFF_SUB = 32
CHUNK = 128
D_FF = 2816
EPS = 1e-6
NEG_BIG = -1e30
LOG2E = 1.4426950408889634
VMEM_LIMIT_BYTES = 56 * 1024 * 1024

F32 = jnp.float32
BF16 = jnp.bfloat16


def _dot(a, b):
    return jnp.dot(a, b, preferred_element_type=F32)


def _dot_nt(a, b):
    return lax.dot_general(a, b, (((1,), (1,)), ((), ())), preferred_element_type=F32)


def _split_bf16(x):
    hi = x.astype(BF16)
    lo = (x - hi.astype(F32)).astype(BF16)
    return hi, lo


def _dot_split(x, w):
    hi, lo = _split_bf16(x)
    return _dot(hi, w) + _dot(lo, w)


def _rms(x, g):
    ms = jnp.mean(x * x, axis=-1, keepdims=True)
    return x * lax.rsqrt(ms + EPS) * g


def _group_rms(z, g, ones_bd, group):
    ms = _dot_split(z * z, ones_bd) * (1.0 / group)
    return z * lax.rsqrt(ms + EPS) * g


def _lane_iota(shape):
    return lax.broadcasted_iota(jnp.int32, shape, len(shape) - 1)


def _row_iota(shape):
    return lax.broadcasted_iota(jnp.int32, shape, len(shape) - 2)


def _div_pow2(x, d):
    assert d & (d - 1) == 0
    return lax.shift_right_logical(x, d.bit_length() - 1)


def _rem_pow2(x, d):
    assert d & (d - 1) == 0
    return x & (d - 1)


def _head_select(parts):
    lane = _lane_iota(parts[0].shape)
    out = parts[0]
    for h in range(1, N_GROUPS):
        out = jnp.where(lane >= HEAD_DIM * h, parts[h], out)
    return out


def _stack_heads(q, n_sub):
    lane = _lane_iota(q.shape)
    width = HEAD_DIM // n_sub
    zero = jnp.zeros_like(q)
    parts = []
    for c in range(n_sub):
        for h in range(N_GROUPS):
            lo = HEAD_DIM * h + width * c
            parts.append(jnp.where((lane >= lo) & (lane < lo + width), q, zero))
    return jnp.concatenate(parts, axis=0)


def _log_sigmoid_pair(z):
    lb = jnp.minimum(z, 0.0) - jnp.log(1.0 + jnp.exp(-jnp.abs(z)))
    return lb, lb - z


def _lam_value(lp, lam_init):
    a = jnp.sum(lp[0:1] * lp[1:2], axis=1, keepdims=True)
    b = jnp.sum(lp[2:3] * lp[3:4], axis=1, keepdims=True)
    return jnp.exp(a) - jnp.exp(b) + lam_init


def _block_ones(group):
    i = np.arange(BR_WIDTH)
    return jnp.asarray((i[:, None] // group) == (i[None, :] // group), dtype=BF16)


def _suffix_ones(n):
    i = np.arange(n)
    l = (i[:, None] > i[None, :])
    return jnp.asarray(np.concatenate([l, l], axis=0), dtype=BF16)


def _alibi_slopes_log2():
    return [LOG2E * 2.0 ** (-8.0 * (h + 1) / N_GROUPS) for h in range(N_GROUPS)]


def _full_spec(shape):
    nd = len(shape)
    return pl.BlockSpec(shape, lambda *_: (0,) * nd, pipeline_mode=pl.Buffered(1))


def _params(*sem):
    return pltpu.CompilerParams(dimension_semantics=sem, vmem_limit_bytes=VMEM_LIMIT_BYTES)


def _group_rms_t(zt, g_col, ones_bd, group):
    hi, lo = _split_bf16(zt * zt)
    ms = (_dot(ones_bd, hi) + _dot(ones_bd, lo)) * (1.0 / group)
    return zt * lax.rsqrt(ms + EPS) * g_col


def _memkv_kernel(mem_ref, g_ref, wt_ref, kg_ref, bd64_ref, k_ref, v_ref, k16_ref, v16_ref):
    hb = _rms(mem_ref[0], g_ref[...]).astype(BF16)
    kvt = _dot_nt(wt_ref[...], hb)
    k = _group_rms_t(kvt[:BR_WIDTH], kg_ref[...], bd64_ref[...], HEAD_DIM)
    v = kvt[BR_WIDTH:]
    k_ref[0] = k
    v_ref[0] = v
    k16_ref[0] = k.astype(BF16)
    v16_ref[0] = v.astype(BF16)


def _mem_kv(mem, g, w, kg, bd64):
    n, m, d = mem.shape
    blk = pl.BlockSpec((1, BR_WIDTH, m), lambda i: (i, 0, 0))
    return pl.pallas_call(
        _memkv_kernel,
        grid=(n,),
        in_specs=[pl.BlockSpec((1, m, d), lambda i: (i, 0, 0)), _full_spec(g.shape), _full_spec(w.shape),
                  _full_spec(kg.shape), _full_spec(bd64.shape)],
        out_specs=[blk, blk, blk, blk],
        out_shape=[jax.ShapeDtypeStruct((n, BR_WIDTH, m), F32)] * 2
        + [jax.ShapeDtypeStruct((n, BR_WIDTH, m), BF16)] * 2,
        compiler_params=_params("arbitrary"),
        name="mem_kv",
    )(mem, g, w, kg, bd64)


def _layer_norm(x, g, b):
    mu = jnp.mean(x, axis=-1, keepdims=True)
    xc = x - mu
    return xc * lax.rsqrt(jnp.mean(xc * xc, axis=-1, keepdims=True) + EPS) * g + b


def _mem_attention(qm, mkt, mvt):
    r = qm.shape[0]
    s = _dot(_stack_heads(qm, 1), mkt)
    p = jnp.exp2(s - jnp.max(s, axis=-1, keepdims=True))
    l = jnp.sum(p, axis=-1, keepdims=True)
    o = _dot_nt(p.astype(BF16), mvt) / l
    return _head_select([o[h * r:(h + 1) * r] for h in range(N_GROUPS)])


def _inproj_kernel(x_ref, g1_ref, win_ref, wsbt_ref, wdft_ref, cw_ref, lng_ref, lnb_ref, ws_ref, gb_ref, dqg_ref,
                   dkg_ref, mqg_ref, bd32_ref, bd64_ref, mk_ref, mv_ref,
                   yabm_ref, qsb_ref, ksb16_ref, vsb16_ref, qdf_ref, kdf16_ref, vdf16_ref,
                   ksb_ref, vsb_ref, kdf_ref, vdf_ref, ca_ref, carry_ref, *, tm):
    t = pl.program_id(1)
    hb = _rms(x_ref[0], g1_ref[...]).astype(BF16)

    def proj(lo, hi):
        return _dot(hb, win_ref[:, lo:hi])

    pa = proj(0, 3 * BR_WIDTH)
    ach = pa[:, BR_WIDTH:2 * BR_WIDTH] * pa[:, 2 * BR_WIDTH:]

    @pl.when(t == 0)
    def _():
        carry_ref[...] = jnp.zeros_like(carry_ref)

    prev = carry_ref[...]
    p1, p2 = prev[7:8], prev[6:7]
    row = _row_iota(ach.shape)
    s1 = jnp.where(row == 0, p1, pltpu.roll(ach, 1, axis=0))
    s2 = jnp.where(row == 0, p2, jnp.where(row == 1, p1, pltpu.roll(ach, 2, axis=0)))
    cw = cw_ref[...]
    yabm_ref[0, :, 0:BR_WIDTH] = (pa[:, :BR_WIDTH] * (cw[0:1] * s2 + cw[1:2] * s1 + cw[2:3] * ach)).astype(BF16)
    carry_ref[...] = ach[tm - 8:tm]
    ca_ref[0] = ach[tm - 2:tm]

    pg = proj(3 * BR_WIDTH, 5 * BR_WIDTH)
    vn = _layer_norm(pg[:, BR_WIDTH:], lng_ref[...], lnb_ref[...])
    tril = _row_iota((CHUNK, CHUNK)) >= _lane_iota((CHUNK, CHUNK))
    wsm = [jnp.where(tril, ws_ref[g], 0.0).astype(BF16) for g in range(N_GROUPS)]
    gb = gb_ref[...]
    for c in range(tm // CHUNK):
        vc = vn[c * CHUNK:(c + 1) * CHUNK].astype(BF16)
        mixed = _head_select([_dot(wsm[g], vc) for g in range(N_GROUPS)])
        yabm_ref[0, c * CHUNK:(c + 1) * CHUNK, BR_WIDTH:2 * BR_WIDTH] = (
            pg[c * CHUNK:(c + 1) * CHUNK, :BR_WIDTH] * (mixed + gb)).astype(BF16)

    qsb_ref[0] = (proj(5 * BR_WIDTH, 6 * BR_WIDTH) * (HEAD_DIM ** -0.5)).astype(BF16)
    kvt = _dot_nt(wsbt_ref[...], hb)
    ksb_ref[0] = kvt[:BR_WIDTH]
    vsb_ref[0] = kvt[BR_WIDTH:]
    ksb16_ref[0] = kvt[:BR_WIDTH].astype(BF16)
    vsb16_ref[0] = kvt[BR_WIDTH:].astype(BF16)

    bd32 = bd32_ref[...]
    qd = _group_rms(proj(8 * BR_WIDTH, 9 * BR_WIDTH), dqg_ref[...], bd32, DIFF_SUB)
    qdf_ref[0] = (qd * (DIFF_SUB ** -0.5 * LOG2E)).astype(BF16)
    kvt = _dot_nt(wdft_ref[...], hb)
    kd = _group_rms_t(kvt[:BR_WIDTH], dkg_ref[...], bd32, DIFF_SUB)
    kdf_ref[0] = kd
    vdf_ref[0] = kvt[BR_WIDTH:]
    kdf16_ref[0] = kd.astype(BF16)
    vdf16_ref[0] = kvt[BR_WIDTH:].astype(BF16)

    qm = _group_rms(proj(11 * BR_WIDTH, 12 * BR_WIDTH), mqg_ref[...], bd64_ref[...], HEAD_DIM)
    qm = (qm * (HEAD_DIM ** -0.5 * LOG2E)).astype(BF16)
    yabm_ref[0, :, 2 * BR_WIDTH:] = _mem_attention(qm, mk_ref[0], mv_ref[0]).astype(BF16)


def _inproj(x, p, mk16, mv16, tm):
    n, t, d = x.shape
    consts = [p["norm1_g"], p["w_in"], p["w_sb_kv_t"], p["w_df_kv_t"], p["conv_a_w"], p["gmlp_ln_g"], p["gmlp_ln_b"],
              p["gmlp_ws"], p["gmlp_b_full"], p["diff_qn_g"], p["diff_kn_g_col"], p["mem_qn_g"], p["bd32"], p["bd64"]]
    seq = lambda w: pl.BlockSpec((1, tm, w), lambda i, j: (i, j, 0))
    seq_t = pl.BlockSpec((1, BR_WIDTH, tm), lambda i, j: (i, 0, j))
    mem_spec = pl.BlockSpec((1,) + mk16.shape[1:], lambda i, j: (i, 0, 0))
    bf = lambda w: jax.ShapeDtypeStruct((n, t, w), BF16)
    bf_t = jax.ShapeDtypeStruct((n, BR_WIDTH, t), BF16)
    fl_t = jax.ShapeDtypeStruct((n, BR_WIDTH, t), F32)
    return pl.pallas_call(
        functools.partial(_inproj_kernel, tm=tm),
        grid=(n, t // tm),
        in_specs=[seq(d)] + [_full_spec(c.shape) for c in consts] + [mem_spec, mem_spec],
        out_specs=[seq(3 * BR_WIDTH), seq(BR_WIDTH), seq_t, seq_t, seq(BR_WIDTH)] + [seq_t] * 6
        + [pl.BlockSpec((1, 2, BR_WIDTH), lambda i, j: (i, 0, 0))],
        out_shape=[bf(3 * BR_WIDTH), bf(BR_WIDTH), bf_t, bf_t, bf(BR_WIDTH), bf_t, bf_t] + [fl_t] * 4
        + [jax.ShapeDtypeStruct((n, 2, BR_WIDTH), F32)],
        scratch_shapes=[pltpu.VMEM((8, BR_WIDTH), F32)],
        compiler_params=_params("arbitrary", "arbitrary"),
        name="in_proj",
    )(x, *consts, mk16, mv16)


def _sb_kernel(q_ref, k_ref, v_ref, ll_ref, o_ref, acc_ref, *, tq, tk):
    qi = pl.program_id(1)
    qs = _stack_heads(q_ref[0], 1)
    rows = N_GROUPS * tq
    kb_diag = (qi * tq + tq - 1) // tk
    ll = ll_ref[...]

    def block(kb, carry, mask):
        start = pl.multiple_of(kb * tk, tk)
        z = _dot(qs, k_ref[0, :, pl.ds(start, tk)])
        lb, l1m = _log_sigmoid_pair(z)
        if mask is not None:
            l1m = jnp.where(mask, l1m, 0.0)
        hi, lo = _split_bf16(l1m)
        between = _dot(jnp.concatenate([hi, lo], axis=1), ll)
        a = jnp.exp(lb + between + carry)
        if mask is not None:
            a = jnp.where(mask, a, 0.0)
        pv = _dot_nt(a.astype(BF16), v_ref[0, :, pl.ds(start, tk)])
        return pv, carry + jnp.sum(l1m, axis=1, keepdims=True)

    q_pos = qi * tq + _rem_pow2(_row_iota((rows, tk)), tq)
    k_pos = kb_diag * tk + _lane_iota((rows, tk))
    pv, carry = block(kb_diag, jnp.zeros((rows, 1), F32), k_pos < q_pos)
    acc_ref[...] = pv

    def body(i, carry):
        pv, carry = block(kb_diag - 1 - i, carry, None)
        acc_ref[...] += pv
        return carry

    lax.fori_loop(0, kb_diag, body, carry)
    acc = acc_ref[...]
    o_ref[0] = _head_select([acc[h * tq:(h + 1) * tq] for h in range(N_GROUPS)]).astype(BF16)


def _sb_attention(q, k, v, tq, tk):
    n, t, w = q.shape
    ll = _suffix_ones(tk)
    kv_spec = pl.BlockSpec((1, w, t), lambda i, j: (i, 0, 0))
    return pl.pallas_call(
        functools.partial(_sb_kernel, tq=tq, tk=tk),
        grid=(n, t // tq),
        in_specs=[pl.BlockSpec((1, tq, w), lambda i, j: (i, j, 0)), kv_spec, kv_spec, _full_spec(ll.shape)],
        out_specs=pl.BlockSpec((1, tq, w), lambda i, j: (i, j, 0)),
        out_shape=jax.ShapeDtypeStruct((n, t, w), BF16),
        scratch_shapes=[pltpu.VMEM((N_GROUPS * tq, w), F32)],
        compiler_params=_params("arbitrary", "arbitrary"),
        name="sb_attn",
    )(q, k, v, ll)


def _diff_combine(acc, l, lam, og, bd64, lam_init, r):
    y = acc / l
    y0 = _head_select([y[h * r:(h + 1) * r] for h in range(N_GROUPS)])
    y1 = _head_select([y[(N_GROUPS + h) * r:(N_GROUPS + h + 1) * r] for h in range(N_GROUPS)])
    return _group_rms(y0 - lam * y1, og, bd64, HEAD_DIM) * (1.0 - lam_init)


def _diff_kernel(q_ref, k_ref, v_ref, lp_ref, og_ref, bd64_ref, o_ref, m_ref, l_ref, acc_ref, *, tq, tk, lam_init):
    qi = pl.program_id(1)
    qs = _stack_heads(q_ref[0], 2)
    rows = 2 * N_GROUPS * tq
    kb_diag = (qi * tq + tq - 1) // tk
    slopes = _alibi_slopes_log2()
    head = _rem_pow2(_div_pow2(_row_iota((rows, 1)), tq), N_GROUPS)
    sl = jnp.full((rows, 1), slopes[0], F32)
    for h in range(1, N_GROUPS):
        sl = jnp.where(head == h, slopes[h], sl)
    q_rel = _rem_pow2(_row_iota((rows, tk)), tq)
    k_rel = _lane_iota((rows, tk))
    bias0 = sl * (k_rel - q_rel).astype(F32)

    def block(kb, masked):
        start = pl.multiple_of(kb * tk, tk)
        s = _dot(qs, k_ref[0, :, pl.ds(start, tk)]) + bias0
        if masked:
            s = jnp.where(kb * tk + k_rel <= qi * tq + q_rel, s, NEG_BIG)
        shift = sl * (kb * tk - qi * tq).astype(F32)
        return s, shift, v_ref[0, :, pl.ds(start, tk)]

    s, shift, vb = block(kb_diag, True)
    m = jnp.max(s, axis=1, keepdims=True) + shift
    p = jnp.exp2(s - (m - shift))
    m_ref[...] = m
    l_ref[...] = jnp.sum(p, axis=1, keepdims=True)
    acc_ref[...] = _dot_nt(p.astype(BF16), vb)

    def body(i, _):
        s, shift, vb = block(kb_diag - 1 - i, False)
        m_old = m_ref[...]
        m_new = jnp.maximum(m_old, jnp.max(s, axis=1, keepdims=True) + shift)
        alpha = jnp.exp2(m_old - m_new)
        p = jnp.exp2(s - (m_new - shift))
        l_ref[...] = alpha * l_ref[...] + jnp.sum(p, axis=1, keepdims=True)
        acc_ref[...] = alpha * acc_ref[...] + _dot_nt(p.astype(BF16), vb)
        m_ref[...] = m_new
        return 0

    lax.fori_loop(0, kb_diag, body, 0)
    lam = _lam_value(lp_ref[...], lam_init)
    o_ref[0] = _diff_combine(acc_ref[...], l_ref[...], lam, og_ref[...], bd64_ref[...], lam_init, tq).astype(BF16)


def _diff_attention(q, k, v, lp, og, bd64, lam_init, tq, tk):
    n, t, w = q.shape
    rows = 2 * N_GROUPS * tq
    kv_spec = pl.BlockSpec((1, w, t), lambda i, j: (i, 0, 0))
    return pl.pallas_call(
        functools.partial(_diff_kernel, tq=tq, tk=tk, lam_init=lam_init),
        grid=(n, t // tq),
        in_specs=[pl.BlockSpec((1, tq, w), lambda i, j: (i, j, 0)), kv_spec, kv_spec,
                  _full_spec(lp.shape), _full_spec(og.shape), _full_spec(bd64.shape)],
        out_specs=pl.BlockSpec((1, tq, w), lambda i, j: (i, j, 0)),
        out_shape=jax.ShapeDtypeStruct((n, t, w), BF16),
        scratch_shapes=[pltpu.VMEM((rows, 1), F32), pltpu.VMEM((rows, 1), F32), pltpu.VMEM((rows, w), F32)],
        compiler_params=_params("arbitrary", "arbitrary"),
        name="diff_attn",
    )(q, k, v, lp, og, bd64)


def _merge_kernel(x_ref, yabm_ref, yc_ref, yd_ref, g1_ref, wg_ref, bg_ref, wb_ref, wo_ref, o_ref):
    x = x_ref[0]
    hb = _rms(x, g1_ref[...]).astype(BF16)
    ys = [yabm_ref[0, :, 0:BR_WIDTH], yabm_ref[0, :, BR_WIDTH:2 * BR_WIDTH], yc_ref[0], yd_ref[0],
          yabm_ref[0, :, 2 * BR_WIDTH:]]
    acc = None
    for b in range(5):
        gate = jax.nn.sigmoid(_dot(hb, wg_ref[:, b * D_MODEL:(b + 1) * D_MODEL]) + bg_ref[:, b * D_MODEL:(b + 1) * D_MODEL])
        term = gate * _dot(ys[b], wb_ref[b])
        acc = term if acc is None else acc + term
    o_ref[0] = x + _dot(acc.astype(BF16), wo_ref[...])


def _merge(x, yabm, yc, yd, p, tm):
    n, t, d = x.shape
    consts = [p["norm1_g"], p["w_gate"], p["b_gate"], p["w_branch"], p["w_o"]]
    seq = lambda w: pl.BlockSpec((1, tm, w), lambda i, j: (i, j, 0))
    return pl.pallas_call(
        _merge_kernel,
        grid=(n, t // tm),
        in_specs=[seq(d), seq(3 * BR_WIDTH), seq(BR_WIDTH), seq(BR_WIDTH)] + [_full_spec(c.shape) for c in consts],
        out_specs=seq(d),
        out_shape=jax.ShapeDtypeStruct((n, t, d), F32),
        compiler_params=_params("arbitrary", "arbitrary"),
        name="merge",
    )(x, yabm, yc, yd, *consts)


FFN_COLS = 1408


def _ffn_kernel(x_ref, g2_ref, wup_ref, cw_ref, cb_ref, wdn_ref, o_ref, st_ref, carry_ref, *, tm):
    t = pl.program_id(1)
    x = x_ref[0]
    hb = _rms(x, g2_ref[...]).astype(BF16)

    @pl.when(t == 0)
    def _():
        carry_ref[...] = jnp.zeros_like(carry_ref)

    row = _row_iota((tm, FFN_COLS))

    def conv(lo):
        u = _dot(hb, wup_ref[:, lo:lo + FFN_COLS])
        prev = carry_ref[:, lo:lo + FFN_COLS]
        p1, p2 = prev[7:8], prev[6:7]
        s1 = jnp.where(row == 0, p1, pltpu.roll(u, 1, axis=0))
        s2 = jnp.where(row == 0, p2, jnp.where(row == 1, p1, pltpu.roll(u, 2, axis=0)))
        carry_ref[:, lo:lo + FFN_COLS] = u[tm - 8:tm]
        st_ref[0, :, lo:lo + FFN_COLS] = u[tm - 2:tm]
        return (cw_ref[0:1, lo:lo + FFN_COLS] * s2 + cw_ref[1:2, lo:lo + FFN_COLS] * s1
                + cw_ref[2:3, lo:lo + FFN_COLS] * u + cb_ref[:, lo:lo + FFN_COLS])

    acc = x
    for j in range(D_FF // FFN_COLS):
        gate = conv(j * FFN_COLS)
        val = conv(D_FF + j * FFN_COLS)
        act = (gate * jax.nn.sigmoid(gate) * val).astype(BF16)
        acc = acc + _dot(act, wdn_ref[j * FFN_COLS:(j + 1) * FFN_COLS, :])
    o_ref[0] = acc


def _ffn(x, p, tm):
    n, t, d = x.shape
    consts = [p["norm2_g"], p["w_up"], p["conv_ffn_w"], p["conv_ffn_b"], p["w_down"]]
    seq = pl.BlockSpec((1, tm, d), lambda i, j: (i, j, 0))
    return pl.pallas_call(
        functools.partial(_ffn_kernel, tm=tm),
        grid=(n, t // tm),
        in_specs=[seq] + [_full_spec(c.shape) for c in consts],
        out_specs=[seq, pl.BlockSpec((1, 2, 2 * D_FF), lambda i, j: (i, 0, 0))],
        out_shape=[jax.ShapeDtypeStruct((n, t, d), F32), jax.ShapeDtypeStruct((n, 2, 2 * D_FF), F32)],
        scratch_shapes=[pltpu.VMEM((8, 2 * D_FF), F32)],
        compiler_params=_params("arbitrary", "arbitrary"),
        name="conv_ffn",
    )(x, *consts)


def _ffn_step_kernel(x_ref, p0_ref, p1_ref, g2_ref, wup_ref, cw_ref, cb_ref, wdn_ref, o_ref, u_ref):
    x = x_ref[...]
    hb = _rms(x, g2_ref[...]).astype(BF16)

    def conv(lo):
        u = _dot(hb, wup_ref[:, lo:lo + FFN_COLS])
        u_ref[:, lo:lo + FFN_COLS] = u
        return (cw_ref[0:1, lo:lo + FFN_COLS] * p0_ref[:, lo:lo + FFN_COLS]
                + cw_ref[1:2, lo:lo + FFN_COLS] * p1_ref[:, lo:lo + FFN_COLS]
                + cw_ref[2:3, lo:lo + FFN_COLS] * u + cb_ref[:, lo:lo + FFN_COLS])

    acc = x
    for j in range(D_FF // FFN_COLS):
        gate = conv(j * FFN_COLS)
        val = conv(D_FF + j * FFN_COLS)
        act = (gate * jax.nn.sigmoid(gate) * val).astype(BF16)
        acc = acc + _dot(act, wdn_ref[j * FFN_COLS:(j + 1) * FFN_COLS, :])
    o_ref[...] = acc


def _ffn_step(x, prev0, prev1, p):
    m, d = x.shape
    consts = [p["norm2_g"], p["w_up"], p["conv_ffn_w"], p["conv_ffn_b"], p["w_down"]]
    args = [x, prev0, prev1] + consts
    return pl.pallas_call(
        _ffn_step_kernel,
        grid=(1,),
        in_specs=[_full_spec(a.shape) for a in args],
        out_specs=[_full_spec((m, d)), _full_spec((m, 2 * D_FF))],
        out_shape=[jax.ShapeDtypeStruct((m, d), F32), jax.ShapeDtypeStruct((m, 2 * D_FF), F32)],
        compiler_params=_params("arbitrary"),
        name="conv_ffn_step",
    )(*args)


def _inproj_step_kernel(x_ref, c0_ref, c1_ref, g1_ref, win_ref, cw_ref, lng_ref, lnb_ref, ws_ref, gb_ref,
                        dqg_ref, dkg_ref, mqg_ref, bd32_ref, bd64_ref,
                        yab_ref, q_ref, ksb_ref, vsb_ref, kdf_ref, vdf_ref, vn_ref, ach_ref):
    hb = _rms(x_ref[...], g1_ref[...]).astype(BF16)

    def proj(lo, hi):
        return _dot(hb, win_ref[:, lo:hi])

    pa = proj(0, 3 * BR_WIDTH)
    ach = pa[:, BR_WIDTH:2 * BR_WIDTH] * pa[:, 2 * BR_WIDTH:]
    cw = cw_ref[...]
    yab_ref[:, 0:BR_WIDTH] = (pa[:, :BR_WIDTH] * (cw[0:1] * c0_ref[...] + cw[1:2] * c1_ref[...] + cw[2:3] * ach)).astype(BF16)
    ach_ref[...] = ach

    pg = proj(3 * BR_WIDTH, 5 * BR_WIDTH)
    vn = _layer_norm(pg[:, BR_WIDTH:], lng_ref[...], lnb_ref[...])
    vn_ref[...] = vn
    lane = _lane_iota((1, BR_WIDTH))
    w00 = jnp.zeros((1, BR_WIDTH), F32)
    for g in range(N_GROUPS):
        w00 = jnp.where((lane >= HEAD_DIM * g) & (lane < HEAD_DIM * (g + 1)), ws_ref[g, 0:1, 0:1], w00)
    yab_ref[:, BR_WIDTH:] = (pg[:, :BR_WIDTH] * (w00 * vn + gb_ref[0:1, :])).astype(BF16)

    ps = proj(5 * BR_WIDTH, 8 * BR_WIDTH)
    q_ref[:, 0:BR_WIDTH] = (ps[:, :BR_WIDTH] * (HEAD_DIM ** -0.5)).astype(BF16)
    ksb_ref[...] = ps[:, BR_WIDTH:2 * BR_WIDTH]
    vsb_ref[...] = ps[:, 2 * BR_WIDTH:]

    pd = proj(8 * BR_WIDTH, 11 * BR_WIDTH)
    bd32 = bd32_ref[...]
    qd = _group_rms(pd[:, :BR_WIDTH], dqg_ref[...], bd32, DIFF_SUB)
    q_ref[:, BR_WIDTH:2 * BR_WIDTH] = (qd * (DIFF_SUB ** -0.5 * LOG2E)).astype(BF16)
    kdf_ref[...] = _group_rms(pd[:, BR_WIDTH:2 * BR_WIDTH], dkg_ref[...], bd32, DIFF_SUB)
    vdf_ref[...] = pd[:, 2 * BR_WIDTH:]

    qm = _group_rms(proj(11 * BR_WIDTH, 12 * BR_WIDTH), mqg_ref[...], bd64_ref[...], HEAD_DIM)
    q_ref[:, 2 * BR_WIDTH:] = (qm * (HEAD_DIM ** -0.5 * LOG2E)).astype(BF16)


def _inproj_step(x, c0, c1, p):
    m, d = x.shape
    consts = [p["norm1_g"], p["w_in"], p["conv_a_w"], p["gmlp_ln_g"], p["gmlp_ln_b"], p["gmlp_ws"],
              p["gmlp_b_full"], p["diff_qn_g"], p["diff_kn_g"], p["mem_qn_g"], p["bd32"], p["bd64"]]
    args = [x, c0, c1] + consts
    fl = jax.ShapeDtypeStruct((m, BR_WIDTH), F32)
    out_shape = [jax.ShapeDtypeStruct((m, 2 * BR_WIDTH), BF16), jax.ShapeDtypeStruct((m, 3 * BR_WIDTH), BF16)] + [fl] * 6
    return pl.pallas_call(
        _inproj_step_kernel,
        grid=(1,),
        in_specs=[_full_spec(a.shape) for a in args],
        out_specs=[_full_spec(s.shape) for s in out_shape],
        out_shape=out_shape,
        compiler_params=_params("arbitrary"),
        name="in_proj_step",
    )(*args)


def _decode_kernel(pt_ref, q_ref, kn_ref, vn_ref, mk_ref, mv_ref, lp_ref, og_ref, bd64_ref, ll_ref, *rest,
                   n_pages, page, lam_init):
    del pt_ref
    pages = rest[:4 * n_pages]
    o_ref = rest[4 * n_pages]
    ksb, vsb, kdf, vdf = (pages[i * n_pages:(i + 1) * n_pages] for i in range(4))
    rows = 2 * N_GROUPS
    q = jnp.broadcast_to(q_ref[0].astype(F32), (rows, 3 * BR_WIDTH))
    row = _row_iota((rows, BR_WIDTH))
    lane = _lane_iota((rows, BR_WIDTH))
    lane_head = _div_pow2(lane, HEAD_DIM)

    def masked_q(col, mask):
        return jnp.where(mask, q[:, col * BR_WIDTH:(col + 1) * BR_WIDTH], 0.0).astype(BF16)

    head_mask = lane_head == row
    qs = masked_q(0, head_mask)
    ll = ll_ref[...]
    carry = jnp.zeros((rows, 1), F32)
    acc = jnp.zeros((rows, BR_WIDTH), F32)
    for pg in reversed(range(n_pages)):
        z = _dot(qs, ksb[pg][0, 0].astype(BF16))
        lb, l1m = _log_sigmoid_pair(z)
        hi, lo = _split_bf16(l1m)
        between = _dot(jnp.concatenate([hi, lo], axis=1), ll)
        a = jnp.exp(lb + between + carry)
        acc = acc + _dot_nt(a.astype(BF16), vsb[pg][0, 0].astype(BF16))
        carry = carry + jnp.sum(l1m, axis=1, keepdims=True)
    y_c = jnp.sum(jnp.where(head_mask, acc, 0.0), axis=0, keepdims=True)

    half_mask = lane_head == _rem_pow2(row, N_GROUPS)
    sub_mask = half_mask & (_rem_pow2(_div_pow2(lane, DIFF_SUB), 2) == _div_pow2(row, N_GROUPS))
    qd = masked_q(1, sub_mask)
    slopes = _alibi_slopes_log2()
    head8 = _rem_pow2(_row_iota((rows, 1)), N_GROUPS)
    sl = jnp.full((rows, 1), slopes[0], F32)
    for h in range(1, N_GROUPS):
        sl = jnp.where(head8 == h, slopes[h], sl)
    past = n_pages * page
    k_rel = _lane_iota((rows, page)).astype(F32)
    scores = []
    for pg in range(n_pages):
        s = _dot(qd, kdf[pg][0, 0].astype(BF16))
        scores.append(s + sl * (k_rel + float(pg * page - past)))
    round16 = lambda a: jnp.broadcast_to(a, (rows, BR_WIDTH)).astype(BF16).astype(F32)
    s_new = jnp.sum(qd.astype(F32) * round16(kn_ref[0]), axis=1, keepdims=True)
    m = s_new
    for s in scores:
        m = jnp.maximum(m, jnp.max(s, axis=1, keepdims=True))
    p_new = jnp.exp2(s_new - m)
    l = p_new
    acc = p_new.astype(BF16).astype(F32) * round16(vn_ref[0])
    for pg in range(n_pages):
        p = jnp.exp2(scores[pg] - m)
        l = l + jnp.sum(p, axis=1, keepdims=True)
        acc = acc + _dot_nt(p.astype(BF16), vdf[pg][0, 0].astype(BF16))
    y = acc / l
    lam = _lam_value(lp_ref[...], lam_init)
    signed = jnp.where(row < N_GROUPS, y, -lam * y)
    y_d = jnp.sum(jnp.where(half_mask, signed, 0.0), axis=0, keepdims=True)
    y_d = _group_rms(jnp.broadcast_to(y_d, (rows, BR_WIDTH)), og_ref[...], bd64_ref[...], HEAD_DIM)[0:1]
    y_d = y_d * (1.0 - lam_init)

    qm = masked_q(2, head_mask)
    s = _dot(qm, mk_ref[0].astype(BF16))
    p = jnp.exp2(s - jnp.max(s, axis=1, keepdims=True))
    o = _dot_nt(p.astype(BF16), mv_ref[0].astype(BF16)) / jnp.sum(p, axis=1, keepdims=True)
    y_m = jnp.sum(jnp.where(head_mask, o, 0.0), axis=0, keepdims=True)

    o_ref[0, :, 0:BR_WIDTH] = y_c.astype(BF16)
    o_ref[0, :, BR_WIDTH:2 * BR_WIDTH] = y_d.astype(BF16)
    o_ref[0, :, 2 * BR_WIDTH:] = y_m.astype(BF16)


def _decode_attention(layer, page_table, q, k_new, v_new, caches, mem_k, mem_v, p, lam_init):
    b = q.shape[0]
    n_pages = page_table.shape[1]
    page = caches[0].shape[3]
    ll = _suffix_ones(page)
    row_spec = lambda w: pl.BlockSpec((1, 1, w), lambda i, pt: (i, 0, 0))
    mem_spec = pl.BlockSpec((None, 1) + mem_k.shape[2:], lambda i, pt: (layer, i, 0, 0))
    const = lambda a: pl.BlockSpec(a.shape, lambda i, pt: (0,) * a.ndim)
    page_specs, page_args = [], []
    for cache in caches:
        for j in range(n_pages):
            page_specs.append(pl.BlockSpec((1, 1, BR_WIDTH, page), lambda i, pt, j=j: (layer, pt[i, j], 0, 0)))
            page_args.append(cache)
    consts = [p["diff_lambda"], p["diff_out_g"], p["bd64"], ll]
    grid_spec = pltpu.PrefetchScalarGridSpec(
        num_scalar_prefetch=1,
        grid=(b,),
        in_specs=[row_spec(3 * BR_WIDTH), row_spec(BR_WIDTH), row_spec(BR_WIDTH), mem_spec, mem_spec]
        + [const(c) for c in consts] + page_specs,
        out_specs=row_spec(3 * BR_WIDTH),
    )
    return pl.pallas_call(
        functools.partial(_decode_kernel, n_pages=n_pages, page=page, lam_init=lam_init),
        grid_spec=grid_spec,
        out_shape=jax.ShapeDtypeStruct((b, 1, 3 * BR_WIDTH), BF16),
        compiler_params=_params("arbitrary"),
        name="decode_attn",
    )(page_table, q, k_new, v_new, mem_k, mem_v, *consts, *page_args)


PROMPT_TILE = 512
ATTN_TQ = 128
ATTN_TK = 256


def _layer_params(l, w):
    row = lambda a: a[l].reshape(1, -1).astype(F32)
    tile4 = lambda a: jnp.tile(a[l].astype(F32), N_GROUPS).reshape(1, -1)
    col4 = lambda a: jnp.tile(a[l].astype(F32), N_GROUPS).reshape(-1, 1)
    w_in = w["w_in"][l].astype(BF16)
    return dict(
        norm1_g=row(w["norm1_g"]), w_in=w_in, conv_a_w=w["conv_a_w"][l].astype(F32),
        w_sb_kv_t=w_in[:, 6 * BR_WIDTH:8 * BR_WIDTH].T, w_df_kv_t=w_in[:, 9 * BR_WIDTH:11 * BR_WIDTH].T,
        diff_kn_g_col=col4(w["diff_kn_g"]), mem_kn_g_col=col4(w["mem_kn_g"]),
        w_mem_kv_t=w["w_mem_kv"][l].astype(BF16).T,
        gmlp_ln_g=row(w["gmlp_ln_g"]), gmlp_ln_b=row(w["gmlp_ln_b"]), gmlp_ws=w["gmlp_ws"][l].astype(F32),
        gmlp_b_full=jnp.repeat(w["gmlp_b"][l].astype(F32).T, HEAD_DIM, axis=1),
        diff_qn_g=tile4(w["diff_qn_g"]), diff_kn_g=tile4(w["diff_kn_g"]), diff_out_g=tile4(w["diff_out_g"]),
        diff_lambda=w["diff_lambda"][l].astype(F32),
        mem_norm_g=row(w["mem_norm_g"]), mem_qn_g=tile4(w["mem_qn_g"]),
        w_branch=w["w_branch"][l].astype(BF16), w_gate=w["w_gate"][l].astype(BF16), b_gate=row(w["b_gate"]),
        w_o=w["w_o"][l].astype(BF16), norm2_g=row(w["norm2_g"]), w_up=w["w_up"][l].astype(BF16),
        conv_ffn_w=w["conv_ffn_w"][l].astype(F32), conv_ffn_b=row(w["conv_ffn_b"]), w_down=w["w_down"][l].astype(BF16),
        bd32=_block_ones(DIFF_SUB), bd64=_block_ones(HEAD_DIM),
    )


def _prompt_layer(x, mem, p, lam_init):
    n, t, _ = x.shape
    tm = min(PROMPT_TILE, t)
    tq, tk = min(ATTN_TQ, t), min(ATTN_TK, t)
    mk, mv, mk16, mv16 = _mem_kv(mem, p["mem_norm_g"], p["w_mem_kv_t"], p["mem_kn_g_col"], p["bd64"])
    (yabm, qsb, ksb16, vsb16, qdf, kdf16, vdf16, ksb, vsb, kdf, vdf, ca) = _inproj(x, p, mk16, mv16, tm)
    yc = _sb_attention(qsb, ksb16, vsb16, tq, tk)
    yd = _diff_attention(qdf, kdf16, vdf16, p["diff_lambda"], p["diff_out_g"], p["bd64"], lam_init, tq, tk)
    x1 = _merge(x, yabm, yc, yd, p, tm)
    x2, cf = _ffn(x1, p, tm)
    return x2, ca, ksb, vsb, kdf, vdf, mk, mv, cf


def _sample_layer(l, x, conv_a, conv_ffn, page_table, caches, mem_k, mem_v, p, lam_init):
    b = x.shape[0]
    yab, q, ksb, vsb, kdf, vdf, vn, ach = _inproj_step(x, conv_a[:, 0], conv_a[:, 1], p)
    row3 = lambda a: a.reshape(b, 1, -1)
    ycdm = _decode_attention(l, page_table, row3(q), row3(kdf), row3(vdf), caches, mem_k, mem_v, p, lam_init)
    yabm = jnp.concatenate([yab, ycdm[:, 0, 2 * BR_WIDTH:]], axis=1)
    x1 = _merge(x[None], yabm[None], ycdm[None, :, 0, 0:BR_WIDTH], ycdm[None, :, 0, BR_WIDTH:2 * BR_WIDTH], p, b)[0]
    x2, u = _ffn_step(x1, conv_ffn[:, 0], conv_ffn[:, 1], p)
    ca_new = jnp.stack([conv_a[:, 1], ach], axis=1)
    cf_new = jnp.stack([conv_ffn[:, 1], u], axis=1)
    return x2, ca_new, vn, ksb, vsb, kdf, vdf, cf_new


def kernel(x_prompt, x_sample, state_conv_a, cache_k_sb, cache_v_sb, cache_k_diff, cache_v_diff, cache_mem_k,
           cache_mem_v, state_conv_ffn, page_table, mem_prompt, norm1_g, w_in, conv_a_w, gmlp_ln_g, gmlp_ln_b,
           gmlp_ws, gmlp_b, diff_qn_g, diff_kn_g, diff_lambda, diff_out_g, mem_norm_g, w_mem_kv, mem_qn_g,
           mem_kn_g, w_branch, w_gate, b_gate, w_o, norm2_g, w_up, conv_ffn_w, conv_ffn_b, w_down):
    w = dict(norm1_g=norm1_g, w_in=w_in, conv_a_w=conv_a_w, gmlp_ln_g=gmlp_ln_g, gmlp_ln_b=gmlp_ln_b,
             gmlp_ws=gmlp_ws, gmlp_b=gmlp_b, diff_qn_g=diff_qn_g, diff_kn_g=diff_kn_g, diff_lambda=diff_lambda,
             diff_out_g=diff_out_g, mem_norm_g=mem_norm_g, w_mem_kv=w_mem_kv, mem_qn_g=mem_qn_g,
             mem_kn_g=mem_kn_g, w_branch=w_branch, w_gate=w_gate, b_gate=b_gate, w_o=w_o, norm2_g=norm2_g,
             w_up=w_up, conv_ffn_w=conv_ffn_w, conv_ffn_b=conv_ffn_b, w_down=w_down)
    depth = w_in.shape[0]
    n_p, t_p, _ = x_prompt.shape
    n_s = x_sample.shape[0]
    flat = lambda c: jnp.transpose(c, (0, 1, 3, 4, 2)).reshape(c.shape[:2] + (BR_WIDTH, c.shape[2]))
    caches = [flat(cache_k_sb), flat(cache_v_sb), flat(cache_k_diff), flat(cache_v_diff)]
    mem_k, mem_v = flat(cache_mem_k), flat(cache_mem_v)
    xp, xs = x_prompt, x_sample[:, 0]
    outs_p, outs_s = [], []
    for l in range(depth):
        p = _layer_params(l, w)
        lam_init = 0.8 - 0.6 * math.exp(-0.3 * l)
        xp, *rest_p = _prompt_layer(xp, mem_prompt, p, lam_init)
        outs_p.append(rest_p)
        xs, *rest_s = _sample_layer(l, xs, state_conv_a[l], state_conv_ffn[l], page_table, caches, mem_k, mem_v,
                                    p, lam_init)
        outs_s.append(rest_s)
    heads = lambda a: a.reshape(a.shape[:-1] + (N_GROUPS, HEAD_DIM))
    stack_p = lambda i: jnp.stack([o[i] for o in outs_p], axis=0)
    stack_s = lambda i: jnp.stack([o[i] for o in outs_s], axis=0)
    step = lambda a: a.reshape(depth, n_s, 1, -1)

    def heads_t(a):
        d0, n, _, t = a.shape
        return jnp.transpose(a.reshape(d0, n, N_GROUPS, HEAD_DIM, t), (0, 1, 4, 2, 3))

    return (xp, xs[:, None, :],
            stack_p(0), stack_s(0), step(stack_s(1)),
            heads_t(stack_p(1)), heads_t(stack_p(2)), heads(step(stack_s(2))), heads(step(stack_s(3))),
            heads_t(stack_p(3)), heads_t(stack_p(4)), heads(step(stack_s(4))), heads(step(stack_s(5))),
            heads_t(stack_p(5)), heads_t(stack_p(6)),
            stack_p(7), stack_s(6))
```

```python
import functools
import math

import numpy as np
import jax
import jax.numpy as jnp
from jax import lax
from jax.experimental import pallas as pl
from jax.experimental.pallas import tpu as pltpu

D_MODEL = 1024
HEAD_DIM = 64
BR_WIDTH = 256
N_GROUPS = 4
DIFF_SUB = 32
CHUNK = 128
D_FF = 2816
EPS = 1e-6
NEG_BIG = -1e30
LOG2E = 1.4426950408889634
VMEM_LIMIT_BYTES = 56 * 1024 * 1024

F32 = jnp.float32
BF16 = jnp.bfloat16


def _dot(a, b):
    return jnp.dot(a, b, preferred_element_type=F32)


def _dot_nt(a, b):
    return lax.dot_general(a, b, (((1,), (1,)), ((), ())), preferred_element_type=F32)


def _split_bf16(x):
    hi = x.astype(BF16)
    lo = (x - hi.astype(F32)).astype(BF16)
    return hi, lo


def _dot_split(x, w):
    hi, lo = _split_bf16(x)
    return _dot(hi, w) + _dot(lo, w)


def _rms(x, g):
    ms = jnp.mean(x * x, axis=-1, keepdims=True)
    return x * lax.rsqrt(ms + EPS) * g


def _group_rms(z, g, ones_bd, group):
    ms = _dot_split(z * z, ones_bd) * (1.0 / group)
    return z * lax.rsqrt(ms + EPS) * g


def _lane_iota(shape):
    return lax.broadcasted_iota(jnp.int32, shape, len(shape) - 1)


def _row_iota(shape):
    return lax.broadcasted_iota(jnp.int32, shape, len(shape) - 2)


def _div_pow2(x, d):
    assert d & (d - 1) == 0
    return lax.shift_right_logical(x, d.bit_length() - 1)


def _rem_pow2(x, d):
    assert d & (d - 1) == 0
    return x & (d - 1)


def _head_select(parts):
    lane = _lane_iota(parts[0].shape)
    out = parts[0]
    for h in range(1, N_GROUPS):
        out = jnp.where(lane >= HEAD_DIM * h, parts[h], out)
    return out


def _stack_heads(q, n_sub):
    lane = _lane_iota(q.shape)
    width = HEAD_DIM // n_sub
    zero = jnp.zeros_like(q)
    parts = []
    for c in range(n_sub):
        for h in range(N_GROUPS):
            lo = HEAD_DIM * h + width * c
            parts.append(jnp.where((lane >= lo) & (lane < lo + width), q, zero))
    return jnp.concatenate(parts, axis=0)


def _log_sigmoid_pair(z):
    lb = jnp.minimum(z, 0.0) - jnp.log(1.0 + jnp.exp(-jnp.abs(z)))
    return lb, lb - z


def _lam_value(lp, lam_init):
    a = jnp.sum(lp[0:1] * lp[1:2], axis=1, keepdims=True)
    b = jnp.sum(lp[2:3] * lp[3:4], axis=1, keepdims=True)
    return jnp.exp(a) - jnp.exp(b) + lam_init


def _block_ones(group):
    i = np.arange(BR_WIDTH)
    return jnp.asarray((i[:, None] // group) == (i[None, :] // group), dtype=BF16)


def _suffix_ones(n):
    i = np.arange(n)
    l = (i[:, None] > i[None, :])
    return jnp.asarray(np.concatenate([l, l], axis=0), dtype=BF16)


def _alibi_slopes_log2():
    return [LOG2E * 2.0 ** (-8.0 * (h + 1) / N_GROUPS) for h in range(N_GROUPS)]


def _full_spec(shape):
    nd = len(shape)
    return pl.BlockSpec(shape, lambda *_: (0,) * nd, pipeline_mode=pl.Buffered(1))


def _params(*sem):
    return pltpu.CompilerParams(dimension_semantics=sem, vmem_limit_bytes=VMEM_LIMIT_BYTES)


def _group_rms_t(zt, g_col, ones_bd, group):
    hi, lo = _split_bf16(zt * zt)
    ms = (_dot(ones_bd, hi) + _dot(ones_bd, lo)) * (1.0 / group)
    return zt * lax.rsqrt(ms + EPS) * g_col


def _memkv_kernel(mem_ref, g_ref, wt_ref, kg_ref, bd64_ref, k_ref, v_ref, k16_ref, v16_ref):
    hb = _rms(mem_ref[0], g_ref[...]).astype(BF16)
    kvt = _dot_nt(wt_ref[...], hb)
    k = _group_rms_t(kvt[:BR_WIDTH], kg_ref[...], bd64_ref[...], HEAD_DIM)
    v = kvt[BR_WIDTH:]
    k_ref[0] = k
    v_ref[0] = v
    k16_ref[0] = k.astype(BF16)
    v16_ref[0] = v.astype(BF16)


def _mem_kv(mem, g, w, kg, bd64):
    n, m, d = mem.shape
    blk = pl.BlockSpec((1, BR_WIDTH, m), lambda i: (i, 0, 0))
    return pl.pallas_call(
        _memkv_kernel,
        grid=(n,),
        in_specs=[pl.BlockSpec((1, m, d), lambda i: (i, 0, 0)), _full_spec(g.shape), _full_spec(w.shape),
                  _full_spec(kg.shape), _full_spec(bd64.shape)],
        out_specs=[blk, blk, blk, blk],
        out_shape=[jax.ShapeDtypeStruct((n, BR_WIDTH, m), F32)] * 2
        + [jax.ShapeDtypeStruct((n, BR_WIDTH, m), BF16)] * 2,
        compiler_params=_params("arbitrary"),
        name="mem_kv",
    )(mem, g, w, kg, bd64)


def _layer_norm(x, g, b):
    mu = jnp.mean(x, axis=-1, keepdims=True)
    xc = x - mu
    return xc * lax.rsqrt(jnp.mean(xc * xc, axis=-1, keepdims=True) + EPS) * g + b


def _mem_attention(qm, mkt, mvt):
    r = qm.shape[0]
    s = _dot(_stack_heads(qm, 1), mkt)
    p = jnp.exp2(s - jnp.max(s, axis=-1, keepdims=True))
    l = jnp.sum(p, axis=-1, keepdims=True)
    o = _dot_nt(p.astype(BF16), mvt) / l
    return _head_select([o[h * r:(h + 1) * r] for h in range(N_GROUPS)])


def _inproj_kernel(x_ref, g1_ref, win_ref, wsbt_ref, wdft_ref, cw_ref, lng_ref, lnb_ref, ws_ref, gb_ref, dqg_ref,
                   dkg_ref, dkgr_ref, mqg_ref, bd32_ref, bd64_ref, mk_ref, mv_ref,
                   yabm_ref, qsb_ref, ksb16_ref, vsb16_ref, qdf_ref, kdf16_ref, vdf16_ref,
                   ksb_ref, vsb_ref, kdf_ref, vdf_ref, ca_ref, carry_ref, *, tm):
    t = pl.program_id(1)
    hb = _rms(x_ref[0], g1_ref[...]).astype(BF16)

    def proj(lo, hi):
        return _dot(hb, win_ref[:, lo:hi])

    pa = proj(0, 3 * BR_WIDTH)
    ach = pa[:, BR_WIDTH:2 * BR_WIDTH] * pa[:, 2 * BR_WIDTH:]

    @pl.when(t == 0)
    def _():
        carry_ref[...] = jnp.zeros_like(carry_ref)

    prev = carry_ref[...]
    p1, p2 = prev[7:8], prev[6:7]
    row = _row_iota(ach.shape)
    s1 = jnp.where(row == 0, p1, pltpu.roll(ach, 1, axis=0))
    s2 = jnp.where(row == 0, p2, jnp.where(row == 1, p1, pltpu.roll(ach, 2, axis=0)))
    cw = cw_ref[...]
    yabm_ref[0, :, 0:BR_WIDTH] = (pa[:, :BR_WIDTH] * (cw[0:1] * s2 + cw[1:2] * s1 + cw[2:3] * ach)).astype(BF16)
    carry_ref[...] = ach[tm - 8:tm]
    ca_ref[0] = ach[tm - 2:tm]

    pg = proj(3 * BR_WIDTH, 5 * BR_WIDTH)
    vn = _layer_norm(pg[:, BR_WIDTH:], lng_ref[...], lnb_ref[...])
    tril = _row_iota((CHUNK, CHUNK)) >= _lane_iota((CHUNK, CHUNK))
    wsm = [jnp.where(tril, ws_ref[g], 0.0).astype(BF16) for g in range(N_GROUPS)]
    gb = gb_ref[...]
    for c in range(tm // CHUNK):
        vc = vn[c * CHUNK:(c + 1) * CHUNK].astype(BF16)
        mixed = _head_select([_dot(wsm[g], vc) for g in range(N_GROUPS)])
        yabm_ref[0, c * CHUNK:(c + 1) * CHUNK, BR_WIDTH:2 * BR_WIDTH] = (
            pg[c * CHUNK:(c + 1) * CHUNK, :BR_WIDTH] * (mixed + gb)).astype(BF16)

    qsb_ref[0] = (proj(5 * BR_WIDTH, 6 * BR_WIDTH) * (HEAD_DIM ** -0.5)).astype(BF16)
    kvt = _dot_nt(wsbt_ref[...], hb)
    ksb_ref[0] = kvt[:BR_WIDTH]
    vsb_ref[0] = kvt[BR_WIDTH:]
    ksb16_ref[0] = kvt[:BR_WIDTH].astype(BF16)
    vsb16_ref[0] = kvt[BR_WIDTH:].astype(BF16)

    bd32 = bd32_ref[...]
    qd = _group_rms(proj(8 * BR_WIDTH, 9 * BR_WIDTH), dqg_ref[...], bd32, DIFF_SUB)
    qdf_ref[0] = (qd * (DIFF_SUB ** -0.5 * LOG2E)).astype(BF16)
    kvt = _dot_nt(wdft_ref[...], hb)
    kd = _group_rms_t(kvt[:BR_WIDTH], dkg_ref[...], bd32, DIFF_SUB)
    kdf_ref[0] = kd
    vdf_ref[0] = kvt[BR_WIDTH:]
    kdf16_ref[0] = _group_rms(proj(9 * BR_WIDTH, 10 * BR_WIDTH), dkgr_ref[...], bd32, DIFF_SUB).astype(BF16)
    vdf16_ref[0] = kvt[BR_WIDTH:].astype(BF16)

    qm = _group_rms(proj(11 * BR_WIDTH, 12 * BR_WIDTH), mqg_ref[...], bd64_ref[...], HEAD_DIM)
    qm = (qm * (HEAD_DIM ** -0.5 * LOG2E)).astype(BF16)
    yabm_ref[0, :, 2 * BR_WIDTH:] = _mem_attention(qm, mk_ref[0], mv_ref[0]).astype(BF16)


def _inproj(x, p, mk16, mv16, tm):
    n, t, d = x.shape
    consts = [p["norm1_g"], p["w_in"], p["w_sb_kv_t"], p["w_df_kv_t"], p["conv_a_w"], p["gmlp_ln_g"], p["gmlp_ln_b"],
              p["gmlp_ws"], p["gmlp_b_full"], p["diff_qn_g"], p["diff_kn_g_col"], p["diff_kn_g"], p["mem_qn_g"],
              p["bd32"], p["bd64"]]
    seq = lambda w: pl.BlockSpec((1, tm, w), lambda i, j: (i, j, 0))
    seq_t = pl.BlockSpec((1, BR_WIDTH, tm), lambda i, j: (i, 0, j))
    mem_spec = pl.BlockSpec((1,) + mk16.shape[1:], lambda i, j: (i, 0, 0))
    bf = lambda w: jax.ShapeDtypeStruct((n, t, w), BF16)
    bf_t = jax.ShapeDtypeStruct((n, BR_WIDTH, t), BF16)
    fl_t = jax.ShapeDtypeStruct((n, BR_WIDTH, t), F32)
    return pl.pallas_call(
        functools.partial(_inproj_kernel, tm=tm),
        grid=(n, t // tm),
        in_specs=[seq(d)] + [_full_spec(c.shape) for c in consts] + [mem_spec, mem_spec],
        out_specs=[seq(3 * BR_WIDTH), seq(BR_WIDTH), seq_t, seq_t, seq(BR_WIDTH), seq(BR_WIDTH)] + [seq_t] * 5
        + [pl.BlockSpec((1, 2, BR_WIDTH), lambda i, j: (i, 0, 0))],
        out_shape=[bf(3 * BR_WIDTH), bf(BR_WIDTH), bf_t, bf_t, bf(BR_WIDTH), bf(BR_WIDTH), bf_t] + [fl_t] * 4
        + [jax.ShapeDtypeStruct((n, 2, BR_WIDTH), F32)],
        scratch_shapes=[pltpu.VMEM((8, BR_WIDTH), F32)],
        compiler_params=_params("arbitrary", "arbitrary"),
        name="in_proj",
    )(x, *consts, mk16, mv16)


def _sb_kernel(q_ref, k_ref, v_ref, ll_ref, o_ref, acc_ref, *, tq, tk):
    qi = pl.program_id(1)
    qs = _stack_heads(q_ref[0], 1)
    rows = N_GROUPS * tq
    kb_diag = (qi * tq + tq - 1) // tk
    ll = ll_ref[...]

    def block(kb, carry, mask):
        start = pl.multiple_of(kb * tk, tk)
        z = _dot(qs, k_ref[0, :, pl.ds(start, tk)])
        lb, l1m = _log_sigmoid_pair(z)
        if mask is not None:
            l1m = jnp.where(mask, l1m, 0.0)
        hi, lo = _split_bf16(l1m)
        between = _dot(jnp.concatenate([hi, lo], axis=1), ll)
        a = jnp.exp(lb + between + carry)
        if mask is not None:
            a = jnp.where(mask, a, 0.0)
        pv = _dot_nt(a.astype(BF16), v_ref[0, :, pl.ds(start, tk)])
        return pv, carry + jnp.sum(l1m, axis=1, keepdims=True)

    q_pos = qi * tq + _rem_pow2(_row_iota((rows, tk)), tq)
    k_pos = kb_diag * tk + _lane_iota((rows, tk))
    pv, carry = block(kb_diag, jnp.zeros((rows, 1), F32), k_pos < q_pos)
    acc_ref[...] = pv

    def body(i, carry):
        pv, carry = block(kb_diag - 1 - i, carry, None)
        acc_ref[...] += pv
        return carry

    lax.fori_loop(0, kb_diag, body, carry)
    acc = acc_ref[...]
    o_ref[0] = _head_select([acc[h * tq:(h + 1) * tq] for h in range(N_GROUPS)]).astype(BF16)


def _sb_attention(q, k, v, tq, tk):
    n, t, w = q.shape
    ll = _suffix_ones(tk)
    kv_spec = pl.BlockSpec((1, w, t), lambda i, j: (i, 0, 0))
    return pl.pallas_call(
        functools.partial(_sb_kernel, tq=tq, tk=tk),
        grid=(n, t // tq),
        in_specs=[pl.BlockSpec((1, tq, w), lambda i, j: (i, j, 0)), kv_spec, kv_spec, _full_spec(ll.shape)],
        out_specs=pl.BlockSpec((1, tq, w), lambda i, j: (i, j, 0)),
        out_shape=jax.ShapeDtypeStruct((n, t, w), BF16),
        scratch_shapes=[pltpu.VMEM((N_GROUPS * tq, w), F32)],
        compiler_params=_params("arbitrary", "arbitrary"),
        name="sb_attn",
    )(q, k, v, ll)


def _diff_kernel(q_ref, k_ref, v_ref, lp_ref, og_ref, bd64_ref, eye_ref, o_ref,
                 qm_ref, bias_ref, m_ref, l_ref, acc_ref, *, tb, lam_init):
    qi = pl.program_id(1)
    slopes = _alibi_slopes_log2()
    rel = _row_iota((tb, tb)) - _lane_iota((tb, tb))

    @pl.when(qi == 0)
    def _():
        for h in range(N_GROUPS):
            bias_ref[h] = slopes[h] * rel.astype(F32)

    q = q_ref[0]
    lane = _lane_iota(q.shape)
    for c in range(2):
        for h in range(N_GROUPS):
            lo = HEAD_DIM * h + DIFF_SUB * c
            qm_ref[c * N_GROUPS + h] = jnp.where((lane >= lo) & (lane < lo + DIFF_SUB), q, jnp.zeros_like(q))

    def block(kb, first):
        start = pl.multiple_of(kb * tb, tb)
        kblk = k_ref[0, pl.ds(start, tb), :]
        off = ((kb - qi) * tb).astype(F32)
        for c in range(2):
            for h in range(N_GROUPS):
                ch = c * N_GROUPS + h
                s = _dot_nt(kblk, qm_ref[ch]) + bias_ref[h]
                shift = slopes[h] * off
                vh = v_ref[0, HEAD_DIM * h:HEAD_DIM * (h + 1), pl.ds(start, tb)]
                if first:
                    s = jnp.where(rel <= 0, s, NEG_BIG)
                    m_new = jnp.max(s, axis=0, keepdims=True) + shift
                    p = jnp.exp2(s - (m_new - shift))
                    l_ref[ch:ch + 1] = jnp.sum(p, axis=0, keepdims=True)
                    acc_ref[ch] = _dot(vh, p.astype(BF16))
                else:
                    m_old = m_ref[ch:ch + 1]
                    m_new = jnp.maximum(m_old, jnp.max(s, axis=0, keepdims=True) + shift)
                    alpha = jnp.exp2(m_old - m_new)
                    p = jnp.exp2(s - (m_new - shift))
                    l_ref[ch:ch + 1] = alpha * l_ref[ch:ch + 1] + jnp.sum(p, axis=0, keepdims=True)
                    acc_ref[ch] = alpha * acc_ref[ch] + _dot(vh, p.astype(BF16))
                m_ref[ch:ch + 1] = m_new

    block(qi, True)

    def body(i, _):
        block(qi - 1 - i, False)
        return 0

    lax.fori_loop(0, qi, body, 0)
    lam = _lam_value(lp_ref[...], lam_init)
    ys = []
    for h in range(N_GROUPS):
        y0 = acc_ref[h] / l_ref[h:h + 1]
        y1 = acc_ref[N_GROUPS + h] / l_ref[N_GROUPS + h:N_GROUPS + h + 1]
        ys.append(y0 - lam * y1)
    yt = _group_rms_t(jnp.concatenate(ys, axis=0), og_ref[...], bd64_ref[...], HEAD_DIM) * (1.0 - lam_init)
    o_ref[0] = _dot_nt(eye_ref[...], yt.astype(BF16)).astype(BF16)


def _diff_attention(q, k, v, lp, og_col, bd64, lam_init, tb):
    n, t, w = q.shape
    maps = 2 * N_GROUPS
    eye = jnp.asarray(np.eye(tb), dtype=BF16)
    return pl.pallas_call(
        functools.partial(_diff_kernel, tb=tb, lam_init=lam_init),
        grid=(n, t // tb),
        in_specs=[pl.BlockSpec((1, tb, w), lambda i, j: (i, j, 0)),
                  pl.BlockSpec((1, t, w), lambda i, j: (i, 0, 0)),
                  pl.BlockSpec((1, w, t), lambda i, j: (i, 0, 0)),
                  _full_spec(lp.shape), _full_spec(og_col.shape), _full_spec(bd64.shape), _full_spec(eye.shape)],
        out_specs=pl.BlockSpec((1, tb, w), lambda i, j: (i, j, 0)),
        out_shape=jax.ShapeDtypeStruct((n, t, w), BF16),
        scratch_shapes=[pltpu.VMEM((maps, tb, w), BF16), pltpu.VMEM((N_GROUPS, tb, tb), F32),
                        pltpu.VMEM((maps, tb), F32), pltpu.VMEM((maps, tb), F32),
                        pltpu.VMEM((maps, HEAD_DIM, tb), F32)],
        compiler_params=_params("arbitrary", "arbitrary"),
        name="diff_attn",
    )(q, k, v, lp, og_col, bd64, eye)


def _merge_kernel(x_ref, yabm_ref, yc_ref, yd_ref, g1_ref, wg_ref, bg_ref, wb_ref, wo_ref, o_ref):
    x = x_ref[0]
    hb = _rms(x, g1_ref[...]).astype(BF16)
    ys = [yabm_ref[0, :, 0:BR_WIDTH], yabm_ref[0, :, BR_WIDTH:2 * BR_WIDTH], yc_ref[0], yd_ref[0],
          yabm_ref[0, :, 2 * BR_WIDTH:]]
    acc = None
    for b in range(5):
        gate = jax.nn.sigmoid(_dot(hb, wg_ref[:, b * D_MODEL:(b + 1) * D_MODEL]) + bg_ref[:, b * D_MODEL:(b + 1) * D_MODEL])
        term = gate * _dot(ys[b], wb_ref[b])
        acc = term if acc is None else acc + term
    o_ref[0] = x + _dot(acc.astype(BF16), wo_ref[...])


def _merge(x, yabm, yc, yd, p, tm):
    n, t, d = x.shape
    consts = [p["norm1_g"], p["w_gate"], p["b_gate"], p["w_branch"], p["w_o"]]
    seq = lambda w: pl.BlockSpec((1, tm, w), lambda i, j: (i, j, 0))
    return pl.pallas_call(
        _merge_kernel,
        grid=(n, t // tm),
        in_specs=[seq(d), seq(3 * BR_WIDTH), seq(BR_WIDTH), seq(BR_WIDTH)] + [_full_spec(c.shape) for c in consts],
        out_specs=seq(d),
        out_shape=jax.ShapeDtypeStruct((n, t, d), F32),
        compiler_params=_params("arbitrary", "arbitrary"),
        name="merge",
    )(x, yabm, yc, yd, *consts)


FFN_COLS = 1408


def _ffn_kernel(x_ref, g2_ref, wup_ref, cw_ref, cb_ref, wdn_ref, o_ref, st_ref, carry_ref, *, tm):
    t = pl.program_id(1)
    x = x_ref[0]
    hb = _rms(x, g2_ref[...]).astype(BF16)

    @pl.when(t == 0)
    def _():
        carry_ref[...] = jnp.zeros_like(carry_ref)

    row = _row_iota((tm, FFN_COLS))

    def conv(lo):
        u = _dot(hb, wup_ref[:, lo:lo + FFN_COLS])
        prev = carry_ref[:, lo:lo + FFN_COLS]
        p1, p2 = prev[7:8], prev[6:7]
        s1 = jnp.where(row == 0, p1, pltpu.roll(u, 1, axis=0))
        s2 = jnp.where(row == 0, p2, jnp.where(row == 1, p1, pltpu.roll(u, 2, axis=0)))
        carry_ref[:, lo:lo + FFN_COLS] = u[tm - 8:tm]
        st_ref[0, :, lo:lo + FFN_COLS] = u[tm - 2:tm]
        return (cw_ref[0:1, lo:lo + FFN_COLS] * s2 + cw_ref[1:2, lo:lo + FFN_COLS] * s1
                + cw_ref[2:3, lo:lo + FFN_COLS] * u + cb_ref[:, lo:lo + FFN_COLS])

    acc = x
    for j in range(D_FF // FFN_COLS):
        gate = conv(j * FFN_COLS)
        val = conv(D_FF + j * FFN_COLS)
        act = (gate * jax.nn.sigmoid(gate) * val).astype(BF16)
        acc = acc + _dot(act, wdn_ref[j * FFN_COLS:(j + 1) * FFN_COLS, :])
    o_ref[0] = acc


def _ffn(x, p, tm):
    n, t, d = x.shape
    consts = [p["norm2_g"], p["w_up"], p["conv_ffn_w"], p["conv_ffn_b"], p["w_down"]]
    seq = pl.BlockSpec((1, tm, d), lambda i, j: (i, j, 0))
    return pl.pallas_call(
        functools.partial(_ffn_kernel, tm=tm),
        grid=(n, t // tm),
        in_specs=[seq] + [_full_spec(c.shape) for c in consts],
        out_specs=[seq, pl.BlockSpec((1, 2, 2 * D_FF), lambda i, j: (i, 0, 0))],
        out_shape=[jax.ShapeDtypeStruct((n, t, d), F32), jax.ShapeDtypeStruct((n, 2, 2 * D_FF), F32)],
        scratch_shapes=[pltpu.VMEM((8, 2 * D_FF), F32)],
        compiler_params=_params("arbitrary", "arbitrary"),
        name="conv_ffn",
    )(x, *consts)


def _ffn_step_kernel(x_ref, p0_ref, p1_ref, g2_ref, wup_ref, cw_ref, cb_ref, wdn_ref, o_ref, u_ref):
    x = x_ref[...]
    hb = _rms(x, g2_ref[...]).astype(BF16)

    def conv(lo):
        u = _dot(hb, wup_ref[:, lo:lo + FFN_COLS])
        u_ref[:, lo:lo + FFN_COLS] = u
        return (cw_ref[0:1, lo:lo + FFN_COLS] * p0_ref[:, lo:lo + FFN_COLS]
                + cw_ref[1:2, lo:lo + FFN_COLS] * p1_ref[:, lo:lo + FFN_COLS]
                + cw_ref[2:3, lo:lo + FFN_COLS] * u + cb_ref[:, lo:lo + FFN_COLS])

    acc = x
    for j in range(D_FF // FFN_COLS):
        gate = conv(j * FFN_COLS)
        val = conv(D_FF + j * FFN_COLS)
        act = (gate * jax.nn.sigmoid(gate) * val).astype(BF16)
        acc = acc + _dot(act, wdn_ref[j * FFN_COLS:(j + 1) * FFN_COLS, :])
    o_ref[...] = acc


def _ffn_step(x, prev0, prev1, p):
    m, d = x.shape
    consts = [p["norm2_g"], p["w_up"], p["conv_ffn_w"], p["conv_ffn_b"], p["w_down"]]
    args = [x, prev0, prev1] + consts
    return pl.pallas_call(
        _ffn_step_kernel,
        grid=(1,),
        in_specs=[_full_spec(a.shape) for a in args],
        out_specs=[_full_spec((m, d)), _full_spec((m, 2 * D_FF))],
        out_shape=[jax.ShapeDtypeStruct((m, d), F32), jax.ShapeDtypeStruct((m, 2 * D_FF), F32)],
        compiler_params=_params("arbitrary"),
        name="conv_ffn_step",
    )(*args)


def _inproj_step_kernel(x_ref, c0_ref, c1_ref, g1_ref, win_ref, cw_ref, lng_ref, lnb_ref, ws_ref, gb_ref,
                        dqg_ref, dkg_ref, mqg_ref, bd32_ref, bd64_ref,
                        yab_ref, q_ref, ksb_ref, vsb_ref, kdf_ref, vdf_ref, vn_ref, ach_ref):
    hb = _rms(x_ref[...], g1_ref[...]).astype(BF16)

    def proj(lo, hi):
        return _dot(hb, win_ref[:, lo:hi])

    pa = proj(0, 3 * BR_WIDTH)
    ach = pa[:, BR_WIDTH:2 * BR_WIDTH] * pa[:, 2 * BR_WIDTH:]
    cw = cw_ref[...]
    yab_ref[:, 0:BR_WIDTH] = (pa[:, :BR_WIDTH] * (cw[0:1] * c0_ref[...] + cw[1:2] * c1_ref[...] + cw[2:3] * ach)).astype(BF16)
    ach_ref[...] = ach

    pg = proj(3 * BR_WIDTH, 5 * BR_WIDTH)
    vn = _layer_norm(pg[:, BR_WIDTH:], lng_ref[...], lnb_ref[...])
    vn_ref[...] = vn
    lane = _lane_iota((1, BR_WIDTH))
    w00 = jnp.zeros((1, BR_WIDTH), F32)
    for g in range(N_GROUPS):
        w00 = jnp.where((lane >= HEAD_DIM * g) & (lane < HEAD_DIM * (g + 1)), ws_ref[g, 0:1, 0:1], w00)
    yab_ref[:, BR_WIDTH:] = (pg[:, :BR_WIDTH] * (w00 * vn + gb_ref[0:1, :])).astype(BF16)

    ps = proj(5 * BR_WIDTH, 8 * BR_WIDTH)
    q_ref[:, 0:BR_WIDTH] = (ps[:, :BR_WIDTH] * (HEAD_DIM ** -0.5)).astype(BF16)
    ksb_ref[...] = ps[:, BR_WIDTH:2 * BR_WIDTH]
    vsb_ref[...] = ps[:, 2 * BR_WIDTH:]

    pd = proj(8 * BR_WIDTH, 11 * BR_WIDTH)
    bd32 = bd32_ref[...]
    qd = _group_rms(pd[:, :BR_WIDTH], dqg_ref[...], bd32, DIFF_SUB)
    q_ref[:, BR_WIDTH:2 * BR_WIDTH] = (qd * (DIFF_SUB ** -0.5 * LOG2E)).astype(BF16)
    kdf_ref[...] = _group_rms(pd[:, BR_WIDTH:2 * BR_WIDTH], dkg_ref[...], bd32, DIFF_SUB)
    vdf_ref[...] = pd[:, 2 * BR_WIDTH:]

    qm = _group_rms(proj(11 * BR_WIDTH, 12 * BR_WIDTH), mqg_ref[...], bd64_ref[...], HEAD_DIM)
    q_ref[:, 2 * BR_WIDTH:] = (qm * (HEAD_DIM ** -0.5 * LOG2E)).astype(BF16)


def _inproj_step(x, c0, c1, p):
    m, d = x.shape
    consts = [p["norm1_g"], p["w_in"], p["conv_a_w"], p["gmlp_ln_g"], p["gmlp_ln_b"], p["gmlp_ws"],
              p["gmlp_b_full"], p["diff_qn_g"], p["diff_kn_g"], p["mem_qn_g"], p["bd32"], p["bd64"]]
    args = [x, c0, c1] + consts
    fl = jax.ShapeDtypeStruct((m, BR_WIDTH), F32)
    out_shape = [jax.ShapeDtypeStruct((m, 2 * BR_WIDTH), BF16), jax.ShapeDtypeStruct((m, 3 * BR_WIDTH), BF16)] + [fl] * 6
    return pl.pallas_call(
        _inproj_step_kernel,
        grid=(1,),
        in_specs=[_full_spec(a.shape) for a in args],
        out_specs=[_full_spec(s.shape) for s in out_shape],
        out_shape=out_shape,
        compiler_params=_params("arbitrary"),
        name="in_proj_step",
    )(*args)


def _decode_kernel(pt_ref, q_ref, kn_ref, vn_ref, mk_ref, mv_ref, lp_ref, og_ref, bd64_ref, ll_ref, ones_ref, *rest,
                   n_pages, page, lam_init):
    del pt_ref
    pages = rest[:4 * n_pages]
    o_ref = rest[4 * n_pages]
    ksb, vsb, kdf, vdf = (pages[i * n_pages:(i + 1) * n_pages] for i in range(4))
    rows = 2 * N_GROUPS
    n_sub = HEAD_DIM // DIFF_SUB
    ones = ones_ref[...]

    def q_column(col):
        qrow = jnp.broadcast_to(q_ref[0, :, col * BR_WIDTH:(col + 1) * BR_WIDTH].astype(F32), (BR_WIDTH, BR_WIDTH))
        diag = jnp.where(_row_iota(qrow.shape) == _lane_iota(qrow.shape), qrow, 0.0).astype(BF16)
        return _dot(diag, ones)

    def group_scores(kt, qcol):
        return jnp.sum((kt * qcol).reshape(rows, DIFF_SUB, kt.shape[1]), axis=1)

    def pair_sum(z):
        r = z.shape[0]
        even = (_row_iota(z.shape) & 1) == 0
        return z + jnp.where(even, pltpu.roll(z, r - 1, axis=0), pltpu.roll(z, 1, axis=0))

    def head_rows(w):
        return jnp.concatenate([jnp.broadcast_to(w[n_sub * h:n_sub * h + 1], (HEAD_DIM, w.shape[1]))
                                for h in range(N_GROUPS)], axis=0)

    def reduce_positions(acc):
        hi, lo = _split_bf16(acc)
        ones_row = jnp.ones((rows, acc.shape[1]), BF16)
        return _dot_nt(ones_row, hi) + _dot_nt(ones_row, lo)

    qc = q_column(0)
    z = jnp.concatenate([pair_sum(group_scores(ksb[pg][0, 0], qc)) for pg in range(n_pages)], axis=0)
    lb, l1m = _log_sigmoid_pair(z)
    hi, lo = _split_bf16(l1m)
    hl = jnp.concatenate([hi, lo], axis=1)
    within = _dot(hl, ll_ref[...])
    total = _dot(hl, ones)
    carry = jnp.zeros((rows, page), F32)
    carries = [None] * n_pages
    for pg in reversed(range(n_pages)):
        carries[pg] = carry
        carry = carry + total[rows * pg:rows * (pg + 1)]
    a = jnp.exp(lb + within + jnp.concatenate(carries, axis=0))
    acc = jnp.zeros((BR_WIDTH, page), F32)
    for pg in range(n_pages):
        acc = acc + head_rows(a[rows * pg:rows * (pg + 1)]) * vsb[pg][0, 0]
    y_c = reduce_positions(acc)[0:1]

    qc = q_column(1)
    s = jnp.concatenate([group_scores(kdf[pg][0, 0], qc) for pg in range(n_pages)], axis=0)
    slopes = _alibi_slopes_log2()
    all_rows = _row_iota((rows * n_pages, page))
    head = _div_pow2(_rem_pow2(all_rows, rows), n_sub)
    sl = jnp.full(all_rows.shape, slopes[0], F32)
    for h in range(1, N_GROUPS):
        sl = jnp.where(head == h, slopes[h], sl)
    past = n_pages * page
    k_pos = _div_pow2(all_rows, rows) * page + _lane_iota(all_rows.shape)
    s = s + sl * (k_pos - past).astype(F32)
    row8 = _row_iota((rows, BR_WIDTH))
    group_lanes = _div_pow2(_lane_iota((rows, BR_WIDTH)), DIFF_SUB) == row8
    q8 = jnp.broadcast_to(q_ref[0, :, BR_WIDTH:2 * BR_WIDTH].astype(F32), (rows, BR_WIDTH))
    kn8 = jnp.broadcast_to(kn_ref[0], (rows, BR_WIDTH))
    s_new = jnp.sum(jnp.where(group_lanes, q8 * kn8, 0.0), axis=1, keepdims=True)
    m8 = s[0:rows]
    for pg in range(1, n_pages):
        m8 = jnp.maximum(m8, s[rows * pg:rows * (pg + 1)])
    m = jnp.maximum(jnp.max(m8, axis=1, keepdims=True), s_new)
    p_new = jnp.exp2(s_new - m)
    m_all = jnp.concatenate([jnp.broadcast_to(m, (rows, page))] * n_pages, axis=0)
    p = jnp.exp2(s - m_all)
    l8 = p[0:rows]
    for pg in range(1, n_pages):
        l8 = l8 + p[rows * pg:rows * (pg + 1)]
    l = jnp.sum(l8, axis=1, keepdims=True) + p_new
    lam = _lam_value(lp_ref[...], lam_init)
    first_map = (_row_iota((rows, 1)) & 1) == 0
    coef = jnp.where(first_map, 1.0, -lam) / l
    w = pair_sum(p * jnp.concatenate([jnp.broadcast_to(coef, (rows, page))] * n_pages, axis=0))
    acc = jnp.zeros((BR_WIDTH, page), F32)
    for pg in range(n_pages):
        acc = acc + head_rows(w[rows * pg:rows * (pg + 1)]) * vdf[pg][0, 0]
    w_new = pair_sum(p_new * coef)
    head_lanes = (_div_pow2(_lane_iota((rows, BR_WIDTH)), HEAD_DIM) * n_sub) == row8
    vn8 = jnp.broadcast_to(vn_ref[0], (rows, BR_WIDTH))
    y_new = jnp.sum(jnp.where(head_lanes, w_new * vn8, 0.0), axis=0, keepdims=True)
    y_d = reduce_positions(acc)[0:1] + y_new
    y_d = _group_rms(jnp.broadcast_to(y_d, (rows, BR_WIDTH)), og_ref[...], bd64_ref[...], HEAD_DIM)[0:1]
    y_d = y_d * (1.0 - lam_init)

    qc = q_column(2)
    n_mem = mk_ref.shape[2]
    qc = jnp.concatenate([qc] * (n_mem // page), axis=1)
    s = pair_sum(group_scores(mk_ref[0], qc))
    p = jnp.exp2(s - jnp.max(s, axis=1, keepdims=True))
    p = p / jnp.sum(p, axis=1, keepdims=True)
    y_m = reduce_positions(head_rows(p) * mv_ref[0])[0:1]

    o_ref[0, :, 0:BR_WIDTH] = y_c.astype(BF16)
    o_ref[0, :, BR_WIDTH:2 * BR_WIDTH] = y_d.astype(BF16)
    o_ref[0, :, 2 * BR_WIDTH:] = y_m.astype(BF16)


def _decode_attention(layer, page_table, q, k_new, v_new, caches, mem_k, mem_v, p, lam_init):
    b = q.shape[0]
    n_pages = page_table.shape[1]
    page = caches[0].shape[3]
    ll = _suffix_ones(page)
    row_spec = lambda w: pl.BlockSpec((1, 1, w), lambda i, pt: (i, 0, 0))
    mem_spec = pl.BlockSpec((None, 1) + mem_k.shape[2:], lambda i, pt: (layer, i, 0, 0))
    const = lambda a: pl.BlockSpec(a.shape, lambda i, pt: (0,) * a.ndim)
    page_specs, page_args = [], []
    for cache in caches:
        for j in range(n_pages):
            page_specs.append(pl.BlockSpec((1, 1, BR_WIDTH, page), lambda i, pt, j=j: (layer, pt[i, j], 0, 0)))
            page_args.append(cache)
    consts = [p["diff_lambda"], p["diff_out_g"], p["bd64"], ll, jnp.ones((BR_WIDTH, page), BF16)]
    grid_spec = pltpu.PrefetchScalarGridSpec(
        num_scalar_prefetch=1,
        grid=(b,),
        in_specs=[row_spec(3 * BR_WIDTH), row_spec(BR_WIDTH), row_spec(BR_WIDTH), mem_spec, mem_spec]
        + [const(c) for c in consts] + page_specs,
        out_specs=row_spec(3 * BR_WIDTH),
    )
    return pl.pallas_call(
        functools.partial(_decode_kernel, n_pages=n_pages, page=page, lam_init=lam_init),
        grid_spec=grid_spec,
        out_shape=jax.ShapeDtypeStruct((b, 1, 3 * BR_WIDTH), BF16),
        compiler_params=_params("arbitrary"),
        name="decode_attn",
    )(page_table, q, k_new, v_new, mem_k, mem_v, *consts, *page_args)


PROMPT_TILE = 512
ATTN_TQ = 128
ATTN_TK = 256
DIFF_BLOCK = 256


def _layer_params(l, w):
    row = lambda a: a[l].reshape(1, -1).astype(F32)
    tile4 = lambda a: jnp.tile(a[l].astype(F32), N_GROUPS).reshape(1, -1)
    col4 = lambda a: jnp.tile(a[l].astype(F32), N_GROUPS).reshape(-1, 1)
    w_in = w["w_in"][l].astype(BF16)
    return dict(
        norm1_g=row(w["norm1_g"]), w_in=w_in, conv_a_w=w["conv_a_w"][l].astype(F32),
        w_sb_kv_t=w_in[:, 6 * BR_WIDTH:8 * BR_WIDTH].T, w_df_kv_t=w_in[:, 9 * BR_WIDTH:11 * BR_WIDTH].T,
        diff_kn_g_col=col4(w["diff_kn_g"]), mem_kn_g_col=col4(w["mem_kn_g"]), diff_out_g_col=col4(w["diff_out_g"]),
        w_mem_kv_t=w["w_mem_kv"][l].astype(BF16).T,
        gmlp_ln_g=row(w["gmlp_ln_g"]), gmlp_ln_b=row(w["gmlp_ln_b"]), gmlp_ws=w["gmlp_ws"][l].astype(F32),
        gmlp_b_full=jnp.repeat(w["gmlp_b"][l].astype(F32).T, HEAD_DIM, axis=1),
        diff_qn_g=tile4(w["diff_qn_g"]), diff_kn_g=tile4(w["diff_kn_g"]), diff_out_g=tile4(w["diff_out_g"]),
        diff_lambda=w["diff_lambda"][l].astype(F32),
        mem_norm_g=row(w["mem_norm_g"]), mem_qn_g=tile4(w["mem_qn_g"]),
        w_branch=w["w_branch"][l].astype(BF16), w_gate=w["w_gate"][l].astype(BF16), b_gate=row(w["b_gate"]),
        w_o=w["w_o"][l].astype(BF16), norm2_g=row(w["norm2_g"]), w_up=w["w_up"][l].astype(BF16),
        conv_ffn_w=w["conv_ffn_w"][l].astype(F32), conv_ffn_b=row(w["conv_ffn_b"]), w_down=w["w_down"][l].astype(BF16),
        bd32=_block_ones(DIFF_SUB), bd64=_block_ones(HEAD_DIM),
    )


def _prompt_layer(x, mem, p, lam_init):
    n, t, _ = x.shape
    tm = min(PROMPT_TILE, t)
    tq, tk = min(ATTN_TQ, t), min(ATTN_TK, t)
    mk, mv, mk16, mv16 = _mem_kv(mem, p["mem_norm_g"], p["w_mem_kv_t"], p["mem_kn_g_col"], p["bd64"])
    (yabm, qsb, ksb16, vsb16, qdf, kdf16, vdf16, ksb, vsb, kdf, vdf, ca) = _inproj(x, p, mk16, mv16, tm)
    yc = _sb_attention(qsb, ksb16, vsb16, tq, tk)
    yd = _diff_attention(qdf, kdf16, vdf16, p["diff_lambda"], p["diff_out_g_col"], p["bd64"], lam_init,
                         min(DIFF_BLOCK, t))
    x1 = _merge(x, yabm, yc, yd, p, tm)
    x2, cf = _ffn(x1, p, tm)
    return x2, ca, ksb, vsb, kdf, vdf, mk, mv, cf


def _sample_layer(l, x, conv_a, conv_ffn, page_table, caches, mem_k, mem_v, p, lam_init):
    b = x.shape[0]
    yab, q, ksb, vsb, kdf, vdf, vn, ach = _inproj_step(x, conv_a[:, 0], conv_a[:, 1], p)
    row3 = lambda a: a.reshape(b, 1, -1)
    ycdm = _decode_attention(l, page_table, row3(q), row3(kdf), row3(vdf), caches, mem_k, mem_v, p, lam_init)
    yabm = jnp.concatenate([yab, ycdm[:, 0, 2 * BR_WIDTH:]], axis=1)
    x1 = _merge(x[None], yabm[None], ycdm[None, :, 0, 0:BR_WIDTH], ycdm[None, :, 0, BR_WIDTH:2 * BR_WIDTH], p, b)[0]
    x2, u = _ffn_step(x1, conv_ffn[:, 0], conv_ffn[:, 1], p)
    ca_new = jnp.stack([conv_a[:, 1], ach], axis=1)
    cf_new = jnp.stack([conv_ffn[:, 1], u], axis=1)
    return x2, ca_new, vn, ksb, vsb, kdf, vdf, cf_new


def kernel(x_prompt, x_sample, state_conv_a, cache_k_sb, cache_v_sb, cache_k_diff, cache_v_diff, cache_mem_k,
           cache_mem_v, state_conv_ffn, page_table, mem_prompt, norm1_g, w_in, conv_a_w, gmlp_ln_g, gmlp_ln_b,
           gmlp_ws, gmlp_b, diff_qn_g, diff_kn_g, diff_lambda, diff_out_g, mem_norm_g, w_mem_kv, mem_qn_g,
           mem_kn_g, w_branch, w_gate, b_gate, w_o, norm2_g, w_up, conv_ffn_w, conv_ffn_b, w_down):
    w = dict(norm1_g=norm1_g, w_in=w_in, conv_a_w=conv_a_w, gmlp_ln_g=gmlp_ln_g, gmlp_ln_b=gmlp_ln_b,
             gmlp_ws=gmlp_ws, gmlp_b=gmlp_b, diff_qn_g=diff_qn_g, diff_kn_g=diff_kn_g, diff_lambda=diff_lambda,
             diff_out_g=diff_out_g, mem_norm_g=mem_norm_g, w_mem_kv=w_mem_kv, mem_qn_g=mem_qn_g,
             mem_kn_g=mem_kn_g, w_branch=w_branch, w_gate=w_gate, b_gate=b_gate, w_o=w_o, norm2_g=norm2_g,
             w_up=w_up, conv_ffn_w=conv_ffn_w, conv_ffn_b=conv_ffn_b, w_down=w_down)
    depth = w_in.shape[0]
    n_p, t_p, _ = x_prompt.shape
    n_s = x_sample.shape[0]
    flat = lambda c: jnp.transpose(c, (0, 1, 3, 4, 2)).reshape(c.shape[:2] + (BR_WIDTH, c.shape[2]))
    caches = [flat(cache_k_sb), flat(cache_v_sb), flat(cache_k_diff), flat(cache_v_diff)]
    mem_k, mem_v = flat(cache_mem_k), flat(cache_mem_v)
    xp, xs = x_prompt, x_sample[:, 0]
    outs_p, outs_s = [], []
    for l in range(depth):
        p = _layer_params(l, w)
        lam_init = 0.8 - 0.6 * math.exp(-0.3 * l)
        xp, *rest_p = _prompt_layer(xp, mem_prompt, p, lam_init)
        outs_p.append(rest_p)
        xs, *rest_s = _sample_layer(l, xs, state_conv_a[l], state_conv_ffn[l], page_table, caches, mem_k, mem_v,
                                    p, lam_init)
        outs_s.append(rest_s)
    heads = lambda a: a.reshape(a.shape[:-1] + (N_GROUPS, HEAD_DIM))
    stack_p = lambda i: jnp.stack([o[i] for o in outs_p], axis=0)
    stack_s = lambda i: jnp.stack([o[i] for o in outs_s], axis=0)
    step = lambda a: a.reshape(depth, n_s, 1, -1)

    def heads_t(a):
        d0, n, _, t = a.shape
        return jnp.transpose(a.reshape(d0, n, N_GROUPS, HEAD_DIM, t), (0, 1, 4, 2, 3))

    return (xp, xs[:, None, :],
            stack_p(0), stack_s(0), step(stack_s(1)),
            heads_t(stack_p(1)), heads_t(stack_p(2)), heads(step(stack_s(2))), heads(step(stack_s(3))),
            heads_t(stack_p(3)), heads_t(stack_p(4)), heads(step(stack_s(4))), heads(step(stack_s(5))),
            heads_t(stack_p(5)), heads_t(stack_p(6)),
            stack_p(7), stack_s(6))
```

```python
import functools
import math

import numpy as np
import jax
import jax.numpy as jnp
from jax import lax
from jax.experimental import pallas as pl
from jax.experimental.pallas import tpu as pltpu

D_MODEL = 1024
HEAD_DIM = 64
BR_WIDTH = 256
N_GROUPS = 4
DIFF_SUB = 32
CHUNK = 128
D_FF = 2816
EPS = 1e-6
NEG_BIG = -1e30
SB_UNDERFLOW = -104.0
LOG2E = 1.4426950408889634
VMEM_LIMIT_BYTES = 56 * 1024 * 1024

F32 = jnp.float32
BF16 = jnp.bfloat16


def _dot(a, b):
    return jnp.dot(a, b, preferred_element_type=F32)


def _dot_nt(a, b):
    return lax.dot_general(a, b, (((1,), (1,)), ((), ())), preferred_element_type=F32)


def _split_bf16(x):
    hi = x.astype(BF16)
    lo = (x - hi.astype(F32)).astype(BF16)
    return hi, lo


def _dot_split(x, w):
    hi, lo = _split_bf16(x)
    return _dot(hi, w) + _dot(lo, w)


def _rms(x, g):
    ms = jnp.mean(x * x, axis=-1, keepdims=True)
    return x * lax.rsqrt(ms + EPS) * g


def _group_rms(z, g, ones_bd, group):
    ms = _dot_split(z * z, ones_bd) * (1.0 / group)
    return z * lax.rsqrt(ms + EPS) * g


def _lane_iota(shape):
    return lax.broadcasted_iota(jnp.int32, shape, len(shape) - 1)


def _row_iota(shape):
    return lax.broadcasted_iota(jnp.int32, shape, len(shape) - 2)


def _div_pow2(x, d):
    assert d & (d - 1) == 0
    return lax.shift_right_logical(x, d.bit_length() - 1)


def _rem_pow2(x, d):
    assert d & (d - 1) == 0
    return x & (d - 1)


def _head_select(parts):
    lane = _lane_iota(parts[0].shape)
    out = parts[0]
    for h in range(1, N_GROUPS):
        out = jnp.where(lane >= HEAD_DIM * h, parts[h], out)
    return out


def _stack_heads(q, n_sub):
    lane = _lane_iota(q.shape)
    width = HEAD_DIM // n_sub
    zero = jnp.zeros_like(q)
    parts = []
    for c in range(n_sub):
        for h in range(N_GROUPS):
            lo = HEAD_DIM * h + width * c
            parts.append(jnp.where((lane >= lo) & (lane < lo + width), q, zero))
    return jnp.concatenate(parts, axis=0)


def _log_sigmoid_pair(z):
    lb = jnp.minimum(z, 0.0) - jnp.log(1.0 + jnp.exp(-jnp.abs(z)))
    return lb, lb - z


def _lam_value(lp, lam_init):
    a = jnp.sum(lp[0:1] * lp[1:2], axis=1, keepdims=True)
    b = jnp.sum(lp[2:3] * lp[3:4], axis=1, keepdims=True)
    return jnp.exp(a) - jnp.exp(b) + lam_init


def _block_ones(group):
    i = np.arange(BR_WIDTH)
    return jnp.asarray((i[:, None] // group) == (i[None, :] // group), dtype=BF16)


def _suffix_ones(n):
    i = np.arange(n)
    l = (i[:, None] > i[None, :])
    return jnp.asarray(np.concatenate([l, l], axis=0), dtype=BF16)


def _alibi_slopes_log2():
    return [LOG2E * 2.0 ** (-8.0 * (h + 1) / N_GROUPS) for h in range(N_GROUPS)]


def _full_spec(shape):
    nd = len(shape)
    return pl.BlockSpec(shape, lambda *_: (0,) * nd, pipeline_mode=pl.Buffered(1))


def _params(*sem):
    return pltpu.CompilerParams(dimension_semantics=sem, vmem_limit_bytes=VMEM_LIMIT_BYTES)


def _group_rms_t(zt, g_col, ones_bd, group):
    hi, lo = _split_bf16(zt * zt)
    ms = (_dot(ones_bd, hi) + _dot(ones_bd, lo)) * (1.0 / group)
    return zt * lax.rsqrt(ms + EPS) * g_col


def _memkv_kernel(mem_ref, g_ref, wt_ref, kg_ref, bd64_ref, k_ref, v_ref, k16_ref, v16_ref):
    hb = _rms(mem_ref[0], g_ref[...]).astype(BF16)
    kvt = _dot_nt(wt_ref[...], hb)
    k = _group_rms_t(kvt[:BR_WIDTH], kg_ref[...], bd64_ref[...], HEAD_DIM)
    v = kvt[BR_WIDTH:]
    k_ref[0] = k
    v_ref[0] = v
    k16_ref[0] = k.astype(BF16)
    v16_ref[0] = v.astype(BF16)


def _mem_kv(mem, g, w, kg, bd64):
    n, m, d = mem.shape
    blk = pl.BlockSpec((1, BR_WIDTH, m), lambda i: (i, 0, 0))
    return pl.pallas_call(
        _memkv_kernel,
        grid=(n,),
        in_specs=[pl.BlockSpec((1, m, d), lambda i: (i, 0, 0)), _full_spec(g.shape), _full_spec(w.shape),
                  _full_spec(kg.shape), _full_spec(bd64.shape)],
        out_specs=[blk, blk, blk, blk],
        out_shape=[jax.ShapeDtypeStruct((n, BR_WIDTH, m), F32)] * 2
        + [jax.ShapeDtypeStruct((n, BR_WIDTH, m), BF16)] * 2,
        compiler_params=_params("arbitrary"),
        name="mem_kv",
    )(mem, g, w, kg, bd64)


def _layer_norm(x, g, b):
    mu = jnp.mean(x, axis=-1, keepdims=True)
    xc = x - mu
    return xc * lax.rsqrt(jnp.mean(xc * xc, axis=-1, keepdims=True) + EPS) * g + b


def _mem_attention(qm, mkt, mvt):
    r = qm.shape[0]
    s = _dot(_stack_heads(qm, 1), mkt)
    p = jnp.exp2(s - jnp.max(s, axis=-1, keepdims=True))
    l = jnp.sum(p, axis=-1, keepdims=True)
    o = _dot_nt(p.astype(BF16), mvt) / l
    return _head_select([o[h * r:(h + 1) * r] for h in range(N_GROUPS)])


def _inproj_kernel(x_ref, g1_ref, win_ref, wsbt_ref, wdft_ref, cw_ref, lng_ref, lnb_ref, ws_ref, gb_ref, dqg_ref,
                   dkg_ref, dkgr_ref, mqg_ref, bd32_ref, bd64_ref, mk_ref, mv_ref,
                   yabm_ref, qsb_ref, ksb16_ref, vsb16_ref, qdf_ref, kdf16_ref, vdf16_ref,
                   ksb_ref, vsb_ref, kdf_ref, vdf_ref, ca_ref, carry_ref, *, tm):
    t = pl.program_id(1)
    hb = _rms(x_ref[0], g1_ref[...]).astype(BF16)

    def proj(lo, hi):
        return _dot(hb, win_ref[:, lo:hi])

    pa = proj(0, 3 * BR_WIDTH)
    ach = pa[:, BR_WIDTH:2 * BR_WIDTH] * pa[:, 2 * BR_WIDTH:]

    @pl.when(t == 0)
    def _():
        carry_ref[...] = jnp.zeros_like(carry_ref)

    prev = carry_ref[...]
    p1, p2 = prev[7:8], prev[6:7]
    row = _row_iota(ach.shape)
    s1 = jnp.where(row == 0, p1, pltpu.roll(ach, 1, axis=0))
    s2 = jnp.where(row == 0, p2, jnp.where(row == 1, p1, pltpu.roll(ach, 2, axis=0)))
    cw = cw_ref[...]
    yabm_ref[0, :, 0:BR_WIDTH] = (pa[:, :BR_WIDTH] * (cw[0:1] * s2 + cw[1:2] * s1 + cw[2:3] * ach)).astype(BF16)
    carry_ref[...] = ach[tm - 8:tm]
    ca_ref[0] = ach[tm - 2:tm]

    pg = proj(3 * BR_WIDTH, 5 * BR_WIDTH)
    vn = _layer_norm(pg[:, BR_WIDTH:], lng_ref[...], lnb_ref[...])
    tril = _row_iota((CHUNK, CHUNK)) >= _lane_iota((CHUNK, CHUNK))
    wsm = [jnp.where(tril, ws_ref[g], 0.0).astype(BF16) for g in range(N_GROUPS)]
    gb = gb_ref[...]
    for c in range(tm // CHUNK):
        vc = vn[c * CHUNK:(c + 1) * CHUNK].astype(BF16)
        mixed = _head_select([_dot(wsm[g], vc) for g in range(N_GROUPS)])
        yabm_ref[0, c * CHUNK:(c + 1) * CHUNK, BR_WIDTH:2 * BR_WIDTH] = (
            pg[c * CHUNK:(c + 1) * CHUNK, :BR_WIDTH] * (mixed + gb)).astype(BF16)

    qsb_ref[0] = (proj(5 * BR_WIDTH, 6 * BR_WIDTH) * (HEAD_DIM ** -0.5)).astype(BF16)
    kvt = _dot_nt(wsbt_ref[...], hb)
    ksb_ref[0] = kvt[:BR_WIDTH]
    vsb_ref[0] = kvt[BR_WIDTH:]
    ksb16_ref[0] = kvt[:BR_WIDTH].astype(BF16)
    vsb16_ref[0] = kvt[BR_WIDTH:].astype(BF16)

    bd32 = bd32_ref[...]
    qd = _group_rms(proj(8 * BR_WIDTH, 9 * BR_WIDTH), dqg_ref[...], bd32, DIFF_SUB)
    qdf_ref[0] = (qd * (DIFF_SUB ** -0.5 * LOG2E)).astype(BF16)
    kvt = _dot_nt(wdft_ref[...], hb)
    kd = _group_rms_t(kvt[:BR_WIDTH], dkg_ref[...], bd32, DIFF_SUB)
    kdf_ref[0] = kd
    vdf_ref[0] = kvt[BR_WIDTH:]
    kdf16_ref[0] = _group_rms(proj(9 * BR_WIDTH, 10 * BR_WIDTH), dkgr_ref[...], bd32, DIFF_SUB).astype(BF16)
    vdf16_ref[0] = kvt[BR_WIDTH:].astype(BF16)

    qm = _group_rms(proj(11 * BR_WIDTH, 12 * BR_WIDTH), mqg_ref[...], bd64_ref[...], HEAD_DIM)
    qm = (qm * (HEAD_DIM ** -0.5 * LOG2E)).astype(BF16)
    yabm_ref[0, :, 2 * BR_WIDTH:] = _mem_attention(qm, mk_ref[0], mv_ref[0]).astype(BF16)


def _inproj(x, p, mk16, mv16, tm):
    n, t, d = x.shape
    consts = [p["norm1_g"], p["w_in"], p["w_sb_kv_t"], p["w_df_kv_t"], p["conv_a_w"], p["gmlp_ln_g"], p["gmlp_ln_b"],
              p["gmlp_ws"], p["gmlp_b_full"], p["diff_qn_g"], p["diff_kn_g_col"], p["diff_kn_g"], p["mem_qn_g"],
              p["bd32"], p["bd64"]]
    seq = lambda w: pl.BlockSpec((1, tm, w), lambda i, j: (i, j, 0))
    seq_t = pl.BlockSpec((1, BR_WIDTH, tm), lambda i, j: (i, 0, j))
    mem_spec = pl.BlockSpec((1,) + mk16.shape[1:], lambda i, j: (i, 0, 0))
    bf = lambda w: jax.ShapeDtypeStruct((n, t, w), BF16)
    bf_t = jax.ShapeDtypeStruct((n, BR_WIDTH, t), BF16)
    fl_t = jax.ShapeDtypeStruct((n, BR_WIDTH, t), F32)
    return pl.pallas_call(
        functools.partial(_inproj_kernel, tm=tm),
        grid=(n, t // tm),
        in_specs=[seq(d)] + [_full_spec(c.shape) for c in consts] + [mem_spec, mem_spec],
        out_specs=[seq(3 * BR_WIDTH), seq(BR_WIDTH), seq_t, seq_t, seq(BR_WIDTH), seq(BR_WIDTH)] + [seq_t] * 5
        + [pl.BlockSpec((1, 2, BR_WIDTH), lambda i, j: (i, 0, 0))],
        out_shape=[bf(3 * BR_WIDTH), bf(BR_WIDTH), bf_t, bf_t, bf(BR_WIDTH), bf(BR_WIDTH), bf_t] + [fl_t] * 4
        + [jax.ShapeDtypeStruct((n, 2, BR_WIDTH), F32)],
        scratch_shapes=[pltpu.VMEM((8, BR_WIDTH), F32)],
        compiler_params=_params("arbitrary", "arbitrary"),
        name="in_proj",
    )(x, *consts, mk16, mv16)


def _sb_kernel(q_ref, k_ref, v_ref, ll_ref, o_ref, acc_ref, *, tq, tk):
    qi = pl.program_id(1)
    qs = _stack_heads(q_ref[0], 1)
    rows = N_GROUPS * tq
    kb_diag = (qi * tq + tq - 1) // tk
    ll = ll_ref[...]

    def block(kb, carry, mask):
        start = pl.multiple_of(kb * tk, tk)
        z = _dot(qs, k_ref[0, :, pl.ds(start, tk)])
        lb, l1m = _log_sigmoid_pair(z)
        if mask is not None:
            l1m = jnp.where(mask, l1m, 0.0)
        hi, lo = _split_bf16(l1m)
        between = _dot(jnp.concatenate([hi, lo], axis=1), ll)
        a = jnp.exp(lb + between + carry)
        if mask is not None:
            a = jnp.where(mask, a, 0.0)
        pv = _dot_nt(a.astype(BF16), v_ref[0, :, pl.ds(start, tk)])
        return pv, carry + jnp.sum(l1m, axis=1, keepdims=True)

    q_pos = qi * tq + _rem_pow2(_row_iota((rows, tk)), tq)
    k_pos = kb_diag * tk + _lane_iota((rows, tk))
    pv, carry = block(kb_diag, jnp.zeros((rows, 1), F32), k_pos < q_pos)
    acc_ref[...] = pv

    def cond(state):
        i, live, _ = state
        return (i < kb_diag) & (live > SB_UNDERFLOW)

    def body(state):
        i, _, carry = state
        pv, carry = block(kb_diag - 1 - i, carry, None)
        acc_ref[...] += pv
        return i + 1, jnp.max(carry), carry

    lax.while_loop(cond, body, (jnp.int32(0), jnp.max(carry), carry))
    acc = acc_ref[...]
    o_ref[0] = _head_select([acc[h * tq:(h + 1) * tq] for h in range(N_GROUPS)]).astype(BF16)


def _sb_attention(q, k, v, tq, tk):
    n, t, w = q.shape
    ll = _suffix_ones(tk)
    kv_spec = pl.BlockSpec((1, w, t), lambda i, j: (i, 0, 0))
    return pl.pallas_call(
        functools.partial(_sb_kernel, tq=tq, tk=tk),
        grid=(n, t // tq),
        in_specs=[pl.BlockSpec((1, tq, w), lambda i, j: (i, j, 0)), kv_spec, kv_spec, _full_spec(ll.shape)],
        out_specs=pl.BlockSpec((1, tq, w), lambda i, j: (i, j, 0)),
        out_shape=jax.ShapeDtypeStruct((n, t, w), BF16),
        scratch_shapes=[pltpu.VMEM((N_GROUPS * tq, w), F32)],
        compiler_params=_params("arbitrary", "arbitrary"),
        name="sb_attn",
    )(q, k, v, ll)


def _diff_kernel(q_ref, k_ref, v_ref, lp_ref, og_ref, bd64_ref, eye_ref, o_ref,
                 qm_ref, bias_ref, m_ref, l_ref, acc_ref, *, tb, lam_init):
    qi = pl.program_id(1)
    slopes = _alibi_slopes_log2()
    rel = _row_iota((tb, tb)) - _lane_iota((tb, tb))

    @pl.when(qi == 0)
    def _():
        for h in range(N_GROUPS):
            bias_ref[h] = slopes[h] * rel.astype(F32)

    q = q_ref[0]
    lane = _lane_iota(q.shape)
    for c in range(2):
        for h in range(N_GROUPS):
            lo = HEAD_DIM * h + DIFF_SUB * c
            ch = c * N_GROUPS + h
            qm_ref[ch * tb:(ch + 1) * tb] = jnp.where((lane >= lo) & (lane < lo + DIFF_SUB), q, jnp.zeros_like(q))

    def block(kb, first):
        start = pl.multiple_of(kb * tb, tb)
        kblk = k_ref[0, pl.ds(start, tb), :]
        off = ((kb - qi) * tb).astype(F32)
        s_all = _dot_nt(kblk, qm_ref[...])
        probs, alphas = [], []
        for c in range(2):
            for h in range(N_GROUPS):
                ch = c * N_GROUPS + h
                s = s_all[:, ch * tb:(ch + 1) * tb] + bias_ref[h]
                shift = slopes[h] * off
                if first:
                    s = jnp.where(rel <= 0, s, NEG_BIG)
                    m_new = jnp.max(s, axis=0, keepdims=True) + shift
                    p = jnp.exp2(s - (m_new - shift))
                    l_ref[ch:ch + 1] = jnp.sum(p, axis=0, keepdims=True)
                    alphas.append(None)
                else:
                    m_old = m_ref[ch:ch + 1]
                    m_new = jnp.maximum(m_old, jnp.max(s, axis=0, keepdims=True) + shift)
                    alpha = jnp.exp2(m_old - m_new)
                    p = jnp.exp2(s - (m_new - shift))
                    l_ref[ch:ch + 1] = alpha * l_ref[ch:ch + 1] + jnp.sum(p, axis=0, keepdims=True)
                    alphas.append(alpha)
                m_ref[ch:ch + 1] = m_new
                probs.append(p.astype(BF16))
        for h in range(N_GROUPS):
            vh = v_ref[0, HEAD_DIM * h:HEAD_DIM * (h + 1), pl.ds(start, tb)]
            pv = _dot(vh, jnp.concatenate([probs[h], probs[N_GROUPS + h]], axis=1))
            for c in range(2):
                ch = c * N_GROUPS + h
                new = pv[:, c * tb:(c + 1) * tb]
                acc_ref[ch] = new if first else alphas[ch] * acc_ref[ch] + new

    block(qi, True)

    def body(i, _):
        block(qi - 1 - i, False)
        return 0

    lax.fori_loop(0, qi, body, 0)
    lam = _lam_value(lp_ref[...], lam_init)
    ys = []
    for h in range(N_GROUPS):
        y0 = acc_ref[h] / l_ref[h:h + 1]
        y1 = acc_ref[N_GROUPS + h] / l_ref[N_GROUPS + h:N_GROUPS + h + 1]
        ys.append(y0 - lam * y1)
    yt = _group_rms_t(jnp.concatenate(ys, axis=0), og_ref[...], bd64_ref[...], HEAD_DIM) * (1.0 - lam_init)
    o_ref[0] = _dot_nt(eye_ref[...], yt.astype(BF16)).astype(BF16)


def _diff_attention(q, k, v, lp, og_col, bd64, lam_init, tb):
    n, t, w = q.shape
    maps = 2 * N_GROUPS
    eye = jnp.asarray(np.eye(tb), dtype=BF16)
    return pl.pallas_call(
        functools.partial(_diff_kernel, tb=tb, lam_init=lam_init),
        grid=(n, t // tb),
        in_specs=[pl.BlockSpec((1, tb, w), lambda i, j: (i, j, 0)),
                  pl.BlockSpec((1, t, w), lambda i, j: (i, 0, 0)),
                  pl.BlockSpec((1, w, t), lambda i, j: (i, 0, 0)),
                  _full_spec(lp.shape), _full_spec(og_col.shape), _full_spec(bd64.shape), _full_spec(eye.shape)],
        out_specs=pl.BlockSpec((1, tb, w), lambda i, j: (i, j, 0)),
        out_shape=jax.ShapeDtypeStruct((n, t, w), BF16),
        scratch_shapes=[pltpu.VMEM((maps * tb, w), BF16), pltpu.VMEM((N_GROUPS, tb, tb), F32),
                        pltpu.VMEM((maps, tb), F32), pltpu.VMEM((maps, tb), F32),
                        pltpu.VMEM((maps, HEAD_DIM, tb), F32)],
        compiler_params=_params("arbitrary", "arbitrary"),
        name="diff_attn",
    )(q, k, v, lp, og_col, bd64, eye)


def _merge_kernel(x_ref, yabm_ref, yc_ref, yd_ref, g1_ref, wg_ref, bg_ref, wb_ref, wo_ref, o_ref):
    x = x_ref[0]
    hb = _rms(x, g1_ref[...]).astype(BF16)
    ys = [yabm_ref[0, :, 0:BR_WIDTH], yabm_ref[0, :, BR_WIDTH:2 * BR_WIDTH], yc_ref[0], yd_ref[0],
          yabm_ref[0, :, 2 * BR_WIDTH:]]
    acc = None
    for b in range(5):
        gate = jax.nn.sigmoid(_dot(hb, wg_ref[:, b * D_MODEL:(b + 1) * D_MODEL]) + bg_ref[:, b * D_MODEL:(b + 1) * D_MODEL])
        term = gate * _dot(ys[b], wb_ref[b])
        acc = term if acc is None else acc + term
    o_ref[0] = x + _dot(acc.astype(BF16), wo_ref[...])


def _merge(x, yabm, yc, yd, p, tm):
    n, t, d = x.shape
    consts = [p["norm1_g"], p["w_gate"], p["b_gate"], p["w_branch"], p["w_o"]]
    seq = lambda w: pl.BlockSpec((1, tm, w), lambda i, j: (i, j, 0))
    return pl.pallas_call(
        _merge_kernel,
        grid=(n, t // tm),
        in_specs=[seq(d), seq(3 * BR_WIDTH), seq(BR_WIDTH), seq(BR_WIDTH)] + [_full_spec(c.shape) for c in consts],
        out_specs=seq(d),
        out_shape=jax.ShapeDtypeStruct((n, t, d), F32),
        compiler_params=_params("arbitrary", "arbitrary"),
        name="merge",
    )(x, yabm, yc, yd, *consts)


FFN_COLS = 1408


def _ffn_kernel(x_ref, g2_ref, wup_ref, cw_ref, cb_ref, wdn_ref, o_ref, st_ref, carry_ref, *, tm):
    t = pl.program_id(1)
    x = x_ref[0]
    hb = _rms(x, g2_ref[...]).astype(BF16)

    @pl.when(t == 0)
    def _():
        carry_ref[...] = jnp.zeros_like(carry_ref)

    row = _row_iota((tm, FFN_COLS))

    def conv(lo):
        u = _dot(hb, wup_ref[:, lo:lo + FFN_COLS])
        prev = carry_ref[:, lo:lo + FFN_COLS]
        p1, p2 = prev[7:8], prev[6:7]
        s1 = jnp.where(row == 0, p1, pltpu.roll(u, 1, axis=0))
        s2 = jnp.where(row == 0, p2, jnp.where(row == 1, p1, pltpu.roll(u, 2, axis=0)))
        carry_ref[:, lo:lo + FFN_COLS] = u[tm - 8:tm]
        st_ref[0, :, lo:lo + FFN_COLS] = u[tm - 2:tm]
        return (cw_ref[0:1, lo:lo + FFN_COLS] * s2 + cw_ref[1:2, lo:lo + FFN_COLS] * s1
                + cw_ref[2:3, lo:lo + FFN_COLS] * u + cb_ref[:, lo:lo + FFN_COLS])

    acc = x
    for j in range(D_FF // FFN_COLS):
        gate = conv(j * FFN_COLS)
        val = conv(D_FF + j * FFN_COLS)
        act = (gate * jax.nn.sigmoid(gate) * val).astype(BF16)
        acc = acc + _dot(act, wdn_ref[j * FFN_COLS:(j + 1) * FFN_COLS, :])
    o_ref[0] = acc


def _ffn(x, p, tm):
    n, t, d = x.shape
    consts = [p["norm2_g"], p["w_up"], p["conv_ffn_w"], p["conv_ffn_b"], p["w_down"]]
    seq = pl.BlockSpec((1, tm, d), lambda i, j: (i, j, 0))
    return pl.pallas_call(
        functools.partial(_ffn_kernel, tm=tm),
        grid=(n, t // tm),
        in_specs=[seq] + [_full_spec(c.shape) for c in consts],
        out_specs=[seq, pl.BlockSpec((1, 2, 2 * D_FF), lambda i, j: (i, 0, 0))],
        out_shape=[jax.ShapeDtypeStruct((n, t, d), F32), jax.ShapeDtypeStruct((n, 2, 2 * D_FF), F32)],
        scratch_shapes=[pltpu.VMEM((8, 2 * D_FF), F32)],
        compiler_params=_params("arbitrary", "arbitrary"),
        name="conv_ffn",
    )(x, *consts)


def _ffn_step_kernel(x_ref, p0_ref, p1_ref, g2_ref, wup_ref, cw_ref, cb_ref, wdn_ref, o_ref, u_ref):
    x = x_ref[...]
    hb = _rms(x, g2_ref[...]).astype(BF16)

    def conv(lo):
        u = _dot(hb, wup_ref[:, lo:lo + FFN_COLS])
        u_ref[:, lo:lo + FFN_COLS] = u
        return (cw_ref[0:1, lo:lo + FFN_COLS] * p0_ref[:, lo:lo + FFN_COLS]
                + cw_ref[1:2, lo:lo + FFN_COLS] * p1_ref[:, lo:lo + FFN_COLS]
                + cw_ref[2:3, lo:lo + FFN_COLS] * u + cb_ref[:, lo:lo + FFN_COLS])

    acc = x
    for j in range(D_FF // FFN_COLS):
        gate = conv(j * FFN_COLS)
        val = conv(D_FF + j * FFN_COLS)
        act = (gate * jax.nn.sigmoid(gate) * val).astype(BF16)
        acc = acc + _dot(act, wdn_ref[j * FFN_COLS:(j + 1) * FFN_COLS, :])
    o_ref[...] = acc


def _ffn_step(x, prev0, prev1, p):
    m, d = x.shape
    consts = [p["norm2_g"], p["w_up"], p["conv_ffn_w"], p["conv_ffn_b"], p["w_down"]]
    args = [x, prev0, prev1] + consts
    return pl.pallas_call(
        _ffn_step_kernel,
        grid=(1,),
        in_specs=[_full_spec(a.shape) for a in args],
        out_specs=[_full_spec((m, d)), _full_spec((m, 2 * D_FF))],
        out_shape=[jax.ShapeDtypeStruct((m, d), F32), jax.ShapeDtypeStruct((m, 2 * D_FF), F32)],
        compiler_params=_params("arbitrary"),
        name="conv_ffn_step",
    )(*args)


def _inproj_step_kernel(x_ref, c0_ref, c1_ref, g1_ref, win_ref, cw_ref, lng_ref, lnb_ref, ws_ref, gb_ref,
                        dqg_ref, dkg_ref, mqg_ref, bd32_ref, bd64_ref,
                        yab_ref, q_ref, ksb_ref, vsb_ref, kdf_ref, vdf_ref, vn_ref, ach_ref):
    hb = _rms(x_ref[...], g1_ref[...]).astype(BF16)

    def proj(lo, hi):
        return _dot(hb, win_ref[:, lo:hi])

    pa = proj(0, 3 * BR_WIDTH)
    ach = pa[:, BR_WIDTH:2 * BR_WIDTH] * pa[:, 2 * BR_WIDTH:]
    cw = cw_ref[...]
    yab_ref[:, 0:BR_WIDTH] = (pa[:, :BR_WIDTH] * (cw[0:1] * c0_ref[...] + cw[1:2] * c1_ref[...] + cw[2:3] * ach)).astype(BF16)
    ach_ref[...] = ach

    pg = proj(3 * BR_WIDTH, 5 * BR_WIDTH)
    vn = _layer_norm(pg[:, BR_WIDTH:], lng_ref[...], lnb_ref[...])
    vn_ref[...] = vn
    lane = _lane_iota((1, BR_WIDTH))
    w00 = jnp.zeros((1, BR_WIDTH), F32)
    for g in range(N_GROUPS):
        w00 = jnp.where((lane >= HEAD_DIM * g) & (lane < HEAD_DIM * (g + 1)), ws_ref[g, 0:1, 0:1], w00)
    yab_ref[:, BR_WIDTH:] = (pg[:, :BR_WIDTH] * (w00 * vn + gb_ref[0:1, :])).astype(BF16)

    ps = proj(5 * BR_WIDTH, 8 * BR_WIDTH)
    q_ref[:, 0:BR_WIDTH] = (ps[:, :BR_WIDTH] * (HEAD_DIM ** -0.5)).astype(BF16)
    ksb_ref[...] = ps[:, BR_WIDTH:2 * BR_WIDTH]
    vsb_ref[...] = ps[:, 2 * BR_WIDTH:]

    pd = proj(8 * BR_WIDTH, 11 * BR_WIDTH)
    bd32 = bd32_ref[...]
    qd = _group_rms(pd[:, :BR_WIDTH], dqg_ref[...], bd32, DIFF_SUB)
    q_ref[:, BR_WIDTH:2 * BR_WIDTH] = (qd * (DIFF_SUB ** -0.5 * LOG2E)).astype(BF16)
    kdf_ref[...] = _group_rms(pd[:, BR_WIDTH:2 * BR_WIDTH], dkg_ref[...], bd32, DIFF_SUB)
    vdf_ref[...] = pd[:, 2 * BR_WIDTH:]

    qm = _group_rms(proj(11 * BR_WIDTH, 12 * BR_WIDTH), mqg_ref[...], bd64_ref[...], HEAD_DIM)
    q_ref[:, 2 * BR_WIDTH:] = (qm * (HEAD_DIM ** -0.5 * LOG2E)).astype(BF16)


def _inproj_step(x, c0, c1, p):
    m, d = x.shape
    consts = [p["norm1_g"], p["w_in"], p["conv_a_w"], p["gmlp_ln_g"], p["gmlp_ln_b"], p["gmlp_ws"],
              p["gmlp_b_full"], p["diff_qn_g"], p["diff_kn_g"], p["mem_qn_g"], p["bd32"], p["bd64"]]
    args = [x, c0, c1] + consts
    fl = jax.ShapeDtypeStruct((m, BR_WIDTH), F32)
    out_shape = [jax.ShapeDtypeStruct((m, 2 * BR_WIDTH), BF16), jax.ShapeDtypeStruct((m, 3 * BR_WIDTH), BF16)] + [fl] * 6
    return pl.pallas_call(
        _inproj_step_kernel,
        grid=(1,),
        in_specs=[_full_spec(a.shape) for a in args],
        out_specs=[_full_spec(s.shape) for s in out_shape],
        out_shape=out_shape,
        compiler_params=_params("arbitrary"),
        name="in_proj_step",
    )(*args)


def _decode_kernel(pt_ref, q_ref, kn_ref, vn_ref, mk_ref, mv_ref, lp_ref, og_ref, bd64_ref, ll_ref, ones_ref, *rest,
                   n_pages, page, lam_init):
    del pt_ref
    pages = rest[:4 * n_pages]
    o_ref = rest[4 * n_pages]
    ksb, vsb, kdf, vdf = (pages[i * n_pages:(i + 1) * n_pages] for i in range(4))
    rows = 2 * N_GROUPS
    n_sub = HEAD_DIM // DIFF_SUB
    ones = ones_ref[...]

    def q_column(col):
        qrow = jnp.broadcast_to(q_ref[0, :, col * BR_WIDTH:(col + 1) * BR_WIDTH].astype(F32), (BR_WIDTH, BR_WIDTH))
        diag = jnp.where(_row_iota(qrow.shape) == _lane_iota(qrow.shape), qrow, 0.0).astype(BF16)
        return _dot(diag, ones)

    def group_scores(kt, qcol):
        return jnp.sum((kt * qcol).reshape(rows, DIFF_SUB, kt.shape[1]), axis=1)

    def pair_sum(z):
        r = z.shape[0]
        even = (_row_iota(z.shape) & 1) == 0
        return z + jnp.where(even, pltpu.roll(z, r - 1, axis=0), pltpu.roll(z, 1, axis=0))

    def head_rows(w):
        return jnp.concatenate([jnp.broadcast_to(w[n_sub * h:n_sub * h + 1], (HEAD_DIM, w.shape[1]))
                                for h in range(N_GROUPS)], axis=0)

    def reduce_positions(acc):
        hi, lo = _split_bf16(acc)
        ones_row = jnp.ones((rows, acc.shape[1]), BF16)
        return _dot_nt(ones_row, hi) + _dot_nt(ones_row, lo)

    qc = q_column(0)
    z = jnp.concatenate([pair_sum(group_scores(ksb[pg][0, 0], qc)) for pg in range(n_pages)], axis=0)
    lb, l1m = _log_sigmoid_pair(z)
    hi, lo = _split_bf16(l1m)
    hl = jnp.concatenate([hi, lo], axis=1)
    within = _dot(hl, ll_ref[...])
    total = _dot(hl, ones)
    carry = jnp.zeros((rows, page), F32)
    carries = [None] * n_pages
    for pg in reversed(range(n_pages)):
        carries[pg] = carry
        carry = carry + total[rows * pg:rows * (pg + 1)]
    a = jnp.exp(lb + within + jnp.concatenate(carries, axis=0))
    acc = jnp.zeros((BR_WIDTH, page), F32)
    for pg in range(n_pages):
        acc = acc + head_rows(a[rows * pg:rows * (pg + 1)]) * vsb[pg][0, 0]
    y_c = reduce_positions(acc)[0:1]

    qc = q_column(1)
    s = jnp.concatenate([group_scores(kdf[pg][0, 0], qc) for pg in range(n_pages)], axis=0)
    slopes = _alibi_slopes_log2()
    all_rows = _row_iota((rows * n_pages, page))
    head = _div_pow2(_rem_pow2(all_rows, rows), n_sub)
    sl = jnp.full(all_rows.shape, slopes[0], F32)
    for h in range(1, N_GROUPS):
        sl = jnp.where(head == h, slopes[h], sl)
    past = n_pages * page
    k_pos = _div_pow2(all_rows, rows) * page + _lane_iota(all_rows.shape)
    s = s + sl * (k_pos - past).astype(F32)
    row8 = _row_iota((rows, BR_WIDTH))
    group_lanes = _div_pow2(_lane_iota((rows, BR_WIDTH)), DIFF_SUB) == row8
    q8 = jnp.broadcast_to(q_ref[0, :, BR_WIDTH:2 * BR_WIDTH].astype(F32), (rows, BR_WIDTH))
    kn8 = jnp.broadcast_to(kn_ref[0], (rows, BR_WIDTH))
    s_new = jnp.sum(jnp.where(group_lanes, q8 * kn8, 0.0), axis=1, keepdims=True)
    m8 = s[0:rows]
    for pg in range(1, n_pages):
        m8 = jnp.maximum(m8, s[rows * pg:rows * (pg + 1)])
    m = jnp.maximum(jnp.max(m8, axis=1, keepdims=True), s_new)
    p_new = jnp.exp2(s_new - m)
    m_all = jnp.concatenate([jnp.broadcast_to(m, (rows, page))] * n_pages, axis=0)
    p = jnp.exp2(s - m_all)
    l8 = p[0:rows]
    for pg in range(1, n_pages):
        l8 = l8 + p[rows * pg:rows * (pg + 1)]
    l = jnp.sum(l8, axis=1, keepdims=True) + p_new
    lam = _lam_value(lp_ref[...], lam_init)
    first_map = (_row_iota((rows, 1)) & 1) == 0
    coef = jnp.where(first_map, 1.0, -lam) / l
    w = pair_sum(p * jnp.concatenate([jnp.broadcast_to(coef, (rows, page))] * n_pages, axis=0))
    acc = jnp.zeros((BR_WIDTH, page), F32)
    for pg in range(n_pages):
        acc = acc + head_rows(w[rows * pg:rows * (pg + 1)]) * vdf[pg][0, 0]
    w_new = pair_sum(p_new * coef)
    head_lanes = (_div_pow2(_lane_iota((rows, BR_WIDTH)), HEAD_DIM) * n_sub) == row8
    vn8 = jnp.broadcast_to(vn_ref[0], (rows, BR_WIDTH))
    y_new = jnp.sum(jnp.where(head_lanes, w_new * vn8, 0.0), axis=0, keepdims=True)
    y_d = reduce_positions(acc)[0:1] + y_new
    y_d = _group_rms(jnp.broadcast_to(y_d, (rows, BR_WIDTH)), og_ref[...], bd64_ref[...], HEAD_DIM)[0:1]
    y_d = y_d * (1.0 - lam_init)

    qc = q_column(2)
    n_mem = mk_ref.shape[2]
    qc = jnp.concatenate([qc] * (n_mem // page), axis=1)
    s = pair_sum(group_scores(mk_ref[0], qc))
    p = jnp.exp2(s - jnp.max(s, axis=1, keepdims=True))
    p = p / jnp.sum(p, axis=1, keepdims=True)
    y_m = reduce_positions(head_rows(p) * mv_ref[0])[0:1]

    o_ref[0, :, 0:BR_WIDTH] = y_c.astype(BF16)
    o_ref[0, :, BR_WIDTH:2 * BR_WIDTH] = y_d.astype(BF16)
    o_ref[0, :, 2 * BR_WIDTH:] = y_m.astype(BF16)


def _decode_attention(layer, page_table, q, k_new, v_new, caches, mem_k, mem_v, p, lam_init):
    b = q.shape[0]
    n_pages = page_table.shape[1]
    page = caches[0].shape[3]
    ll = _suffix_ones(page)
    row_spec = lambda w: pl.BlockSpec((1, 1, w), lambda i, pt: (i, 0, 0))
    mem_spec = pl.BlockSpec((None, 1) + mem_k.shape[2:], lambda i, pt: (layer, i, 0, 0))
    const = lambda a: pl.BlockSpec(a.shape, lambda i, pt: (0,) * a.ndim)
    page_specs, page_args = [], []
    for cache in caches:
        for j in range(n_pages):
            page_specs.append(pl.BlockSpec((1, 1, BR_WIDTH, page), lambda i, pt, j=j: (layer, pt[i, j], 0, 0)))
            page_args.append(cache)
    consts = [p["diff_lambda"], p["diff_out_g"], p["bd64"], ll, jnp.ones((BR_WIDTH, page), BF16)]
    grid_spec = pltpu.PrefetchScalarGridSpec(
        num_scalar_prefetch=1,
        grid=(b,),
        in_specs=[row_spec(3 * BR_WIDTH), row_spec(BR_WIDTH), row_spec(BR_WIDTH), mem_spec, mem_spec]
        + [const(c) for c in consts] + page_specs,
        out_specs=row_spec(3 * BR_WIDTH),
    )
    return pl.pallas_call(
        functools.partial(_decode_kernel, n_pages=n_pages, page=page, lam_init=lam_init),
        grid_spec=grid_spec,
        out_shape=jax.ShapeDtypeStruct((b, 1, 3 * BR_WIDTH), BF16),
        compiler_params=_params("arbitrary"),
        name="decode_attn",
    )(page_table, q, k_new, v_new, mem_k, mem_v, *consts, *page_args)


PROMPT_TILE = 512
ATTN_TQ = 128
ATTN_TK = 256
DIFF_BLOCK = 256


def _layer_params(l, w):
    row = lambda a: a[l].reshape(1, -1).astype(F32)
    tile4 = lambda a: jnp.tile(a[l].astype(F32), N_GROUPS).reshape(1, -1)
    col4 = lambda a: jnp.tile(a[l].astype(F32), N_GROUPS).reshape(-1, 1)
    w_in = w["w_in"][l].astype(BF16)
    return dict(
        norm1_g=row(w["norm1_g"]), w_in=w_in, conv_a_w=w["conv_a_w"][l].astype(F32),
        w_sb_kv_t=w_in[:, 6 * BR_WIDTH:8 * BR_WIDTH].T, w_df_kv_t=w_in[:, 9 * BR_WIDTH:11 * BR_WIDTH].T,
        diff_kn_g_col=col4(w["diff_kn_g"]), mem_kn_g_col=col4(w["mem_kn_g"]), diff_out_g_col=col4(w["diff_out_g"]),
        w_mem_kv_t=w["w_mem_kv"][l].astype(BF16).T,
        gmlp_ln_g=row(w["gmlp_ln_g"]), gmlp_ln_b=row(w["gmlp_ln_b"]), gmlp_ws=w["gmlp_ws"][l].astype(F32),
        gmlp_b_full=jnp.repeat(w["gmlp_b"][l].astype(F32).T, HEAD_DIM, axis=1),
        diff_qn_g=tile4(w["diff_qn_g"]), diff_kn_g=tile4(w["diff_kn_g"]), diff_out_g=tile4(w["diff_out_g"]),
        diff_lambda=w["diff_lambda"][l].astype(F32),
        mem_norm_g=row(w["mem_norm_g"]), mem_qn_g=tile4(w["mem_qn_g"]),
        w_branch=w["w_branch"][l].astype(BF16), w_gate=w["w_gate"][l].astype(BF16), b_gate=row(w["b_gate"]),
        w_o=w["w_o"][l].astype(BF16), norm2_g=row(w["norm2_g"]), w_up=w["w_up"][l].astype(BF16),
        conv_ffn_w=w["conv_ffn_w"][l].astype(F32), conv_ffn_b=row(w["conv_ffn_b"]), w_down=w["w_down"][l].astype(BF16),
        bd32=_block_ones(DIFF_SUB), bd64=_block_ones(HEAD_DIM),
    )


def _prompt_layer(x, mem, p, lam_init):
    n, t, _ = x.shape
    tm = min(PROMPT_TILE, t)
    tq, tk = min(ATTN_TQ, t), min(ATTN_TK, t)
    mk, mv, mk16, mv16 = _mem_kv(mem, p["mem_norm_g"], p["w_mem_kv_t"], p["mem_kn_g_col"], p["bd64"])
    (yabm, qsb, ksb16, vsb16, qdf, kdf16, vdf16, ksb, vsb, kdf, vdf, ca) = _inproj(x, p, mk16, mv16, tm)
    yc = _sb_attention(qsb, ksb16, vsb16, tq, tk)
    yd = _diff_attention(qdf, kdf16, vdf16, p["diff_lambda"], p["diff_out_g_col"], p["bd64"], lam_init,
                         min(DIFF_BLOCK, t))
    x1 = _merge(x, yabm, yc, yd, p, tm)
    x2, cf = _ffn(x1, p, tm)
    return x2, ca, ksb, vsb, kdf, vdf, mk, mv, cf


def _sample_layer(l, x, conv_a, conv_ffn, page_table, caches, mem_k, mem_v, p, lam_init):
    b = x.shape[0]
    yab, q, ksb, vsb, kdf, vdf, vn, ach = _inproj_step(x, conv_a[:, 0], conv_a[:, 1], p)
    row3 = lambda a: a.reshape(b, 1, -1)
    ycdm = _decode_attention(l, page_table, row3(q), row3(kdf), row3(vdf), caches, mem_k, mem_v, p, lam_init)
    yabm = jnp.concatenate([yab, ycdm[:, 0, 2 * BR_WIDTH:]], axis=1)
    x1 = _merge(x[None], yabm[None], ycdm[None, :, 0, 0:BR_WIDTH], ycdm[None, :, 0, BR_WIDTH:2 * BR_WIDTH], p, b)[0]
    x2, u = _ffn_step(x1, conv_ffn[:, 0], conv_ffn[:, 1], p)
    ca_new = jnp.stack([conv_a[:, 1], ach], axis=1)
    cf_new = jnp.stack([conv_ffn[:, 1], u], axis=1)
    return x2, ca_new, vn, ksb, vsb, kdf, vdf, cf_new


def kernel(x_prompt, x_sample, state_conv_a, cache_k_sb, cache_v_sb, cache_k_diff, cache_v_diff, cache_mem_k,
           cache_mem_v, state_conv_ffn, page_table, mem_prompt, norm1_g, w_in, conv_a_w, gmlp_ln_g, gmlp_ln_b,
           gmlp_ws, gmlp_b, diff_qn_g, diff_kn_g, diff_lambda, diff_out_g, mem_norm_g, w_mem_kv, mem_qn_g,
           mem_kn_g, w_branch, w_gate, b_gate, w_o, norm2_g, w_up, conv_ffn_w, conv_ffn_b, w_down):
    w = dict(norm1_g=norm1_g, w_in=w_in, conv_a_w=conv_a_w, gmlp_ln_g=gmlp_ln_g, gmlp_ln_b=gmlp_ln_b,
             gmlp_ws=gmlp_ws, gmlp_b=gmlp_b, diff_qn_g=diff_qn_g, diff_kn_g=diff_kn_g, diff_lambda=diff_lambda,
             diff_out_g=diff_out_g, mem_norm_g=mem_norm_g, w_mem_kv=w_mem_kv, mem_qn_g=mem_qn_g,
             mem_kn_g=mem_kn_g, w_branch=w_branch, w_gate=w_gate, b_gate=b_gate, w_o=w_o, norm2_g=norm2_g,
             w_up=w_up, conv_ffn_w=conv_ffn_w, conv_ffn_b=conv_ffn_b, w_down=w_down)
    depth = w_in.shape[0]
    n_p, t_p, _ = x_prompt.shape
    n_s = x_sample.shape[0]
    flat = lambda c: jnp.transpose(c, (0, 1, 3, 4, 2)).reshape(c.shape[:2] + (BR_WIDTH, c.shape[2]))
    caches = [flat(cache_k_sb), flat(cache_v_sb), flat(cache_k_diff), flat(cache_v_diff)]
    mem_k, mem_v = flat(cache_mem_k), flat(cache_mem_v)
    xp, xs = x_prompt, x_sample[:, 0]
    outs_p, outs_s = [], []
    for l in range(depth):
        p = _layer_params(l, w)
        lam_init = 0.8 - 0.6 * math.exp(-0.3 * l)
        xp, *rest_p = _prompt_layer(xp, mem_prompt, p, lam_init)
        outs_p.append(rest_p)
        xs, *rest_s = _sample_layer(l, xs, state_conv_a[l], state_conv_ffn[l], page_table, caches, mem_k, mem_v,
                                    p, lam_init)
        outs_s.append(rest_s)
    heads = lambda a: a.reshape(a.shape[:-1] + (N_GROUPS, HEAD_DIM))
    stack_p = lambda i: jnp.stack([o[i] for o in outs_p], axis=0)
    stack_s = lambda i: jnp.stack([o[i] for o in outs_s], axis=0)
    step = lambda a: a.reshape(depth, n_s, 1, -1)

    def heads_t(a):
        d0, n, _, t = a.shape
        return jnp.transpose(a.reshape(d0, n, N_GROUPS, HEAD_DIM, t), (0, 1, 4, 2, 3))

    return (xp, xs[:, None, :],
            stack_p(0), stack_s(0), step(stack_s(1)),
            heads_t(stack_p(1)), heads_t(stack_p(2)), heads(step(stack_s(2))), heads(step(stack_s(3))),
            heads_t(stack_p(3)), heads_t(stack_p(4)), heads(step(stack_s(4))), heads(step(stack_s(5))),
            heads_t(stack_p(5)), heads_t(stack_p(6)),
            stack_p(7), stack_s(6))
```

```python
import functools
import math

import numpy as np
import jax
import jax.numpy as jnp
from jax import lax
from jax.experimental import pallas as pl
from jax.experimental.pallas import tpu as pltpu

D_MODEL = 1024
HEAD_DIM = 64
BR_WIDTH = 256
N_GROUPS = 4
DIFF_SUB = 32
CHUNK = 128
D_FF = 2816
EPS = 1e-6
NEG_BIG = -1e30
SB_UNDERFLOW = -104.0
LOG2E = 1.4426950408889634
VMEM_LIMIT_BYTES = 56 * 1024 * 1024

F32 = jnp.float32
BF16 = jnp.bfloat16


def _dot(a, b):
    return jnp.dot(a, b, preferred_element_type=F32)


def _dot_nt(a, b):
    return lax.dot_general(a, b, (((1,), (1,)), ((), ())), preferred_element_type=F32)


def _split_bf16(x):
    hi = x.astype(BF16)
    lo = (x - hi.astype(F32)).astype(BF16)
    return hi, lo


def _dot_split(x, w):
    hi, lo = _split_bf16(x)
    return _dot(hi, w) + _dot(lo, w)


def _rms(x, g):
    ms = jnp.mean(x * x, axis=-1, keepdims=True)
    return x * lax.rsqrt(ms + EPS) * g


def _group_rms(z, g, ones_bd, group):
    ms = _dot_split(z * z, ones_bd) * (1.0 / group)
    return z * lax.rsqrt(ms + EPS) * g


def _lane_iota(shape):
    return lax.broadcasted_iota(jnp.int32, shape, len(shape) - 1)


def _row_iota(shape):
    return lax.broadcasted_iota(jnp.int32, shape, len(shape) - 2)


def _div_pow2(x, d):
    assert d & (d - 1) == 0
    return lax.shift_right_logical(x, d.bit_length() - 1)


def _rem_pow2(x, d):
    assert d & (d - 1) == 0
    return x & (d - 1)


def _head_select(parts):
    lane = _lane_iota(parts[0].shape)
    out = parts[0]
    for h in range(1, N_GROUPS):
        out = jnp.where(lane >= HEAD_DIM * h, parts[h], out)
    return out


def _stack_heads(q, n_sub):
    lane = _lane_iota(q.shape)
    width = HEAD_DIM // n_sub
    zero = jnp.zeros_like(q)
    parts = []
    for c in range(n_sub):
        for h in range(N_GROUPS):
            lo = HEAD_DIM * h + width * c
            parts.append(jnp.where((lane >= lo) & (lane < lo + width), q, zero))
    return jnp.concatenate(parts, axis=0)


def _log_sigmoid_pair(z):
    lb = jnp.minimum(z, 0.0) - jnp.log(1.0 + jnp.exp(-jnp.abs(z)))
    return lb, lb - z


def _lam_value(lp, lam_init):
    a = jnp.sum(lp[0:1] * lp[1:2], axis=1, keepdims=True)
    b = jnp.sum(lp[2:3] * lp[3:4], axis=1, keepdims=True)
    return jnp.exp(a) - jnp.exp(b) + lam_init


def _block_ones(group):
    i = np.arange(BR_WIDTH)
    return jnp.asarray((i[:, None] // group) == (i[None, :] // group), dtype=BF16)


def _suffix_ones(n):
    i = np.arange(n)
    l = (i[:, None] > i[None, :])
    return jnp.asarray(np.concatenate([l, l], axis=0), dtype=BF16)


def _alibi_slopes_log2():
    return [LOG2E * 2.0 ** (-8.0 * (h + 1) / N_GROUPS) for h in range(N_GROUPS)]


def _full_spec(shape):
    nd = len(shape)
    return pl.BlockSpec(shape, lambda *_: (0,) * nd, pipeline_mode=pl.Buffered(1))


def _params(*sem):
    return pltpu.CompilerParams(dimension_semantics=sem, vmem_limit_bytes=VMEM_LIMIT_BYTES)


def _group_rms_t(zt, g_col, ones_bd, group):
    hi, lo = _split_bf16(zt * zt)
    ms = (_dot(ones_bd, hi) + _dot(ones_bd, lo)) * (1.0 / group)
    return zt * lax.rsqrt(ms + EPS) * g_col


def _memkv_kernel(mem_ref, g_ref, wt_ref, kg_ref, bd64_ref, k_ref, v_ref, k16_ref, v16_ref):
    hb = _rms(mem_ref[0], g_ref[...]).astype(BF16)
    kvt = _dot_nt(wt_ref[...], hb)
    k = _group_rms_t(kvt[:BR_WIDTH], kg_ref[...], bd64_ref[...], HEAD_DIM)
    v = kvt[BR_WIDTH:]
    k_ref[0] = k
    v_ref[0] = v
    k16_ref[0] = k.astype(BF16)
    v16_ref[0] = v.astype(BF16)


def _mem_kv(mem, g, w, kg, bd64):
    n, m, d = mem.shape
    blk = pl.BlockSpec((1, BR_WIDTH, m), lambda i: (i, 0, 0))
    return pl.pallas_call(
        _memkv_kernel,
        grid=(n,),
        in_specs=[pl.BlockSpec((1, m, d), lambda i: (i, 0, 0)), _full_spec(g.shape), _full_spec(w.shape),
                  _full_spec(kg.shape), _full_spec(bd64.shape)],
        out_specs=[blk, blk, blk, blk],
        out_shape=[jax.ShapeDtypeStruct((n, BR_WIDTH, m), F32)] * 2
        + [jax.ShapeDtypeStruct((n, BR_WIDTH, m), BF16)] * 2,
        compiler_params=_params("arbitrary"),
        name="mem_kv",
    )(mem, g, w, kg, bd64)


def _layer_norm(x, g, b):
    mu = jnp.mean(x, axis=-1, keepdims=True)
    xc = x - mu
    return xc * lax.rsqrt(jnp.mean(xc * xc, axis=-1, keepdims=True) + EPS) * g + b


def _mem_attention(qm, mkt, mvt):
    r = qm.shape[0]
    s = _dot(_stack_heads(qm, 1), mkt)
    p = jnp.exp2(s - jnp.max(s, axis=-1, keepdims=True))
    l = jnp.sum(p, axis=-1, keepdims=True)
    o = _dot_nt(p.astype(BF16), mvt) / l
    return _head_select([o[h * r:(h + 1) * r] for h in range(N_GROUPS)])


def _inproj_kernel(x_ref, g1_ref, win_ref, wsbt_ref, wdft_ref, cw_ref, lng_ref, lnb_ref, ws_ref, gb_ref, dqg_ref,
                   dkg_ref, dkgr_ref, mqg_ref, bd32_ref, bd64_ref, mk_ref, mv_ref,
                   yabm_ref, qsb_ref, ksb16_ref, vsb16_ref, qdf_ref, kdf16_ref, vdf16_ref,
                   ksb_ref, vsb_ref, kdf_ref, vdf_ref, ca_ref, carry_ref, *, tm):
    t = pl.program_id(1)
    hb = _rms(x_ref[0], g1_ref[...]).astype(BF16)

    def proj(lo, hi):
        return _dot(hb, win_ref[:, lo:hi])

    pa = proj(0, 3 * BR_WIDTH)
    ach = pa[:, BR_WIDTH:2 * BR_WIDTH] * pa[:, 2 * BR_WIDTH:]

    @pl.when(t == 0)
    def _():
        carry_ref[...] = jnp.zeros_like(carry_ref)

    prev = carry_ref[...]
    p1, p2 = prev[7:8], prev[6:7]
    row = _row_iota(ach.shape)
    s1 = jnp.where(row == 0, p1, pltpu.roll(ach, 1, axis=0))
    s2 = jnp.where(row == 0, p2, jnp.where(row == 1, p1, pltpu.roll(ach, 2, axis=0)))
    cw = cw_ref[...]
    yabm_ref[0, :, 0:BR_WIDTH] = (pa[:, :BR_WIDTH] * (cw[0:1] * s2 + cw[1:2] * s1 + cw[2:3] * ach)).astype(BF16)
    carry_ref[...] = ach[tm - 8:tm]
    ca_ref[0] = ach[tm - 2:tm]

    pg = proj(3 * BR_WIDTH, 5 * BR_WIDTH)
    vn = _layer_norm(pg[:, BR_WIDTH:], lng_ref[...], lnb_ref[...])
    tril = _row_iota((CHUNK, CHUNK)) >= _lane_iota((CHUNK, CHUNK))
    wsm = [jnp.where(tril, ws_ref[g], 0.0).astype(BF16) for g in range(N_GROUPS)]
    gb = gb_ref[...]
    for c in range(tm // CHUNK):
        vc = vn[c * CHUNK:(c + 1) * CHUNK].astype(BF16)
        mixed = _head_select([_dot(wsm[g], vc) for g in range(N_GROUPS)])
        yabm_ref[0, c * CHUNK:(c + 1) * CHUNK, BR_WIDTH:2 * BR_WIDTH] = (
            pg[c * CHUNK:(c + 1) * CHUNK, :BR_WIDTH] * (mixed + gb)).astype(BF16)

    qsb_ref[0] = (proj(5 * BR_WIDTH, 6 * BR_WIDTH) * (HEAD_DIM ** -0.5)).astype(BF16)
    kvt = _dot_nt(wsbt_ref[...], hb)
    ksb_ref[0] = kvt[:BR_WIDTH]
    vsb_ref[0] = kvt[BR_WIDTH:]
    ksb16_ref[0] = kvt[:BR_WIDTH].astype(BF16)
    vsb16_ref[0] = kvt[BR_WIDTH:].astype(BF16)

    bd32 = bd32_ref[...]
    qd = _group_rms(proj(8 * BR_WIDTH, 9 * BR_WIDTH), dqg_ref[...], bd32, DIFF_SUB)
    qdf_ref[0] = (qd * (DIFF_SUB ** -0.5 * LOG2E)).astype(BF16)
    kvt = _dot_nt(wdft_ref[...], hb)
    kd = _group_rms_t(kvt[:BR_WIDTH], dkg_ref[...], bd32, DIFF_SUB)
    kdf_ref[0] = kd
    vdf_ref[0] = kvt[BR_WIDTH:]
    kdf16_ref[0] = _group_rms(proj(9 * BR_WIDTH, 10 * BR_WIDTH), dkgr_ref[...], bd32, DIFF_SUB).astype(BF16)
    vdf16_ref[0] = kvt[BR_WIDTH:].astype(BF16)

    qm = _group_rms(proj(11 * BR_WIDTH, 12 * BR_WIDTH), mqg_ref[...], bd64_ref[...], HEAD_DIM)
    qm = (qm * (HEAD_DIM ** -0.5 * LOG2E)).astype(BF16)
    yabm_ref[0, :, 2 * BR_WIDTH:] = _mem_attention(qm, mk_ref[0], mv_ref[0]).astype(BF16)


def _inproj(x, p, mk16, mv16, tm):
    n, t, d = x.shape
    consts = [p["norm1_g"], p["w_in"], p["w_sb_kv_t"], p["w_df_kv_t"], p["conv_a_w"], p["gmlp_ln_g"], p["gmlp_ln_b"],
              p["gmlp_ws"], p["gmlp_b_full"], p["diff_qn_g"], p["diff_kn_g_col"], p["diff_kn_g"], p["mem_qn_g"],
              p["bd32"], p["bd64"]]
    seq = lambda w: pl.BlockSpec((1, tm, w), lambda i, j: (i, j, 0))
    seq_t = pl.BlockSpec((1, BR_WIDTH, tm), lambda i, j: (i, 0, j))
    mem_spec = pl.BlockSpec((1,) + mk16.shape[1:], lambda i, j: (i, 0, 0))
    bf = lambda w: jax.ShapeDtypeStruct((n, t, w), BF16)
    bf_t = jax.ShapeDtypeStruct((n, BR_WIDTH, t), BF16)
    fl_t = jax.ShapeDtypeStruct((n, BR_WIDTH, t), F32)
    return pl.pallas_call(
        functools.partial(_inproj_kernel, tm=tm),
        grid=(n, t // tm),
        in_specs=[seq(d)] + [_full_spec(c.shape) for c in consts] + [mem_spec, mem_spec],
        out_specs=[seq(3 * BR_WIDTH), seq(BR_WIDTH), seq_t, seq_t, seq(BR_WIDTH), seq(BR_WIDTH)] + [seq_t] * 5
        + [pl.BlockSpec((1, 2, BR_WIDTH), lambda i, j: (i, 0, 0))],
        out_shape=[bf(3 * BR_WIDTH), bf(BR_WIDTH), bf_t, bf_t, bf(BR_WIDTH), bf(BR_WIDTH), bf_t] + [fl_t] * 4
        + [jax.ShapeDtypeStruct((n, 2, BR_WIDTH), F32)],
        scratch_shapes=[pltpu.VMEM((8, BR_WIDTH), F32)],
        compiler_params=_params("arbitrary", "arbitrary"),
        name="in_proj",
    )(x, *consts, mk16, mv16)


def _sb_kernel(q_ref, k_ref, v_ref, ll_ref, o_ref, acc_ref, *, tq, tk):
    qi = pl.program_id(1)
    qs = _stack_heads(q_ref[0], 1)
    rows = N_GROUPS * tq
    kb_diag = (qi * tq + tq - 1) // tk
    ll = ll_ref[...]

    def block(kb, carry, mask):
        start = pl.multiple_of(kb * tk, tk)
        z = _dot(qs, k_ref[0, :, pl.ds(start, tk)])
        lb, l1m = _log_sigmoid_pair(z)
        if mask is not None:
            l1m = jnp.where(mask, l1m, 0.0)
        hi, lo = _split_bf16(l1m)
        between = _dot(jnp.concatenate([hi, lo], axis=1), ll)
        a = jnp.exp(lb + between + carry)
        if mask is not None:
            a = jnp.where(mask, a, 0.0)
        pv = _dot_nt(a.astype(BF16), v_ref[0, :, pl.ds(start, tk)])
        return pv, carry + jnp.sum(l1m, axis=1, keepdims=True)

    q_pos = qi * tq + _rem_pow2(_row_iota((rows, tk)), tq)
    k_pos = kb_diag * tk + _lane_iota((rows, tk))
    pv, carry = block(kb_diag, jnp.zeros((rows, 1), F32), k_pos < q_pos)
    acc_ref[...] = pv

    def cond(state):
        i, live, _ = state
        return (i < kb_diag) & (live > SB_UNDERFLOW)

    def body(state):
        i, _, carry = state
        pv, carry = block(kb_diag - 1 - i, carry, None)
        acc_ref[...] += pv
        return i + 1, jnp.max(carry), carry

    lax.while_loop(cond, body, (jnp.int32(0), jnp.max(carry), carry))
    acc = acc_ref[...]
    o_ref[0] = _head_select([acc[h * tq:(h + 1) * tq] for h in range(N_GROUPS)]).astype(BF16)


def _sb_attention(q, k, v, tq, tk):
    n, t, w = q.shape
    ll = _suffix_ones(tk)
    kv_spec = pl.BlockSpec((1, w, t), lambda i, j: (i, 0, 0))
    return pl.pallas_call(
        functools.partial(_sb_kernel, tq=tq, tk=tk),
        grid=(n, t // tq),
        in_specs=[pl.BlockSpec((1, tq, w), lambda i, j: (i, j, 0)), kv_spec, kv_spec, _full_spec(ll.shape)],
        out_specs=pl.BlockSpec((1, tq, w), lambda i, j: (i, j, 0)),
        out_shape=jax.ShapeDtypeStruct((n, t, w), BF16),
        scratch_shapes=[pltpu.VMEM((N_GROUPS * tq, w), F32)],
        compiler_params=_params("arbitrary", "arbitrary"),
        name="sb_attn",
    )(q, k, v, ll)


def _diff_kernel(q_ref, k_ref, v_ref, lp_ref, og_ref, bd64_ref, eye_ref, o_ref,
                 qm_ref, bias_ref, m_ref, l_ref, acc_ref, *, tb, lam_init):
    qi = pl.program_id(1)
    slopes = _alibi_slopes_log2()
    rel = _row_iota((tb, tb)) - _lane_iota((tb, tb))

    @pl.when(qi == 0)
    def _():
        for h in range(N_GROUPS):
            bias_ref[h] = slopes[h] * rel.astype(F32)

    q = q_ref[0]
    lane = _lane_iota(q.shape)
    for c in range(2):
        for h in range(N_GROUPS):
            lo = HEAD_DIM * h + DIFF_SUB * c
            ch = c * N_GROUPS + h
            qm_ref[ch * tb:(ch + 1) * tb] = jnp.where((lane >= lo) & (lane < lo + DIFF_SUB), q, jnp.zeros_like(q))

    def block(kb, first):
        start = pl.multiple_of(kb * tb, tb)
        kblk = k_ref[0, pl.ds(start, tb), :]
        off = ((kb - qi) * tb).astype(F32)
        s_all = _dot_nt(kblk, qm_ref[...])
        probs, alphas = [], []
        for c in range(2):
            for h in range(N_GROUPS):
                ch = c * N_GROUPS + h
                s = s_all[:, ch * tb:(ch + 1) * tb] + bias_ref[h]
                shift = slopes[h] * off
                if first:
                    s = jnp.where(rel <= 0, s, NEG_BIG)
                    m_new = jnp.max(s, axis=0, keepdims=True) + shift
                    p = jnp.exp2(s - (m_new - shift))
                    l_ref[ch:ch + 1] = jnp.sum(p, axis=0, keepdims=True)
                    alphas.append(None)
                else:
                    m_old = m_ref[ch:ch + 1]
                    m_new = jnp.maximum(m_old, jnp.max(s, axis=0, keepdims=True) + shift)
                    alpha = jnp.exp2(m_old - m_new)
                    p = jnp.exp2(s - (m_new - shift))
                    l_ref[ch:ch + 1] = alpha * l_ref[ch:ch + 1] + jnp.sum(p, axis=0, keepdims=True)
                    alphas.append(alpha)
                m_ref[ch:ch + 1] = m_new
                probs.append(p.astype(BF16))
        for h in range(N_GROUPS):
            vh = v_ref[0, HEAD_DIM * h:HEAD_DIM * (h + 1), pl.ds(start, tb)]
            pv = _dot(vh, jnp.concatenate([probs[h], probs[N_GROUPS + h]], axis=1))
            for c in range(2):
                ch = c * N_GROUPS + h
                new = pv[:, c * tb:(c + 1) * tb]
                acc_ref[ch] = new if first else alphas[ch] * acc_ref[ch] + new

    block(qi, True)

    def body(i, _):
        block(qi - 1 - i, False)
        return 0

    lax.fori_loop(0, qi, body, 0)
    lam = _lam_value(lp_ref[...], lam_init)
    ys = []
    for h in range(N_GROUPS):
        y0 = acc_ref[h] / l_ref[h:h + 1]
        y1 = acc_ref[N_GROUPS + h] / l_ref[N_GROUPS + h:N_GROUPS + h + 1]
        ys.append(y0 - lam * y1)
    yt = _group_rms_t(jnp.concatenate(ys, axis=0), og_ref[...], bd64_ref[...], HEAD_DIM) * (1.0 - lam_init)
    o_ref[0] = _dot_nt(eye_ref[...], yt.astype(BF16)).astype(BF16)


def _diff_attention(q, k, v, lp, og_col, bd64, lam_init, tb):
    n, t, w = q.shape
    maps = 2 * N_GROUPS
    eye = jnp.asarray(np.eye(tb), dtype=BF16)
    return pl.pallas_call(
        functools.partial(_diff_kernel, tb=tb, lam_init=lam_init),
        grid=(n, t // tb),
        in_specs=[pl.BlockSpec((1, tb, w), lambda i, j: (i, j, 0)),
                  pl.BlockSpec((1, t, w), lambda i, j: (i, 0, 0)),
                  pl.BlockSpec((1, w, t), lambda i, j: (i, 0, 0)),
                  _full_spec(lp.shape), _full_spec(og_col.shape), _full_spec(bd64.shape), _full_spec(eye.shape)],
        out_specs=pl.BlockSpec((1, tb, w), lambda i, j: (i, j, 0)),
        out_shape=jax.ShapeDtypeStruct((n, t, w), BF16),
        scratch_shapes=[pltpu.VMEM((maps * tb, w), BF16), pltpu.VMEM((N_GROUPS, tb, tb), F32),
                        pltpu.VMEM((maps, tb), F32), pltpu.VMEM((maps, tb), F32),
                        pltpu.VMEM((maps, HEAD_DIM, tb), F32)],
        compiler_params=_params("arbitrary", "arbitrary"),
        name="diff_attn",
    )(q, k, v, lp, og_col, bd64, eye)


def _merge_kernel(x_ref, yabm_ref, yc_ref, yd_ref, g1_ref, wg_ref, bg_ref, wb_ref, wo_ref, o_ref):
    x = x_ref[0]
    hb = _rms(x, g1_ref[...]).astype(BF16)
    ys = [yabm_ref[0, :, 0:BR_WIDTH], yabm_ref[0, :, BR_WIDTH:2 * BR_WIDTH], yc_ref[0], yd_ref[0],
          yabm_ref[0, :, 2 * BR_WIDTH:]]
    acc = None
    for b in range(5):
        gate = jax.nn.sigmoid(_dot(hb, wg_ref[:, b * D_MODEL:(b + 1) * D_MODEL]) + bg_ref[:, b * D_MODEL:(b + 1) * D_MODEL])
        term = gate * _dot(ys[b], wb_ref[b])
        acc = term if acc is None else acc + term
    o_ref[0] = x + _dot(acc.astype(BF16), wo_ref[...])


def _merge(x, yabm, yc, yd, p, tm):
    n, t, d = x.shape
    consts = [p["norm1_g"], p["w_gate"], p["b_gate"], p["w_branch"], p["w_o"]]
    seq = lambda w: pl.BlockSpec((1, tm, w), lambda i, j: (i, j, 0))
    return pl.pallas_call(
        _merge_kernel,
        grid=(n, t // tm),
        in_specs=[seq(d), seq(3 * BR_WIDTH), seq(BR_WIDTH), seq(BR_WIDTH)] + [_full_spec(c.shape) for c in consts],
        out_specs=seq(d),
        out_shape=jax.ShapeDtypeStruct((n, t, d), F32),
        compiler_params=_params("arbitrary", "arbitrary"),
        name="merge",
    )(x, yabm, yc, yd, *consts)


FFN_COLS = 1408


def _ffn_kernel(x_ref, g2_ref, wup_ref, cw_ref, cb_ref, wdn_ref, o_ref, st_ref, carry_ref, *, tm):
    t = pl.program_id(1)
    x = x_ref[0]
    hb = _rms(x, g2_ref[...]).astype(BF16)

    @pl.when(t == 0)
    def _():
        carry_ref[...] = jnp.zeros_like(carry_ref)

    row = _row_iota((tm, FFN_COLS))

    def conv(lo):
        u = _dot(hb, wup_ref[:, lo:lo + FFN_COLS])
        prev = carry_ref[:, lo:lo + FFN_COLS]
        p1, p2 = prev[7:8], prev[6:7]
        s1 = jnp.where(row == 0, p1, pltpu.roll(u, 1, axis=0))
        s2 = jnp.where(row == 0, p2, jnp.where(row == 1, p1, pltpu.roll(u, 2, axis=0)))
        carry_ref[:, lo:lo + FFN_COLS] = u[tm - 8:tm]
        st_ref[0, :, lo:lo + FFN_COLS] = u[tm - 2:tm]
        return (cw_ref[0:1, lo:lo + FFN_COLS] * s2 + cw_ref[1:2, lo:lo + FFN_COLS] * s1
                + cw_ref[2:3, lo:lo + FFN_COLS] * u + cb_ref[:, lo:lo + FFN_COLS])

    acc = x
    for j in range(D_FF // FFN_COLS):
        gate = conv(j * FFN_COLS)
        val = conv(D_FF + j * FFN_COLS)
        act = (gate * jax.nn.sigmoid(gate) * val).astype(BF16)
        acc = acc + _dot(act, wdn_ref[j * FFN_COLS:(j + 1) * FFN_COLS, :])
    o_ref[0] = acc


def _ffn(x, p, tm):
    n, t, d = x.shape
    consts = [p["norm2_g"], p["w_up"], p["conv_ffn_w"], p["conv_ffn_b"], p["w_down"]]
    seq = pl.BlockSpec((1, tm, d), lambda i, j: (i, j, 0))
    return pl.pallas_call(
        functools.partial(_ffn_kernel, tm=tm),
        grid=(n, t // tm),
        in_specs=[seq] + [_full_spec(c.shape) for c in consts],
        out_specs=[seq, pl.BlockSpec((1, 2, 2 * D_FF), lambda i, j: (i, 0, 0))],
        out_shape=[jax.ShapeDtypeStruct((n, t, d), F32), jax.ShapeDtypeStruct((n, 2, 2 * D_FF), F32)],
        scratch_shapes=[pltpu.VMEM((8, 2 * D_FF), F32)],
        compiler_params=_params("arbitrary", "arbitrary"),
        name="conv_ffn",
    )(x, *consts)


def _ffn_step_kernel(x_ref, p0_ref, p1_ref, g2_ref, wup_ref, cw_ref, cb_ref, wdn_ref, o_ref, u_ref):
    x = x_ref[...]
    hb = _rms(x, g2_ref[...]).astype(BF16)

    def conv(lo):
        u = _dot(hb, wup_ref[:, lo:lo + FFN_COLS])
        u_ref[:, lo:lo + FFN_COLS] = u
        return (cw_ref[0:1, lo:lo + FFN_COLS] * p0_ref[:, lo:lo + FFN_COLS]
                + cw_ref[1:2, lo:lo + FFN_COLS] * p1_ref[:, lo:lo + FFN_COLS]
                + cw_ref[2:3, lo:lo + FFN_COLS] * u + cb_ref[:, lo:lo + FFN_COLS])

    acc = x
    for j in range(D_FF // FFN_COLS):
        gate = conv(j * FFN_COLS)
        val = conv(D_FF + j * FFN_COLS)
        act = (gate * jax.nn.sigmoid(gate) * val).astype(BF16)
        acc = acc + _dot(act, wdn_ref[j * FFN_COLS:(j + 1) * FFN_COLS, :])
    o_ref[...] = acc


def _ffn_step(x, prev0, prev1, p):
    m, d = x.shape
    consts = [p["norm2_g"], p["w_up"], p["conv_ffn_w"], p["conv_ffn_b"], p["w_down"]]
    args = [x, prev0, prev1] + consts
    return pl.pallas_call(
        _ffn_step_kernel,
        grid=(1,),
        in_specs=[_full_spec(a.shape) for a in args],
        out_specs=[_full_spec((m, d)), _full_spec((m, 2 * D_FF))],
        out_shape=[jax.ShapeDtypeStruct((m, d), F32), jax.ShapeDtypeStruct((m, 2 * D_FF), F32)],
        compiler_params=_params("arbitrary"),
        name="conv_ffn_step",
    )(*args)


def _inproj_step_kernel(x_ref, c0_ref, c1_ref, g1_ref, win_ref, cw_ref, lng_ref, lnb_ref, ws_ref, gb_ref,
                        dqg_ref, dkg_ref, mqg_ref, bd32_ref, bd64_ref,
                        yab_ref, q_ref, ksb_ref, vsb_ref, kdf_ref, vdf_ref, vn_ref, ach_ref):
    hb = _rms(x_ref[...], g1_ref[...]).astype(BF16)

    def proj(lo, hi):
        return _dot(hb, win_ref[:, lo:hi])

    pa = proj(0, 3 * BR_WIDTH)
    ach = pa[:, BR_WIDTH:2 * BR_WIDTH] * pa[:, 2 * BR_WIDTH:]
    cw = cw_ref[...]
    yab_ref[:, 0:BR_WIDTH] = (pa[:, :BR_WIDTH] * (cw[0:1] * c0_ref[...] + cw[1:2] * c1_ref[...] + cw[2:3] * ach)).astype(BF16)
    ach_ref[...] = ach

    pg = proj(3 * BR_WIDTH, 5 * BR_WIDTH)
    vn = _layer_norm(pg[:, BR_WIDTH:], lng_ref[...], lnb_ref[...])
    vn_ref[...] = vn
    lane = _lane_iota((1, BR_WIDTH))
    w00 = jnp.zeros((1, BR_WIDTH), F32)
    for g in range(N_GROUPS):
        w00 = jnp.where((lane >= HEAD_DIM * g) & (lane < HEAD_DIM * (g + 1)), ws_ref[g, 0:1, 0:1], w00)
    yab_ref[:, BR_WIDTH:] = (pg[:, :BR_WIDTH] * (w00 * vn + gb_ref[0:1, :])).astype(BF16)

    ps = proj(5 * BR_WIDTH, 8 * BR_WIDTH)
    q_ref[:, 0:BR_WIDTH] = (ps[:, :BR_WIDTH] * (HEAD_DIM ** -0.5)).astype(BF16)
    ksb_ref[...] = ps[:, BR_WIDTH:2 * BR_WIDTH]
    vsb_ref[...] = ps[:, 2 * BR_WIDTH:]

    pd = proj(8 * BR_WIDTH, 11 * BR_WIDTH)
    bd32 = bd32_ref[...]
    qd = _group_rms(pd[:, :BR_WIDTH], dqg_ref[...], bd32, DIFF_SUB)
    q_ref[:, BR_WIDTH:2 * BR_WIDTH] = (qd * (DIFF_SUB ** -0.5 * LOG2E)).astype(BF16)
    kdf_ref[...] = _group_rms(pd[:, BR_WIDTH:2 * BR_WIDTH], dkg_ref[...], bd32, DIFF_SUB)
    vdf_ref[...] = pd[:, 2 * BR_WIDTH:]

    qm = _group_rms(proj(11 * BR_WIDTH, 12 * BR_WIDTH), mqg_ref[...], bd64_ref[...], HEAD_DIM)
    q_ref[:, 2 * BR_WIDTH:] = (qm * (HEAD_DIM ** -0.5 * LOG2E)).astype(BF16)


def _inproj_step(x, c0, c1, p):
    m, d = x.shape
    consts = [p["norm1_g"], p["w_in"], p["conv_a_w"], p["gmlp_ln_g"], p["gmlp_ln_b"], p["gmlp_ws"],
              p["gmlp_b_full"], p["diff_qn_g"], p["diff_kn_g"], p["mem_qn_g"], p["bd32"], p["bd64"]]
    args = [x, c0, c1] + consts
    fl = jax.ShapeDtypeStruct((m, BR_WIDTH), F32)
    out_shape = [jax.ShapeDtypeStruct((m, 2 * BR_WIDTH), BF16), jax.ShapeDtypeStruct((m, 3 * BR_WIDTH), BF16)] + [fl] * 6
    return pl.pallas_call(
        _inproj_step_kernel,
        grid=(1,),
        in_specs=[_full_spec(a.shape) for a in args],
        out_specs=[_full_spec(s.shape) for s in out_shape],
        out_shape=out_shape,
        compiler_params=_params("arbitrary"),
        name="in_proj_step",
    )(*args)


def _decode_kernel(pt_ref, q_ref, kn_ref, vn_ref, mk_ref, mv_ref, lp_ref, og_ref, bd64_ref, ll_ref, ones_ref,
                   ksb_hbm, vsb_hbm, kdf_hbm, vdf_hbm, o_ref, buf_ref, sem_ref, *, layer, n_pages, page, lam_init):
    step = pl.program_id(0)
    pools = (ksb_hbm, vsb_hbm, kdf_hbm, vdf_hbm)

    def page_copy(sample, slot, c, j):
        return pltpu.make_async_copy(pools[c].at[layer, pt_ref[sample, j]], buf_ref.at[slot, c, j], sem_ref.at[slot])

    def for_each_page(sample, slot, fn):
        for c in range(len(pools)):
            for j in range(n_pages):
                fn(page_copy(sample, slot, c, j))

    @pl.when(step == 0)
    def _():
        for_each_page(0, 0, lambda cp: cp.start())

    @pl.when(step + 1 < pl.num_programs(0))
    def _():
        for_each_page(step + 1, (step + 1) & 1, lambda cp: cp.start())

    slot = step & 1
    for_each_page(step, slot, lambda cp: cp.wait())
    ksb, vsb, kdf, vdf = ([buf_ref.at[slot, c, j] for j in range(n_pages)] for c in range(len(pools)))
    rows = 2 * N_GROUPS
    n_sub = HEAD_DIM // DIFF_SUB
    ones = ones_ref[...]

    def q_column(col):
        qrow = jnp.broadcast_to(q_ref[0, :, col * BR_WIDTH:(col + 1) * BR_WIDTH].astype(F32), (BR_WIDTH, BR_WIDTH))
        diag = jnp.where(_row_iota(qrow.shape) == _lane_iota(qrow.shape), qrow, 0.0).astype(BF16)
        return _dot(diag, ones)

    def group_scores(kt, qcol):
        return jnp.sum((kt * qcol).reshape(rows, DIFF_SUB, kt.shape[1]), axis=1)

    def pair_sum(z):
        r = z.shape[0]
        even = (_row_iota(z.shape) & 1) == 0
        return z + jnp.where(even, pltpu.roll(z, r - 1, axis=0), pltpu.roll(z, 1, axis=0))

    def head_rows(w):
        return jnp.concatenate([jnp.broadcast_to(w[n_sub * h:n_sub * h + 1], (HEAD_DIM, w.shape[1]))
                                for h in range(N_GROUPS)], axis=0)

    def reduce_positions(acc):
        hi, lo = _split_bf16(acc)
        ones_row = jnp.ones((rows, acc.shape[1]), BF16)
        return _dot_nt(ones_row, hi) + _dot_nt(ones_row, lo)

    qc = q_column(0)
    z = jnp.concatenate([pair_sum(group_scores(ksb[pg][...], qc)) for pg in range(n_pages)], axis=0)
    lb, l1m = _log_sigmoid_pair(z)
    hi, lo = _split_bf16(l1m)
    hl = jnp.concatenate([hi, lo], axis=1)
    within = _dot(hl, ll_ref[...])
    total = _dot(hl, ones)
    carry = jnp.zeros((rows, page), F32)
    carries = [None] * n_pages
    for pg in reversed(range(n_pages)):
        carries[pg] = carry
        carry = carry + total[rows * pg:rows * (pg + 1)]
    a = jnp.exp(lb + within + jnp.concatenate(carries, axis=0))
    acc = jnp.zeros((BR_WIDTH, page), F32)
    for pg in range(n_pages):
        acc = acc + head_rows(a[rows * pg:rows * (pg + 1)]) * vsb[pg][...]
    y_c = reduce_positions(acc)[0:1]

    qc = q_column(1)
    s = jnp.concatenate([group_scores(kdf[pg][...], qc) for pg in range(n_pages)], axis=0)
    slopes = _alibi_slopes_log2()
    all_rows = _row_iota((rows * n_pages, page))
    head = _div_pow2(_rem_pow2(all_rows, rows), n_sub)
    sl = jnp.full(all_rows.shape, slopes[0], F32)
    for h in range(1, N_GROUPS):
        sl = jnp.where(head == h, slopes[h], sl)
    past = n_pages * page
    k_pos = _div_pow2(all_rows, rows) * page + _lane_iota(all_rows.shape)
    s = s + sl * (k_pos - past).astype(F32)
    row8 = _row_iota((rows, BR_WIDTH))
    group_lanes = _div_pow2(_lane_iota((rows, BR_WIDTH)), DIFF_SUB) == row8
    q8 = jnp.broadcast_to(q_ref[0, :, BR_WIDTH:2 * BR_WIDTH].astype(F32), (rows, BR_WIDTH))
    kn8 = jnp.broadcast_to(kn_ref[0], (rows, BR_WIDTH))
    s_new = jnp.sum(jnp.where(group_lanes, q8 * kn8, 0.0), axis=1, keepdims=True)
    m8 = s[0:rows]
    for pg in range(1, n_pages):
        m8 = jnp.maximum(m8, s[rows * pg:rows * (pg + 1)])
    m = jnp.maximum(jnp.max(m8, axis=1, keepdims=True), s_new)
    p_new = jnp.exp2(s_new - m)
    m_all = jnp.concatenate([jnp.broadcast_to(m, (rows, page))] * n_pages, axis=0)
    p = jnp.exp2(s - m_all)
    l8 = p[0:rows]
    for pg in range(1, n_pages):
        l8 = l8 + p[rows * pg:rows * (pg + 1)]
    l = jnp.sum(l8, axis=1, keepdims=True) + p_new
    lam = _lam_value(lp_ref[...], lam_init)
    first_map = (_row_iota((rows, 1)) & 1) == 0
    coef = jnp.where(first_map, 1.0, -lam) / l
    w = pair_sum(p * jnp.concatenate([jnp.broadcast_to(coef, (rows, page))] * n_pages, axis=0))
    acc = jnp.zeros((BR_WIDTH, page), F32)
    for pg in range(n_pages):
        acc = acc + head_rows(w[rows * pg:rows * (pg + 1)]) * vdf[pg][...]
    w_new = pair_sum(p_new * coef)
    head_lanes = (_div_pow2(_lane_iota((rows, BR_WIDTH)), HEAD_DIM) * n_sub) == row8
    vn8 = jnp.broadcast_to(vn_ref[0], (rows, BR_WIDTH))
    y_new = jnp.sum(jnp.where(head_lanes, w_new * vn8, 0.0), axis=0, keepdims=True)
    y_d = reduce_positions(acc)[0:1] + y_new
    y_d = _group_rms(jnp.broadcast_to(y_d, (rows, BR_WIDTH)), og_ref[...], bd64_ref[...], HEAD_DIM)[0:1]
    y_d = y_d * (1.0 - lam_init)

    qc = q_column(2)
    n_mem = mk_ref.shape[2]
    qc = jnp.concatenate([qc] * (n_mem // page), axis=1)
    s = pair_sum(group_scores(mk_ref[0], qc))
    p = jnp.exp2(s - jnp.max(s, axis=1, keepdims=True))
    p = p / jnp.sum(p, axis=1, keepdims=True)
    y_m = reduce_positions(head_rows(p) * mv_ref[0])[0:1]

    o_ref[0, :, 0:BR_WIDTH] = y_c.astype(BF16)
    o_ref[0, :, BR_WIDTH:2 * BR_WIDTH] = y_d.astype(BF16)
    o_ref[0, :, 2 * BR_WIDTH:] = y_m.astype(BF16)


def _decode_attention(layer, page_table, q, k_new, v_new, caches, mem_k, mem_v, p, lam_init):
    b = q.shape[0]
    n_pages = page_table.shape[1]
    page = caches[0].shape[3]
    ll = _suffix_ones(page)
    row_spec = lambda w: pl.BlockSpec((1, 1, w), lambda i, pt: (i, 0, 0))
    mem_spec = pl.BlockSpec((None, 1) + mem_k.shape[2:], lambda i, pt: (layer, i, 0, 0))
    const = lambda a: pl.BlockSpec(a.shape, lambda i, pt: (0,) * a.ndim)
    consts = [p["diff_lambda"], p["diff_out_g"], p["bd64"], ll, jnp.ones((BR_WIDTH, page), BF16)]
    grid_spec = pltpu.PrefetchScalarGridSpec(
        num_scalar_prefetch=1,
        grid=(b,),
        in_specs=[row_spec(3 * BR_WIDTH), row_spec(BR_WIDTH), row_spec(BR_WIDTH), mem_spec, mem_spec]
        + [const(c) for c in consts] + [pl.BlockSpec(memory_space=pl.ANY)] * len(caches),
        out_specs=row_spec(3 * BR_WIDTH),
        scratch_shapes=[pltpu.VMEM((2, len(caches), n_pages, BR_WIDTH, page), F32), pltpu.SemaphoreType.DMA((2,))],
    )
    return pl.pallas_call(
        functools.partial(_decode_kernel, layer=layer, n_pages=n_pages, page=page, lam_init=lam_init),
        grid_spec=grid_spec,
        out_shape=jax.ShapeDtypeStruct((b, 1, 3 * BR_WIDTH), BF16),
        compiler_params=_params("arbitrary"),
        name="decode_attn",
    )(page_table, q, k_new, v_new, mem_k, mem_v, *consts, *caches)


PROMPT_TILE = 512
ATTN_TQ = 128
ATTN_TK = 256
DIFF_BLOCK = 256


def _layer_params(l, w):
    row = lambda a: a[l].reshape(1, -1).astype(F32)
    tile4 = lambda a: jnp.tile(a[l].astype(F32), N_GROUPS).reshape(1, -1)
    col4 = lambda a: jnp.tile(a[l].astype(F32), N_GROUPS).reshape(-1, 1)
    w_in = w["w_in"][l].astype(BF16)
    return dict(
        norm1_g=row(w["norm1_g"]), w_in=w_in, conv_a_w=w["conv_a_w"][l].astype(F32),
        w_sb_kv_t=w_in[:, 6 * BR_WIDTH:8 * BR_WIDTH].T, w_df_kv_t=w_in[:, 9 * BR_WIDTH:11 * BR_WIDTH].T,
        diff_kn_g_col=col4(w["diff_kn_g"]), mem_kn_g_col=col4(w["mem_kn_g"]), diff_out_g_col=col4(w["diff_out_g"]),
        w_mem_kv_t=w["w_mem_kv"][l].astype(BF16).T,
        gmlp_ln_g=row(w["gmlp_ln_g"]), gmlp_ln_b=row(w["gmlp_ln_b"]), gmlp_ws=w["gmlp_ws"][l].astype(F32),
        gmlp_b_full=jnp.repeat(w["gmlp_b"][l].astype(F32).T, HEAD_DIM, axis=1),
        diff_qn_g=tile4(w["diff_qn_g"]), diff_kn_g=tile4(w["diff_kn_g"]), diff_out_g=tile4(w["diff_out_g"]),
        diff_lambda=w["diff_lambda"][l].astype(F32),
        mem_norm_g=row(w["mem_norm_g"]), mem_qn_g=tile4(w["mem_qn_g"]),
        w_branch=w["w_branch"][l].astype(BF16), w_gate=w["w_gate"][l].astype(BF16), b_gate=row(w["b_gate"]),
        w_o=w["w_o"][l].astype(BF16), norm2_g=row(w["norm2_g"]), w_up=w["w_up"][l].astype(BF16),
        conv_ffn_w=w["conv_ffn_w"][l].astype(F32), conv_ffn_b=row(w["conv_ffn_b"]), w_down=w["w_down"][l].astype(BF16),
        bd32=_block_ones(DIFF_SUB), bd64=_block_ones(HEAD_DIM),
    )


def _prompt_layer(x, mem, p, lam_init):
    n, t, _ = x.shape
    tm = min(PROMPT_TILE, t)
    tq, tk = min(ATTN_TQ, t), min(ATTN_TK, t)
    mk, mv, mk16, mv16 = _mem_kv(mem, p["mem_norm_g"], p["w_mem_kv_t"], p["mem_kn_g_col"], p["bd64"])
    (yabm, qsb, ksb16, vsb16, qdf, kdf16, vdf16, ksb, vsb, kdf, vdf, ca) = _inproj(x, p, mk16, mv16, tm)
    yc = _sb_attention(qsb, ksb16, vsb16, tq, tk)
    yd = _diff_attention(qdf, kdf16, vdf16, p["diff_lambda"], p["diff_out_g_col"], p["bd64"], lam_init,
                         min(DIFF_BLOCK, t))
    x1 = _merge(x, yabm, yc, yd, p, tm)
    x2, cf = _ffn(x1, p, tm)
    return x2, ca, ksb, vsb, kdf, vdf, mk, mv, cf


def _sample_layer(l, x, conv_a, conv_ffn, page_table, caches, mem_k, mem_v, p, lam_init):
    b = x.shape[0]
    yab, q, ksb, vsb, kdf, vdf, vn, ach = _inproj_step(x, conv_a[:, 0], conv_a[:, 1], p)
    row3 = lambda a: a.reshape(b, 1, -1)
    ycdm = _decode_attention(l, page_table, row3(q), row3(kdf), row3(vdf), caches, mem_k, mem_v, p, lam_init)
    yabm = jnp.concatenate([yab, ycdm[:, 0, 2 * BR_WIDTH:]], axis=1)
    x1 = _merge(x[None], yabm[None], ycdm[None, :, 0, 0:BR_WIDTH], ycdm[None, :, 0, BR_WIDTH:2 * BR_WIDTH], p, b)[0]
    x2, u = _ffn_step(x1, conv_ffn[:, 0], conv_ffn[:, 1], p)
    ca_new = jnp.stack([conv_a[:, 1], ach], axis=1)
    cf_new = jnp.stack([conv_ffn[:, 1], u], axis=1)
    return x2, ca_new, vn, ksb, vsb, kdf, vdf, cf_new


def kernel(x_prompt, x_sample, state_conv_a, cache_k_sb, cache_v_sb, cache_k_diff, cache_v_diff, cache_mem_k,
           cache_mem_v, state_conv_ffn, page_table, mem_prompt, norm1_g, w_in, conv_a_w, gmlp_ln_g, gmlp_ln_b,
           gmlp_ws, gmlp_b, diff_qn_g, diff_kn_g, diff_lambda, diff_out_g, mem_norm_g, w_mem_kv, mem_qn_g,
           mem_kn_g, w_branch, w_gate, b_gate, w_o, norm2_g, w_up, conv_ffn_w, conv_ffn_b, w_down):
    w = dict(norm1_g=norm1_g, w_in=w_in, conv_a_w=conv_a_w, gmlp_ln_g=gmlp_ln_g, gmlp_ln_b=gmlp_ln_b,
             gmlp_ws=gmlp_ws, gmlp_b=gmlp_b, diff_qn_g=diff_qn_g, diff_kn_g=diff_kn_g, diff_lambda=diff_lambda,
             diff_out_g=diff_out_g, mem_norm_g=mem_norm_g, w_mem_kv=w_mem_kv, mem_qn_g=mem_qn_g,
             mem_kn_g=mem_kn_g, w_branch=w_branch, w_gate=w_gate, b_gate=b_gate, w_o=w_o, norm2_g=norm2_g,
             w_up=w_up, conv_ffn_w=conv_ffn_w, conv_ffn_b=conv_ffn_b, w_down=w_down)
    depth = w_in.shape[0]
    n_p, t_p, _ = x_prompt.shape
    n_s = x_sample.shape[0]
    flat = lambda c: jnp.transpose(c, (0, 1, 3, 4, 2)).reshape(c.shape[:2] + (BR_WIDTH, c.shape[2]))
    caches = [flat(cache_k_sb), flat(cache_v_sb), flat(cache_k_diff), flat(cache_v_diff)]
    mem_k, mem_v = flat(cache_mem_k), flat(cache_mem_v)
    xp, xs = x_prompt, x_sample[:, 0]
    outs_p, outs_s = [], []
    for l in range(depth):
        p = _layer_params(l, w)
        lam_init = 0.8 - 0.6 * math.exp(-0.3 * l)
        xp, *rest_p = _prompt_layer(xp, mem_prompt, p, lam_init)
        outs_p.append(rest_p)
        xs, *rest_s = _sample_layer(l, xs, state_conv_a[l], state_conv_ffn[l], page_table, caches, mem_k, mem_v,
                                    p, lam_init)
        outs_s.append(rest_s)
    heads = lambda a: a.reshape(a.shape[:-1] + (N_GROUPS, HEAD_DIM))
    stack_p = lambda i: jnp.stack([o[i] for o in outs_p], axis=0)
    stack_s = lambda i: jnp.stack([o[i] for o in outs_s], axis=0)
    step = lambda a: a.reshape(depth, n_s, 1, -1)

    def heads_t(a):
        d0, n, _, t = a.shape
        return jnp.transpose(a.reshape(d0, n, N_GROUPS, HEAD_DIM, t), (0, 1, 4, 2, 3))

    return (xp, xs[:, None, :],
            stack_p(0), stack_s(0), step(stack_s(1)),
            heads_t(stack_p(1)), heads_t(stack_p(2)), heads(step(stack_s(2))), heads(step(stack_s(3))),
            heads_t(stack_p(3)), heads_t(stack_p(4)), heads(step(stack_s(4))), heads(step(stack_s(5))),
            heads_t(stack_p(5)), heads_t(stack_p(6)),
            stack_p(7), stack_s(6))
```

```python
import functools
import math

import numpy as np
import jax
import jax.numpy as jnp
from jax import lax
from jax.experimental import pallas as pl
from jax.experimental.pallas import tpu as pltpu

D_MODEL = 1024
HEAD_DIM = 64
BR_WIDTH = 256
N_GROUPS = 4
DIFF_SUB = 32
CHUNK = 128
D_FF = 2816
EPS = 1e-6
NEG_BIG = -1e30
SB_UNDERFLOW_LOG2 = -150.0
LOG2E = 1.4426950408889634
VMEM_LIMIT_BYTES = 56 * 1024 * 1024

F32 = jnp.float32
BF16 = jnp.bfloat16


def _dot(a, b):
    return jnp.dot(a, b, preferred_element_type=F32)


def _dot_nt(a, b):
    return lax.dot_general(a, b, (((1,), (1,)), ((), ())), preferred_element_type=F32)


def _split_bf16(x):
    hi = x.astype(BF16)
    lo = (x - hi.astype(F32)).astype(BF16)
    return hi, lo


def _dot_split(x, w):
    hi, lo = _split_bf16(x)
    return _dot(hi, w) + _dot(lo, w)


def _rms(x, g):
    ms = jnp.mean(x * x, axis=-1, keepdims=True)
    return x * lax.rsqrt(ms + EPS) * g


def _group_rms(z, g, ones_bd, group):
    ms = _dot_split(z * z, ones_bd) * (1.0 / group)
    return z * lax.rsqrt(ms + EPS) * g


def _lane_iota(shape):
    return lax.broadcasted_iota(jnp.int32, shape, len(shape) - 1)


def _row_iota(shape):
    return lax.broadcasted_iota(jnp.int32, shape, len(shape) - 2)


def _div_pow2(x, d):
    assert d & (d - 1) == 0
    return lax.shift_right_logical(x, d.bit_length() - 1)


def _rem_pow2(x, d):
    assert d & (d - 1) == 0
    return x & (d - 1)


def _head_select(parts):
    lane = _lane_iota(parts[0].shape)
    out = parts[0]
    for h in range(1, N_GROUPS):
        out = jnp.where(lane >= HEAD_DIM * h, parts[h], out)
    return out


def _stack_heads(q, n_sub):
    lane = _lane_iota(q.shape)
    width = HEAD_DIM // n_sub
    zero = jnp.zeros_like(q)
    parts = []
    for c in range(n_sub):
        for h in range(N_GROUPS):
            lo = HEAD_DIM * h + width * c
            parts.append(jnp.where((lane >= lo) & (lane < lo + width), q, zero))
    return jnp.concatenate(parts, axis=0)


def _log_sigmoid_pair(z):
    lb = jnp.minimum(z, 0.0) - jnp.log(1.0 + jnp.exp(-jnp.abs(z)))
    return lb, lb - z


def _log2_sigmoid_pair(z2):
    lb = jnp.minimum(z2, 0.0) - jnp.log2(1.0 + jnp.exp2(-jnp.abs(z2)))
    return lb, lb - z2


def _lam_value(lp, lam_init):
    a = jnp.sum(lp[0:1] * lp[1:2], axis=1, keepdims=True)
    b = jnp.sum(lp[2:3] * lp[3:4], axis=1, keepdims=True)
    return jnp.exp(a) - jnp.exp(b) + lam_init


def _block_ones(group):
    i = np.arange(BR_WIDTH)
    return jnp.asarray((i[:, None] // group) == (i[None, :] // group), dtype=BF16)


def _suffix_ones(n):
    i = np.arange(n)
    l = (i[:, None] > i[None, :])
    return jnp.asarray(np.concatenate([l, l], axis=0), dtype=BF16)


def _alibi_slopes_log2():
    return [LOG2E * 2.0 ** (-8.0 * (h + 1) / N_GROUPS) for h in range(N_GROUPS)]


def _full_spec(shape):
    nd = len(shape)
    return pl.BlockSpec(shape, lambda *_: (0,) * nd, pipeline_mode=pl.Buffered(1))


def _params(*sem):
    return pltpu.CompilerParams(dimension_semantics=sem, vmem_limit_bytes=VMEM_LIMIT_BYTES)


def _group_rms_t(zt, g_col, ones_bd, group):
    hi, lo = _split_bf16(zt * zt)
    ms = (_dot(ones_bd, hi) + _dot(ones_bd, lo)) * (1.0 / group)
    return zt * lax.rsqrt(ms + EPS) * g_col


def _memkv_kernel(mem_ref, g_ref, wt_ref, kg_ref, bd64_ref, k_ref, v_ref, k16_ref, v16_ref):
    hb = _rms(mem_ref[0], g_ref[...]).astype(BF16)
    kvt = _dot_nt(wt_ref[...], hb)
    k = _group_rms_t(kvt[:BR_WIDTH], kg_ref[...], bd64_ref[...], HEAD_DIM)
    v = kvt[BR_WIDTH:]
    k_ref[0] = k
    v_ref[0] = v
    k16_ref[0] = k.astype(BF16)
    v16_ref[0] = v.astype(BF16)


def _mem_kv(mem, g, w, kg, bd64):
    n, m, d = mem.shape
    blk = pl.BlockSpec((1, BR_WIDTH, m), lambda i: (i, 0, 0))
    return pl.pallas_call(
        _memkv_kernel,
        grid=(n,),
        in_specs=[pl.BlockSpec((1, m, d), lambda i: (i, 0, 0)), _full_spec(g.shape), _full_spec(w.shape),
                  _full_spec(kg.shape), _full_spec(bd64.shape)],
        out_specs=[blk, blk, blk, blk],
        out_shape=[jax.ShapeDtypeStruct((n, BR_WIDTH, m), F32)] * 2
        + [jax.ShapeDtypeStruct((n, BR_WIDTH, m), BF16)] * 2,
        compiler_params=_params("arbitrary"),
        name="mem_kv",
    )(mem, g, w, kg, bd64)


def _layer_norm(x, g, b):
    mu = jnp.mean(x, axis=-1, keepdims=True)
    xc = x - mu
    return xc * lax.rsqrt(jnp.mean(xc * xc, axis=-1, keepdims=True) + EPS) * g + b


def _mem_attention(qm, mkt, mvt):
    r = qm.shape[0]
    s = _dot(_stack_heads(qm, 1), mkt)
    p = jnp.exp2(s - jnp.max(s, axis=-1, keepdims=True))
    l = jnp.sum(p, axis=-1, keepdims=True)
    o = _dot_nt(p.astype(BF16), mvt) / l
    return _head_select([o[h * r:(h + 1) * r] for h in range(N_GROUPS)])


def _inproj_kernel(x_ref, g1_ref, win_ref, wsbt_ref, wdft_ref, cw_ref, lng_ref, lnb_ref, ws_ref, gb_ref, dqg_ref,
                   dkg_ref, dkgr_ref, mqg_ref, bd32_ref, bd64_ref, mk_ref, mv_ref,
                   yabm_ref, qsb_ref, ksb16_ref, vsb16_ref, qdf_ref, kdf16_ref, vdf16_ref,
                   ksb_ref, vsb_ref, kdf_ref, vdf_ref, ca_ref, carry_ref, *, tm):
    t = pl.program_id(1)
    hb = _rms(x_ref[0], g1_ref[...]).astype(BF16)

    def proj(lo, hi):
        return _dot(hb, win_ref[:, lo:hi])

    pa = proj(0, 3 * BR_WIDTH)
    ach = pa[:, BR_WIDTH:2 * BR_WIDTH] * pa[:, 2 * BR_WIDTH:]

    @pl.when(t == 0)
    def _():
        carry_ref[...] = jnp.zeros_like(carry_ref)

    prev = carry_ref[...]
    p1, p2 = prev[7:8], prev[6:7]
    row = _row_iota(ach.shape)
    s1 = jnp.where(row == 0, p1, pltpu.roll(ach, 1, axis=0))
    s2 = jnp.where(row == 0, p2, jnp.where(row == 1, p1, pltpu.roll(ach, 2, axis=0)))
    cw = cw_ref[...]
    yabm_ref[0, :, 0:BR_WIDTH] = (pa[:, :BR_WIDTH] * (cw[0:1] * s2 + cw[1:2] * s1 + cw[2:3] * ach)).astype(BF16)
    carry_ref[...] = ach[tm - 8:tm]
    ca_ref[0] = ach[tm - 2:tm]

    pg = proj(3 * BR_WIDTH, 5 * BR_WIDTH)
    vn = _layer_norm(pg[:, BR_WIDTH:], lng_ref[...], lnb_ref[...])
    tril = _row_iota((CHUNK, CHUNK)) >= _lane_iota((CHUNK, CHUNK))
    wsm = [jnp.where(tril, ws_ref[g], 0.0).astype(BF16) for g in range(N_GROUPS)]
    gb = gb_ref[...]
    for c in range(tm // CHUNK):
        vc = vn[c * CHUNK:(c + 1) * CHUNK].astype(BF16)
        mixed = _head_select([_dot(wsm[g], vc) for g in range(N_GROUPS)])
        yabm_ref[0, c * CHUNK:(c + 1) * CHUNK, BR_WIDTH:2 * BR_WIDTH] = (
            pg[c * CHUNK:(c + 1) * CHUNK, :BR_WIDTH] * (mixed + gb)).astype(BF16)

    pq = proj(5 * BR_WIDTH, 10 * BR_WIDTH)
    qsb_ref[0] = (pq[:, :BR_WIDTH] * (HEAD_DIM ** -0.5 * LOG2E)).astype(BF16)
    kvt = _dot_nt(wsbt_ref[...], hb)
    ksb_ref[0] = kvt[:BR_WIDTH]
    vsb_ref[0] = kvt[BR_WIDTH:]
    ksb16_ref[0] = pq[:, BR_WIDTH:2 * BR_WIDTH].astype(BF16)
    vsb16_ref[0] = kvt[BR_WIDTH:].astype(BF16)

    bd32 = bd32_ref[...]
    qd = _group_rms(pq[:, 2 * BR_WIDTH:3 * BR_WIDTH], dqg_ref[...], bd32, DIFF_SUB)
    qdf_ref[0] = (qd * (DIFF_SUB ** -0.5 * LOG2E)).astype(BF16)
    kvt = _dot_nt(wdft_ref[...], hb)
    kd = _group_rms_t(kvt[:BR_WIDTH], dkg_ref[...], bd32, DIFF_SUB)
    kdf_ref[0] = kd
    vdf_ref[0] = kvt[BR_WIDTH:]
    kdf16_ref[0] = _group_rms(pq[:, 3 * BR_WIDTH:4 * BR_WIDTH], dkgr_ref[...], bd32, DIFF_SUB).astype(BF16)
    vdf16_ref[0] = kvt[BR_WIDTH:].astype(BF16)

    qm = _group_rms(pq[:, 4 * BR_WIDTH:], mqg_ref[...], bd64_ref[...], HEAD_DIM)
    qm = (qm * (HEAD_DIM ** -0.5 * LOG2E)).astype(BF16)
    yabm_ref[0, :, 2 * BR_WIDTH:] = _mem_attention(qm, mk_ref[0], mv_ref[0]).astype(BF16)


def _inproj(x, p, mk16, mv16, tm):
    n, t, d = x.shape
    consts = [p["norm1_g"], p["w_in_tok"], p["w_sb_kv_t"], p["w_df_kv_t"], p["conv_a_w"], p["gmlp_ln_g"], p["gmlp_ln_b"],
              p["gmlp_ws"], p["gmlp_b_full"], p["diff_qn_g"], p["diff_kn_g_col"], p["diff_kn_g"], p["mem_qn_g"],
              p["bd32"], p["bd64"]]
    seq = lambda w: pl.BlockSpec((1, tm, w), lambda i, j: (i, j, 0))
    seq_t = pl.BlockSpec((1, BR_WIDTH, tm), lambda i, j: (i, 0, j))
    mem_spec = pl.BlockSpec((1,) + mk16.shape[1:], lambda i, j: (i, 0, 0))
    bf = lambda w: jax.ShapeDtypeStruct((n, t, w), BF16)
    bf_t = jax.ShapeDtypeStruct((n, BR_WIDTH, t), BF16)
    fl_t = jax.ShapeDtypeStruct((n, BR_WIDTH, t), F32)
    return pl.pallas_call(
        functools.partial(_inproj_kernel, tm=tm),
        grid=(n, t // tm),
        in_specs=[seq(d)] + [_full_spec(c.shape) for c in consts] + [mem_spec, mem_spec],
        out_specs=[seq(3 * BR_WIDTH), seq(BR_WIDTH), seq(BR_WIDTH), seq_t, seq(BR_WIDTH), seq(BR_WIDTH)] + [seq_t] * 5
        + [pl.BlockSpec((1, 2, BR_WIDTH), lambda i, j: (i, 0, 0))],
        out_shape=[bf(3 * BR_WIDTH), bf(BR_WIDTH), bf(BR_WIDTH), bf_t, bf(BR_WIDTH), bf(BR_WIDTH), bf_t] + [fl_t] * 4
        + [jax.ShapeDtypeStruct((n, 2, BR_WIDTH), F32)],
        scratch_shapes=[pltpu.VMEM((8, BR_WIDTH), F32)],
        compiler_params=_params("arbitrary", "arbitrary"),
        name="in_proj",
    )(x, *consts, mk16, mv16)


def _sb_kernel(q_ref, k_ref, v_ref, uu_ref, eye_ref, o_ref, qm_ref, carry_ref, acc_ref, *, tb):
    qi = pl.program_id(1)
    q = q_ref[0]
    lane = _lane_iota(q.shape)
    for h in range(N_GROUPS):
        in_head = (lane >= HEAD_DIM * h) & (lane < HEAD_DIM * (h + 1))
        qm_ref[h * tb:(h + 1) * tb] = jnp.where(in_head, q, jnp.zeros_like(q))
    strictly_before = _row_iota((tb, tb)) < _lane_iota((tb, tb))
    heads = range(N_GROUPS)

    def block(kb, first):
        start = pl.multiple_of(kb * tb, tb)
        z_all = _dot_nt(k_ref[0, pl.ds(start, tb), :], qm_ref[...])
        lbs, l1ms = [], []
        for h in heads:
            lb, l1m = _log2_sigmoid_pair(z_all[:, h * tb:(h + 1) * tb])
            lbs.append(lb)
            l1ms.append(jnp.where(strictly_before, l1m, 0.0) if first else l1m)
        hi, lo = _split_bf16(jnp.concatenate(l1ms, axis=1))
        between = _dot(uu_ref[...], jnp.concatenate([hi, lo], axis=0))
        for h in heads:
            total = lbs[h] + between[:, h * tb:(h + 1) * tb]
            if not first:
                total = total + carry_ref[h:h + 1]
            a = jnp.exp2(total)
            if first:
                a = jnp.where(strictly_before, a, 0.0)
            pv = _dot(v_ref[0, HEAD_DIM * h:HEAD_DIM * (h + 1), pl.ds(start, tb)], a.astype(BF16))
            block_sum = jnp.sum(l1ms[h], axis=0, keepdims=True)
            if first:
                acc_ref[h] = pv
                carry_ref[h:h + 1] = block_sum
            else:
                acc_ref[h] += pv
                carry_ref[h:h + 1] += block_sum

    block(qi, True)

    def cond(state):
        i, live = state
        return (i < qi) & (live > SB_UNDERFLOW_LOG2)

    def body(state):
        i, _ = state
        block(qi - 1 - i, False)
        return i + 1, jnp.max(carry_ref[...])

    lax.while_loop(cond, body, (jnp.int32(0), jnp.max(carry_ref[...])))
    yt = jnp.concatenate([acc_ref[h] for h in heads], axis=0)
    o_ref[0] = _dot_nt(eye_ref[...], yt.astype(BF16)).astype(BF16)


def _sb_attention(q, k, v, tb):
    n, t, w = q.shape
    i = np.arange(tb)
    later = (i[None, :] > i[:, None])
    uu = jnp.asarray(np.concatenate([later, later], axis=1), dtype=BF16)
    eye = jnp.asarray(np.eye(tb), dtype=BF16)
    return pl.pallas_call(
        functools.partial(_sb_kernel, tb=tb),
        grid=(n, t // tb),
        in_specs=[pl.BlockSpec((1, tb, w), lambda i, j: (i, j, 0)),
                  pl.BlockSpec((1, t, w), lambda i, j: (i, 0, 0)),
                  pl.BlockSpec((1, w, t), lambda i, j: (i, 0, 0)),
                  _full_spec(uu.shape), _full_spec(eye.shape)],
        out_specs=pl.BlockSpec((1, tb, w), lambda i, j: (i, j, 0)),
        out_shape=jax.ShapeDtypeStruct((n, t, w), BF16),
        scratch_shapes=[pltpu.VMEM((N_GROUPS * tb, w), BF16), pltpu.VMEM((N_GROUPS, tb), F32),
                        pltpu.VMEM((N_GROUPS, HEAD_DIM, tb), F32)],
        compiler_params=_params("arbitrary", "arbitrary"),
        name="sb_attn",
    )(q, k, v, uu, eye)


def _diff_kernel(q_ref, k_ref, v_ref, lp_ref, og_ref, bd64_ref, eye_ref, o_ref,
                 qm_ref, bias_ref, m_ref, l_ref, acc_ref, *, tb, lam_init):
    qi = pl.program_id(1)
    slopes = _alibi_slopes_log2()
    rel = _row_iota((tb, tb)) - _lane_iota((tb, tb))

    @pl.when(qi == 0)
    def _():
        for h in range(N_GROUPS):
            bias_ref[h] = slopes[h] * rel.astype(F32)

    q = q_ref[0]
    lane = _lane_iota(q.shape)
    for c in range(2):
        for h in range(N_GROUPS):
            lo = HEAD_DIM * h + DIFF_SUB * c
            ch = c * N_GROUPS + h
            qm_ref[ch * tb:(ch + 1) * tb] = jnp.where((lane >= lo) & (lane < lo + DIFF_SUB), q, jnp.zeros_like(q))

    def block(kb, first):
        start = pl.multiple_of(kb * tb, tb)
        kblk = k_ref[0, pl.ds(start, tb), :]
        off = ((kb - qi) * tb).astype(F32)
        s_all = _dot_nt(kblk, qm_ref[...])
        probs, alphas = [], []
        for c in range(2):
            for h in range(N_GROUPS):
                ch = c * N_GROUPS + h
                s = s_all[:, ch * tb:(ch + 1) * tb] + bias_ref[h]
                shift = slopes[h] * off
                if first:
                    s = jnp.where(rel <= 0, s, NEG_BIG)
                    m_new = jnp.max(s, axis=0, keepdims=True) + shift
                    p = jnp.exp2(s - (m_new - shift))
                    l_ref[ch:ch + 1] = jnp.sum(p, axis=0, keepdims=True)
                    alphas.append(None)
                else:
                    m_old = m_ref[ch:ch + 1]
                    m_new = jnp.maximum(m_old, jnp.max(s, axis=0, keepdims=True) + shift)
                    alpha = jnp.exp2(m_old - m_new)
                    p = jnp.exp2(s - (m_new - shift))
                    l_ref[ch:ch + 1] = alpha * l_ref[ch:ch + 1] + jnp.sum(p, axis=0, keepdims=True)
                    alphas.append(alpha)
                m_ref[ch:ch + 1] = m_new
                probs.append(p.astype(BF16))
        for h in range(N_GROUPS):
            vh = v_ref[0, HEAD_DIM * h:HEAD_DIM * (h + 1), pl.ds(start, tb)]
            pv = _dot(vh, jnp.concatenate([probs[h], probs[N_GROUPS + h]], axis=1))
            for c in range(2):
                ch = c * N_GROUPS + h
                new = pv[:, c * tb:(c + 1) * tb]
                acc_ref[ch] = new if first else alphas[ch] * acc_ref[ch] + new

    block(qi, True)

    def body(i, _):
        block(qi - 1 - i, False)
        return 0

    lax.fori_loop(0, qi, body, 0)
    lam = _lam_value(lp_ref[...], lam_init)
    ys = []
    for h in range(N_GROUPS):
        y0 = acc_ref[h] / l_ref[h:h + 1]
        y1 = acc_ref[N_GROUPS + h] / l_ref[N_GROUPS + h:N_GROUPS + h + 1]
        ys.append(y0 - lam * y1)
    yt = _group_rms_t(jnp.concatenate(ys, axis=0), og_ref[...], bd64_ref[...], HEAD_DIM) * (1.0 - lam_init)
    o_ref[0] = _dot_nt(eye_ref[...], yt.astype(BF16)).astype(BF16)


def _diff_attention(q, k, v, lp, og_col, bd64, lam_init, tb):
    n, t, w = q.shape
    maps = 2 * N_GROUPS
    eye = jnp.asarray(np.eye(tb), dtype=BF16)
    return pl.pallas_call(
        functools.partial(_diff_kernel, tb=tb, lam_init=lam_init),
        grid=(n, t // tb),
        in_specs=[pl.BlockSpec((1, tb, w), lambda i, j: (i, j, 0)),
                  pl.BlockSpec((1, t, w), lambda i, j: (i, 0, 0)),
                  pl.BlockSpec((1, w, t), lambda i, j: (i, 0, 0)),
                  _full_spec(lp.shape), _full_spec(og_col.shape), _full_spec(bd64.shape), _full_spec(eye.shape)],
        out_specs=pl.BlockSpec((1, tb, w), lambda i, j: (i, j, 0)),
        out_shape=jax.ShapeDtypeStruct((n, t, w), BF16),
        scratch_shapes=[pltpu.VMEM((maps * tb, w), BF16), pltpu.VMEM((N_GROUPS, tb, tb), F32),
                        pltpu.VMEM((maps, tb), F32), pltpu.VMEM((maps, tb), F32),
                        pltpu.VMEM((maps, HEAD_DIM, tb), F32)],
        compiler_params=_params("arbitrary", "arbitrary"),
        name="diff_attn",
    )(q, k, v, lp, og_col, bd64, eye)


def _merge_kernel(x_ref, yabm_ref, yc_ref, yd_ref, g1_ref, wg_ref, bg_ref, wb_ref, wo_ref, o_ref):
    x = x_ref[0]
    hb = _rms(x, g1_ref[...]).astype(BF16)
    ys = [yabm_ref[0, :, 0:BR_WIDTH], yabm_ref[0, :, BR_WIDTH:2 * BR_WIDTH], yc_ref[0], yd_ref[0],
          yabm_ref[0, :, 2 * BR_WIDTH:]]
    acc = None
    for b in range(5):
        gate = jax.nn.sigmoid(_dot(hb, wg_ref[:, b * D_MODEL:(b + 1) * D_MODEL]) + bg_ref[:, b * D_MODEL:(b + 1) * D_MODEL])
        term = gate * _dot(ys[b], wb_ref[b])
        acc = term if acc is None else acc + term
    o_ref[0] = x + _dot(acc.astype(BF16), wo_ref[...])


def _merge(x, yabm, yc, yd, p, tm):
    n, t, d = x.shape
    consts = [p["norm1_g"], p["w_gate"], p["b_gate"], p["w_branch"], p["w_o"]]
    seq = lambda w: pl.BlockSpec((1, tm, w), lambda i, j: (i, j, 0))
    return pl.pallas_call(
        _merge_kernel,
        grid=(n, t // tm),
        in_specs=[seq(d), seq(3 * BR_WIDTH), seq(BR_WIDTH), seq(BR_WIDTH)] + [_full_spec(c.shape) for c in consts],
        out_specs=seq(d),
        out_shape=jax.ShapeDtypeStruct((n, t, d), F32),
        compiler_params=_params("arbitrary", "arbitrary"),
        name="merge",
    )(x, yabm, yc, yd, *consts)


FFN_COLS = 1408


def _ffn_kernel(x_ref, g2_ref, wup_ref, cw_ref, cb_ref, wdn_ref, o_ref, st_ref, carry_ref, *, tm):
    t = pl.program_id(1)
    x = x_ref[0]
    hb = _rms(x, g2_ref[...]).astype(BF16)

    @pl.when(t == 0)
    def _():
        carry_ref[...] = jnp.zeros_like(carry_ref)

    row = _row_iota((tm, FFN_COLS))

    def conv(lo):
        u = _dot(hb, wup_ref[:, lo:lo + FFN_COLS])
        prev = carry_ref[:, lo:lo + FFN_COLS]
        p1, p2 = prev[7:8], prev[6:7]
        s1 = jnp.where(row == 0, p1, pltpu.roll(u, 1, axis=0))
        s2 = jnp.where(row == 0, p2, jnp.where(row == 1, p1, pltpu.roll(u, 2, axis=0)))
        carry_ref[:, lo:lo + FFN_COLS] = u[tm - 8:tm]
        st_ref[0, :, lo:lo + FFN_COLS] = u[tm - 2:tm]
        return (cw_ref[0:1, lo:lo + FFN_COLS] * s2 + cw_ref[1:2, lo:lo + FFN_COLS] * s1
                + cw_ref[2:3, lo:lo + FFN_COLS] * u + cb_ref[:, lo:lo + FFN_COLS])

    acc = x
    for j in range(D_FF // FFN_COLS):
        gate = conv(j * FFN_COLS)
        val = conv(D_FF + j * FFN_COLS)
        act = (gate * jax.nn.sigmoid(gate) * val).astype(BF16)
        acc = acc + _dot(act, wdn_ref[j * FFN_COLS:(j + 1) * FFN_COLS, :])
    o_ref[0] = acc


def _ffn(x, p, tm):
    n, t, d = x.shape
    consts = [p["norm2_g"], p["w_up"], p["conv_ffn_w"], p["conv_ffn_b"], p["w_down"]]
    seq = pl.BlockSpec((1, tm, d), lambda i, j: (i, j, 0))
    return pl.pallas_call(
        functools.partial(_ffn_kernel, tm=tm),
        grid=(n, t // tm),
        in_specs=[seq] + [_full_spec(c.shape) for c in consts],
        out_specs=[seq, pl.BlockSpec((1, 2, 2 * D_FF), lambda i, j: (i, 0, 0))],
        out_shape=[jax.ShapeDtypeStruct((n, t, d), F32), jax.ShapeDtypeStruct((n, 2, 2 * D_FF), F32)],
        scratch_shapes=[pltpu.VMEM((8, 2 * D_FF), F32)],
        compiler_params=_params("arbitrary", "arbitrary"),
        name="conv_ffn",
    )(x, *consts)


def _ffn_step_kernel(x_ref, p0_ref, p1_ref, g2_ref, wup_ref, cw_ref, cb_ref, wdn_ref, o_ref, u_ref):
    x = x_ref[...]
    hb = _rms(x, g2_ref[...]).astype(BF16)

    def conv(lo):
        u = _dot(hb, wup_ref[:, lo:lo + FFN_COLS])
        u_ref[:, lo:lo + FFN_COLS] = u
        return (cw_ref[0:1, lo:lo + FFN_COLS] * p0_ref[:, lo:lo + FFN_COLS]
                + cw_ref[1:2, lo:lo + FFN_COLS] * p1_ref[:, lo:lo + FFN_COLS]
                + cw_ref[2:3, lo:lo + FFN_COLS] * u + cb_ref[:, lo:lo + FFN_COLS])

    acc = x
    for j in range(D_FF // FFN_COLS):
        gate = conv(j * FFN_COLS)
        val = conv(D_FF + j * FFN_COLS)
        act = (gate * jax.nn.sigmoid(gate) * val).astype(BF16)
        acc = acc + _dot(act, wdn_ref[j * FFN_COLS:(j + 1) * FFN_COLS, :])
    o_ref[...] = acc


def _ffn_step(x, prev0, prev1, p):
    m, d = x.shape
    consts = [p["norm2_g"], p["w_up"], p["conv_ffn_w"], p["conv_ffn_b"], p["w_down"]]
    args = [x, prev0, prev1] + consts
    return pl.pallas_call(
        _ffn_step_kernel,
        grid=(1,),
        in_specs=[_full_spec(a.shape) for a in args],
        out_specs=[_full_spec((m, d)), _full_spec((m, 2 * D_FF))],
        out_shape=[jax.ShapeDtypeStruct((m, d), F32), jax.ShapeDtypeStruct((m, 2 * D_FF), F32)],
        compiler_params=_params("arbitrary"),
        name="conv_ffn_step",
    )(*args)


def _inproj_step_kernel(x_ref, c0_ref, c1_ref, g1_ref, win_ref, cw_ref, lng_ref, lnb_ref, ws_ref, gb_ref,
                        dqg_ref, dkg_ref, mqg_ref, bd32_ref, bd64_ref,
                        yab_ref, q_ref, ksb_ref, vsb_ref, kdf_ref, vdf_ref, vn_ref, ach_ref):
    hb = _rms(x_ref[...], g1_ref[...]).astype(BF16)

    def proj(lo, hi):
        return _dot(hb, win_ref[:, lo:hi])

    pa = proj(0, 3 * BR_WIDTH)
    ach = pa[:, BR_WIDTH:2 * BR_WIDTH] * pa[:, 2 * BR_WIDTH:]
    cw = cw_ref[...]
    yab_ref[:, 0:BR_WIDTH] = (pa[:, :BR_WIDTH] * (cw[0:1] * c0_ref[...] + cw[1:2] * c1_ref[...] + cw[2:3] * ach)).astype(BF16)
    ach_ref[...] = ach

    pg = proj(3 * BR_WIDTH, 5 * BR_WIDTH)
    vn = _layer_norm(pg[:, BR_WIDTH:], lng_ref[...], lnb_ref[...])
    vn_ref[...] = vn
    lane = _lane_iota((1, BR_WIDTH))
    w00 = jnp.zeros((1, BR_WIDTH), F32)
    for g in range(N_GROUPS):
        w00 = jnp.where((lane >= HEAD_DIM * g) & (lane < HEAD_DIM * (g + 1)), ws_ref[g, 0:1, 0:1], w00)
    yab_ref[:, BR_WIDTH:] = (pg[:, :BR_WIDTH] * (w00 * vn + gb_ref[0:1, :])).astype(BF16)

    ps = proj(5 * BR_WIDTH, 8 * BR_WIDTH)
    q_ref[:, 0:BR_WIDTH] = (ps[:, :BR_WIDTH] * (HEAD_DIM ** -0.5)).astype(BF16)
    ksb_ref[...] = ps[:, BR_WIDTH:2 * BR_WIDTH]
    vsb_ref[...] = ps[:, 2 * BR_WIDTH:]

    pd = proj(8 * BR_WIDTH, 11 * BR_WIDTH)
    bd32 = bd32_ref[...]
    qd = _group_rms(pd[:, :BR_WIDTH], dqg_ref[...], bd32, DIFF_SUB)
    q_ref[:, BR_WIDTH:2 * BR_WIDTH] = (qd * (DIFF_SUB ** -0.5 * LOG2E)).astype(BF16)
    kdf_ref[...] = _group_rms(pd[:, BR_WIDTH:2 * BR_WIDTH], dkg_ref[...], bd32, DIFF_SUB)
    vdf_ref[...] = pd[:, 2 * BR_WIDTH:]

    qm = _group_rms(proj(11 * BR_WIDTH, 12 * BR_WIDTH), mqg_ref[...], bd64_ref[...], HEAD_DIM)
    q_ref[:, 2 * BR_WIDTH:] = (qm * (HEAD_DIM ** -0.5 * LOG2E)).astype(BF16)


def _inproj_step(x, c0, c1, p):
    m, d = x.shape
    consts = [p["norm1_g"], p["w_in"], p["conv_a_w"], p["gmlp_ln_g"], p["gmlp_ln_b"], p["gmlp_ws"],
              p["gmlp_b_full"], p["diff_qn_g"], p["diff_kn_g"], p["mem_qn_g"], p["bd32"], p["bd64"]]
    args = [x, c0, c1] + consts
    fl = jax.ShapeDtypeStruct((m, BR_WIDTH), F32)
    out_shape = [jax.ShapeDtypeStruct((m, 2 * BR_WIDTH), BF16), jax.ShapeDtypeStruct((m, 3 * BR_WIDTH), BF16)] + [fl] * 6
    return pl.pallas_call(
        _inproj_step_kernel,
        grid=(1,),
        in_specs=[_full_spec(a.shape) for a in args],
        out_specs=[_full_spec(s.shape) for s in out_shape],
        out_shape=out_shape,
        compiler_params=_params("arbitrary"),
        name="in_proj_step",
    )(*args)


def _decode_kernel(pt_ref, q_ref, kn_ref, vn_ref, mk_ref, mv_ref, lp_ref, og_ref, bd64_ref, ll_ref, ones_ref,
                   ksb_hbm, vsb_hbm, kdf_hbm, vdf_hbm, o_ref, buf_ref, sem_ref, *, layer, n_pages, page, lam_init):
    step = pl.program_id(0)
    pools = (ksb_hbm, vsb_hbm, kdf_hbm, vdf_hbm)

    def page_copy(sample, slot, c, j):
        return pltpu.make_async_copy(pools[c].at[layer, pt_ref[sample, j]], buf_ref.at[slot, c, j], sem_ref.at[slot])

    def for_each_page(sample, slot, fn):
        for c in range(len(pools)):
            for j in range(n_pages):
                fn(page_copy(sample, slot, c, j))

    @pl.when(step == 0)
    def _():
        for_each_page(0, 0, lambda cp: cp.start())

    @pl.when(step + 1 < pl.num_programs(0))
    def _():
        for_each_page(step + 1, (step + 1) & 1, lambda cp: cp.start())

    slot = step & 1
    for_each_page(step, slot, lambda cp: cp.wait())
    ksb, vsb, kdf, vdf = ([buf_ref.at[slot, c, j] for j in range(n_pages)] for c in range(len(pools)))
    rows = 2 * N_GROUPS
    n_sub = HEAD_DIM // DIFF_SUB
    ones = ones_ref[...]

    def q_column(col):
        qrow = jnp.broadcast_to(q_ref[0, :, col * BR_WIDTH:(col + 1) * BR_WIDTH].astype(F32), (BR_WIDTH, BR_WIDTH))
        diag = jnp.where(_row_iota(qrow.shape) == _lane_iota(qrow.shape), qrow, 0.0).astype(BF16)
        return _dot(diag, ones)

    def group_scores(kt, qcol):
        return jnp.sum((kt * qcol).reshape(rows, DIFF_SUB, kt.shape[1]), axis=1)

    def pair_sum(z):
        r = z.shape[0]
        even = (_row_iota(z.shape) & 1) == 0
        return z + jnp.where(even, pltpu.roll(z, r - 1, axis=0), pltpu.roll(z, 1, axis=0))

    def head_rows(w):
        return jnp.concatenate([jnp.broadcast_to(w[n_sub * h:n_sub * h + 1], (HEAD_DIM, w.shape[1]))
                                for h in range(N_GROUPS)], axis=0)

    def reduce_positions(acc):
        hi, lo = _split_bf16(acc)
        ones_row = jnp.ones((rows, acc.shape[1]), BF16)
        return _dot_nt(ones_row, hi) + _dot_nt(ones_row, lo)

    qc = q_column(0)
    z = jnp.concatenate([pair_sum(group_scores(ksb[pg][...], qc)) for pg in range(n_pages)], axis=0)
    lb, l1m = _log_sigmoid_pair(z)
    hi, lo = _split_bf16(l1m)
    hl = jnp.concatenate([hi, lo], axis=1)
    within = _dot(hl, ll_ref[...])
    total = _dot(hl, ones)
    carry = jnp.zeros((rows, page), F32)
    carries = [None] * n_pages
    for pg in reversed(range(n_pages)):
        carries[pg] = carry
        carry = carry + total[rows * pg:rows * (pg + 1)]
    a = jnp.exp(lb + within + jnp.concatenate(carries, axis=0))
    acc = jnp.zeros((BR_WIDTH, page), F32)
    for pg in range(n_pages):
        acc = acc + head_rows(a[rows * pg:rows * (pg + 1)]) * vsb[pg][...]
    y_c = reduce_positions(acc)[0:1]

    qc = q_column(1)
    s = jnp.concatenate([group_scores(kdf[pg][...], qc) for pg in range(n_pages)], axis=0)
    slopes = _alibi_slopes_log2()
    all_rows = _row_iota((rows * n_pages, page))
    head = _div_pow2(_rem_pow2(all_rows, rows), n_sub)
    sl = jnp.full(all_rows.shape, slopes[0], F32)
    for h in range(1, N_GROUPS):
        sl = jnp.where(head == h, slopes[h], sl)
    past = n_pages * page
    k_pos = _div_pow2(all_rows, rows) * page + _lane_iota(all_rows.shape)
    s = s + sl * (k_pos - past).astype(F32)
    row8 = _row_iota((rows, BR_WIDTH))
    group_lanes = _div_pow2(_lane_iota((rows, BR_WIDTH)), DIFF_SUB) == row8
    q8 = jnp.broadcast_to(q_ref[0, :, BR_WIDTH:2 * BR_WIDTH].astype(F32), (rows, BR_WIDTH))
    kn8 = jnp.broadcast_to(kn_ref[0], (rows, BR_WIDTH))
    s_new = jnp.sum(jnp.where(group_lanes, q8 * kn8, 0.0), axis=1, keepdims=True)
    m8 = s[0:rows]
    for pg in range(1, n_pages):
        m8 = jnp.maximum(m8, s[rows * pg:rows * (pg + 1)])
    m = jnp.maximum(jnp.max(m8, axis=1, keepdims=True), s_new)
    p_new = jnp.exp2(s_new - m)
    m_all = jnp.concatenate([jnp.broadcast_to(m, (rows, page))] * n_pages, axis=0)
    p = jnp.exp2(s - m_all)
    l8 = p[0:rows]
    for pg in range(1, n_pages):
        l8 = l8 + p[rows * pg:rows * (pg + 1)]
    l = jnp.sum(l8, axis=1, keepdims=True) + p_new
    lam = _lam_value(lp_ref[...], lam_init)
    first_map = (_row_iota((rows, 1)) & 1) == 0
    coef = jnp.where(first_map, 1.0, -lam) / l
    w = pair_sum(p * jnp.concatenate([jnp.broadcast_to(coef, (rows, page))] * n_pages, axis=0))
    acc = jnp.zeros((BR_WIDTH, page), F32)
    for pg in range(n_pages):
        acc = acc + head_rows(w[rows * pg:rows * (pg + 1)]) * vdf[pg][...]
    w_new = pair_sum(p_new * coef)
    head_lanes = (_div_pow2(_lane_iota((rows, BR_WIDTH)), HEAD_DIM) * n_sub) == row8
    vn8 = jnp.broadcast_to(vn_ref[0], (rows, BR_WIDTH))
    y_new = jnp.sum(jnp.where(head_lanes, w_new * vn8, 0.0), axis=0, keepdims=True)
    y_d = reduce_positions(acc)[0:1] + y_new
    y_d = _group_rms(jnp.broadcast_to(y_d, (rows, BR_WIDTH)), og_ref[...], bd64_ref[...], HEAD_DIM)[0:1]
    y_d = y_d * (1.0 - lam_init)

    qc = q_column(2)
    n_mem = mk_ref.shape[2]
    qc = jnp.concatenate([qc] * (n_mem // page), axis=1)
    s = pair_sum(group_scores(mk_ref[0], qc))
    p = jnp.exp2(s - jnp.max(s, axis=1, keepdims=True))
    p = p / jnp.sum(p, axis=1, keepdims=True)
    y_m = reduce_positions(head_rows(p) * mv_ref[0])[0:1]

    o_ref[0, :, 0:BR_WIDTH] = y_c.astype(BF16)
    o_ref[0, :, BR_WIDTH:2 * BR_WIDTH] = y_d.astype(BF16)
    o_ref[0, :, 2 * BR_WIDTH:] = y_m.astype(BF16)


def _decode_attention(layer, page_table, q, k_new, v_new, caches, mem_k, mem_v, p, lam_init):
    b = q.shape[0]
    n_pages = page_table.shape[1]
    page = caches[0].shape[3]
    ll = _suffix_ones(page)
    row_spec = lambda w: pl.BlockSpec((1, 1, w), lambda i, pt: (i, 0, 0))
    mem_spec = pl.BlockSpec((None, 1) + mem_k.shape[2:], lambda i, pt: (layer, i, 0, 0))
    const = lambda a: pl.BlockSpec(a.shape, lambda i, pt: (0,) * a.ndim)
    consts = [p["diff_lambda"], p["diff_out_g"], p["bd64"], ll, jnp.ones((BR_WIDTH, page), BF16)]
    grid_spec = pltpu.PrefetchScalarGridSpec(
        num_scalar_prefetch=1,
        grid=(b,),
        in_specs=[row_spec(3 * BR_WIDTH), row_spec(BR_WIDTH), row_spec(BR_WIDTH), mem_spec, mem_spec]
        + [const(c) for c in consts] + [pl.BlockSpec(memory_space=pl.ANY)] * len(caches),
        out_specs=row_spec(3 * BR_WIDTH),
        scratch_shapes=[pltpu.VMEM((2, len(caches), n_pages, BR_WIDTH, page), F32), pltpu.SemaphoreType.DMA((2,))],
    )
    return pl.pallas_call(
        functools.partial(_decode_kernel, layer=layer, n_pages=n_pages, page=page, lam_init=lam_init),
        grid_spec=grid_spec,
        out_shape=jax.ShapeDtypeStruct((b, 1, 3 * BR_WIDTH), BF16),
        compiler_params=_params("arbitrary"),
        name="decode_attn",
    )(page_table, q, k_new, v_new, mem_k, mem_v, *consts, *caches)


PROMPT_TILE = 512
ATTN_BLOCK = 256


def _layer_params(l, w):
    row = lambda a: a[l].reshape(1, -1).astype(F32)
    tile4 = lambda a: jnp.tile(a[l].astype(F32), N_GROUPS).reshape(1, -1)
    col4 = lambda a: jnp.tile(a[l].astype(F32), N_GROUPS).reshape(-1, 1)
    w_in = w["w_in"][l].astype(BF16)
    return dict(
        norm1_g=row(w["norm1_g"]), w_in=w_in, conv_a_w=w["conv_a_w"][l].astype(F32),
        w_sb_kv_t=w_in[:, 6 * BR_WIDTH:8 * BR_WIDTH].T, w_df_kv_t=w_in[:, 9 * BR_WIDTH:11 * BR_WIDTH].T,
        w_in_tok=jnp.concatenate([w_in[:, :7 * BR_WIDTH], w_in[:, 8 * BR_WIDTH:10 * BR_WIDTH],
                                  w_in[:, 11 * BR_WIDTH:]], axis=1),
        diff_kn_g_col=col4(w["diff_kn_g"]), mem_kn_g_col=col4(w["mem_kn_g"]), diff_out_g_col=col4(w["diff_out_g"]),
        w_mem_kv_t=w["w_mem_kv"][l].astype(BF16).T,
        gmlp_ln_g=row(w["gmlp_ln_g"]), gmlp_ln_b=row(w["gmlp_ln_b"]), gmlp_ws=w["gmlp_ws"][l].astype(F32),
        gmlp_b_full=jnp.repeat(w["gmlp_b"][l].astype(F32).T, HEAD_DIM, axis=1),
        diff_qn_g=tile4(w["diff_qn_g"]), diff_kn_g=tile4(w["diff_kn_g"]), diff_out_g=tile4(w["diff_out_g"]),
        diff_lambda=w["diff_lambda"][l].astype(F32),
        mem_norm_g=row(w["mem_norm_g"]), mem_qn_g=tile4(w["mem_qn_g"]),
        w_branch=w["w_branch"][l].astype(BF16), w_gate=w["w_gate"][l].astype(BF16), b_gate=row(w["b_gate"]),
        w_o=w["w_o"][l].astype(BF16), norm2_g=row(w["norm2_g"]), w_up=w["w_up"][l].astype(BF16),
        conv_ffn_w=w["conv_ffn_w"][l].astype(F32), conv_ffn_b=row(w["conv_ffn_b"]), w_down=w["w_down"][l].astype(BF16),
        bd32=_block_ones(DIFF_SUB), bd64=_block_ones(HEAD_DIM),
    )


def _prompt_layer(x, mem, p, lam_init):
    n, t, _ = x.shape
    tm = min(PROMPT_TILE, t)
    tb = min(ATTN_BLOCK, t)
    mk, mv, mk16, mv16 = _mem_kv(mem, p["mem_norm_g"], p["w_mem_kv_t"], p["mem_kn_g_col"], p["bd64"])
    (yabm, qsb, ksb16, vsb16, qdf, kdf16, vdf16, ksb, vsb, kdf, vdf, ca) = _inproj(x, p, mk16, mv16, tm)
    yc = _sb_attention(qsb, ksb16, vsb16, tb)
    yd = _diff_attention(qdf, kdf16, vdf16, p["diff_lambda"], p["diff_out_g_col"], p["bd64"], lam_init, tb)
    x1 = _merge(x, yabm, yc, yd, p, tm)
    x2, cf = _ffn(x1, p, tm)
    return x2, ca, ksb, vsb, kdf, vdf, mk, mv, cf


def _sample_layer(l, x, conv_a, conv_ffn, page_table, caches, mem_k, mem_v, p, lam_init):
    b = x.shape[0]
    yab, q, ksb, vsb, kdf, vdf, vn, ach = _inproj_step(x, conv_a[:, 0], conv_a[:, 1], p)
    row3 = lambda a: a.reshape(b, 1, -1)
    ycdm = _decode_attention(l, page_table, row3(q), row3(kdf), row3(vdf), caches, mem_k, mem_v, p, lam_init)
    yabm = jnp.concatenate([yab, ycdm[:, 0, 2 * BR_WIDTH:]], axis=1)
    x1 = _merge(x[None], yabm[None], ycdm[None, :, 0, 0:BR_WIDTH], ycdm[None, :, 0, BR_WIDTH:2 * BR_WIDTH], p, b)[0]
    x2, u = _ffn_step(x1, conv_ffn[:, 0], conv_ffn[:, 1], p)
    ca_new = jnp.stack([conv_a[:, 1], ach], axis=1)
    cf_new = jnp.stack([conv_ffn[:, 1], u], axis=1)
    return x2, ca_new, vn, ksb, vsb, kdf, vdf, cf_new


def kernel(x_prompt, x_sample, state_conv_a, cache_k_sb, cache_v_sb, cache_k_diff, cache_v_diff, cache_mem_k,
           cache_mem_v, state_conv_ffn, page_table, mem_prompt, norm1_g, w_in, conv_a_w, gmlp_ln_g, gmlp_ln_b,
           gmlp_ws, gmlp_b, diff_qn_g, diff_kn_g, diff_lambda, diff_out_g, mem_norm_g, w_mem_kv, mem_qn_g,
           mem_kn_g, w_branch, w_gate, b_gate, w_o, norm2_g, w_up, conv_ffn_w, conv_ffn_b, w_down):
    w = dict(norm1_g=norm1_g, w_in=w_in, conv_a_w=conv_a_w, gmlp_ln_g=gmlp_ln_g, gmlp_ln_b=gmlp_ln_b,
             gmlp_ws=gmlp_ws, gmlp_b=gmlp_b, diff_qn_g=diff_qn_g, diff_kn_g=diff_kn_g, diff_lambda=diff_lambda,
             diff_out_g=diff_out_g, mem_norm_g=mem_norm_g, w_mem_kv=w_mem_kv, mem_qn_g=mem_qn_g,
             mem_kn_g=mem_kn_g, w_branch=w_branch, w_gate=w_gate, b_gate=b_gate, w_o=w_o, norm2_g=norm2_g,
             w_up=w_up, conv_ffn_w=conv_ffn_w, conv_ffn_b=conv_ffn_b, w_down=w_down)
    depth = w_in.shape[0]
    n_p, t_p, _ = x_prompt.shape
    n_s = x_sample.shape[0]
    flat = lambda c: jnp.transpose(c, (0, 1, 3, 4, 2)).reshape(c.shape[:2] + (BR_WIDTH, c.shape[2]))
    caches = [flat(cache_k_sb), flat(cache_v_sb), flat(cache_k_diff), flat(cache_v_diff)]
    mem_k, mem_v = flat(cache_mem_k), flat(cache_mem_v)
    xp, xs = x_prompt, x_sample[:, 0]
    outs_p, outs_s = [], []
    for l in range(depth):
        p = _layer_params(l, w)
        lam_init = 0.8 - 0.6 * math.exp(-0.3 * l)
        xp, *rest_p = _prompt_layer(xp, mem_prompt, p, lam_init)
        outs_p.append(rest_p)
        xs, *rest_s = _sample_layer(l, xs, state_conv_a[l], state_conv_ffn[l], page_table, caches, mem_k, mem_v,
                                    p, lam_init)
        outs_s.append(rest_s)
    heads = lambda a: a.reshape(a.shape[:-1] + (N_GROUPS, HEAD_DIM))
    stack_p = lambda i: jnp.stack([o[i] for o in outs_p], axis=0)
    stack_s = lambda i: jnp.stack([o[i] for o in outs_s], axis=0)
    step = lambda a: a.reshape(depth, n_s, 1, -1)

    def heads_t(a):
        d0, n, _, t = a.shape
        return jnp.transpose(a.reshape(d0, n, N_GROUPS, HEAD_DIM, t), (0, 1, 4, 2, 3))

    return (xp, xs[:, None, :],
            stack_p(0), stack_s(0), step(stack_s(1)),
            heads_t(stack_p(1)), heads_t(stack_p(2)), heads(step(stack_s(2))), heads(step(stack_s(3))),
            heads_t(stack_p(3)), heads_t(stack_p(4)), heads(step(stack_s(4))), heads(step(stack_s(5))),
            heads_t(stack_p(5)), heads_t(stack_p(6)),
            stack_p(7), stack_s(6))
```

```python
import functools
import math
from typing import NamedTuple

import numpy as np
import jax
import jax.numpy as jnp
from jax import lax
from jax.experimental import pallas as pl
from jax.experimental.pallas import tpu as pltpu

D_MODEL = 1024
HEAD_DIM = 64
BR_WIDTH = 256
N_GROUPS = 4
DIFF_SUB = 32
CHUNK = 128
D_FF = 2816
EPS = 1e-6
NEG_BIG = -1e30
SB_UNDERFLOW_LOG2 = -150.0
LOG2E = 1.4426950408889634
VMEM_LIMIT_BYTES = 56 * 1024 * 1024

F32 = jnp.float32
BF16 = jnp.bfloat16


def _dot(a, b):
    return jnp.dot(a, b, preferred_element_type=F32)


def _dot_nt(a, b):
    return lax.dot_general(a, b, (((1,), (1,)), ((), ())), preferred_element_type=F32)


def _split_bf16(x):
    hi = x.astype(BF16)
    lo = (x - hi.astype(F32)).astype(BF16)
    return hi, lo


def _dot_split(x, w):
    hi, lo = _split_bf16(x)
    return _dot(hi, w) + _dot(lo, w)


def _rms(x, g):
    ms = jnp.mean(x * x, axis=-1, keepdims=True)
    return x * lax.rsqrt(ms + EPS) * g


def _group_rms(z, g, ones_bd, group):
    ms = _dot_split(z * z, ones_bd) * (1.0 / group)
    return z * lax.rsqrt(ms + EPS) * g


def _lane_iota(shape):
    return lax.broadcasted_iota(jnp.int32, shape, len(shape) - 1)


def _row_iota(shape):
    return lax.broadcasted_iota(jnp.int32, shape, len(shape) - 2)


def _div_pow2(x, d):
    assert d & (d - 1) == 0
    return lax.shift_right_logical(x, d.bit_length() - 1)


def _rem_pow2(x, d):
    assert d & (d - 1) == 0
    return x & (d - 1)


def _head_select(parts):
    lane = _lane_iota(parts[0].shape)
    out = parts[0]
    for h in range(1, N_GROUPS):
        out = jnp.where(lane >= HEAD_DIM * h, parts[h], out)
    return out


def _stack_heads(q, n_sub):
    lane = _lane_iota(q.shape)
    width = HEAD_DIM // n_sub
    zero = jnp.zeros_like(q)
    parts = []
    for c in range(n_sub):
        for h in range(N_GROUPS):
            lo = HEAD_DIM * h + width * c
            parts.append(jnp.where((lane >= lo) & (lane < lo + width), q, zero))
    return jnp.concatenate(parts, axis=0)


def _log_sigmoid_pair(z):
    lb = jnp.minimum(z, 0.0) - jnp.log(1.0 + jnp.exp(-jnp.abs(z)))
    return lb, lb - z


def _log2_sigmoid_pair(z2):
    lb = jnp.minimum(z2, 0.0) - jnp.log2(1.0 + jnp.exp2(-jnp.abs(z2)))
    return lb, lb - z2


def _lam_value(lp, lam_init):
    a = jnp.sum(lp[0:1] * lp[1:2], axis=1, keepdims=True)
    b = jnp.sum(lp[2:3] * lp[3:4], axis=1, keepdims=True)
    return jnp.exp(a) - jnp.exp(b) + lam_init


def _block_ones(group):
    i = np.arange(BR_WIDTH)
    return jnp.asarray((i[:, None] // group) == (i[None, :] // group), dtype=BF16)


def _suffix_ones(n):
    i = np.arange(n)
    l = (i[:, None] > i[None, :])
    return jnp.asarray(np.concatenate([l, l], axis=0), dtype=BF16)


def _alibi_slopes_log2():
    return [LOG2E * 2.0 ** (-8.0 * (h + 1) / N_GROUPS) for h in range(N_GROUPS)]


def _full_spec(shape):
    nd = len(shape)
    return pl.BlockSpec(shape, lambda *_: (0,) * nd, pipeline_mode=pl.Buffered(1))


class _Layered(NamedTuple):
    array: jax.Array
    layer: int

    @property
    def shape(self):
        return self.array.shape[1:]


def _const_spec(c):
    if isinstance(c, _Layered):
        rest = (0,) * len(c.shape)
        return pl.BlockSpec((None,) + c.shape, lambda *_: (c.layer,) + rest, pipeline_mode=pl.Buffered(1))
    return _full_spec(c.shape)


def _operand(c):
    return c.array if isinstance(c, _Layered) else c


def _params(*sem):
    return pltpu.CompilerParams(dimension_semantics=sem, vmem_limit_bytes=VMEM_LIMIT_BYTES)


def _group_rms_t(zt, g_col, ones_bd, group):
    hi, lo = _split_bf16(zt * zt)
    ms = (_dot(ones_bd, hi) + _dot(ones_bd, lo)) * (1.0 / group)
    return zt * lax.rsqrt(ms + EPS) * g_col


def _memkv_kernel(mem_ref, g_ref, wt_ref, kg_ref, bd64_ref, k_ref, v_ref, k16_ref, v16_ref):
    hb = _rms(mem_ref[0], g_ref[...]).astype(BF16)
    kvt = _dot_nt(wt_ref[...], hb)
    k = _group_rms_t(kvt[:BR_WIDTH], kg_ref[...], bd64_ref[...], HEAD_DIM)
    v = kvt[BR_WIDTH:]
    k_ref[0] = k
    v_ref[0] = v
    k16_ref[0] = k.astype(BF16)
    v16_ref[0] = v.astype(BF16)


def _mem_kv(mem, g, w, kg, bd64):
    n, m, d = mem.shape
    blk = pl.BlockSpec((1, BR_WIDTH, m), lambda i: (i, 0, 0))
    return pl.pallas_call(
        _memkv_kernel,
        grid=(n,),
        in_specs=[pl.BlockSpec((1, m, d), lambda i: (i, 0, 0)), _full_spec(g.shape), _full_spec(w.shape),
                  _full_spec(kg.shape), _full_spec(bd64.shape)],
        out_specs=[blk, blk, blk, blk],
        out_shape=[jax.ShapeDtypeStruct((n, BR_WIDTH, m), F32)] * 2
        + [jax.ShapeDtypeStruct((n, BR_WIDTH, m), BF16)] * 2,
        compiler_params=_params("arbitrary"),
        name="mem_kv",
    )(mem, g, w, kg, bd64)


def _layer_norm(x, g, b):
    mu = jnp.mean(x, axis=-1, keepdims=True)
    xc = x - mu
    return xc * lax.rsqrt(jnp.mean(xc * xc, axis=-1, keepdims=True) + EPS) * g + b


def _mem_attention(qm, mkt, mvt):
    r = qm.shape[0]
    s = _dot(_stack_heads(qm, 1), mkt)
    p = jnp.exp2(s - jnp.max(s, axis=-1, keepdims=True))
    l = jnp.sum(p, axis=-1, keepdims=True)
    o = _dot_nt(p.astype(BF16), mvt) / l
    return _head_select([o[h * r:(h + 1) * r] for h in range(N_GROUPS)])


def _inproj_kernel(x_ref, g1_ref, win_ref, wsbt_ref, wdft_ref, cw_ref, lng_ref, lnb_ref, ws_ref, gb_ref, dqg_ref,
                   dkg_ref, dkgr_ref, mqg_ref, bd32_ref, bd64_ref, mk_ref, mv_ref,
                   yabm_ref, qsb_ref, ksb16_ref, vsb16_ref, qdf_ref, kdf16_ref, vdf16_ref,
                   ksb_ref, vsb_ref, kdf_ref, vdf_ref, ca_ref, carry_ref, *, tm):
    t = pl.program_id(1)
    hb = _rms(x_ref[0], g1_ref[...]).astype(BF16)

    def proj(lo, hi):
        return _dot(hb, win_ref[:, lo:hi])

    pa = proj(0, 3 * BR_WIDTH)
    ach = pa[:, BR_WIDTH:2 * BR_WIDTH] * pa[:, 2 * BR_WIDTH:]

    @pl.when(t == 0)
    def _():
        carry_ref[...] = jnp.zeros_like(carry_ref)

    prev = carry_ref[...]
    p1, p2 = prev[7:8], prev[6:7]
    row = _row_iota(ach.shape)
    s1 = jnp.where(row == 0, p1, pltpu.roll(ach, 1, axis=0))
    s2 = jnp.where(row == 0, p2, jnp.where(row == 1, p1, pltpu.roll(ach, 2, axis=0)))
    cw = cw_ref[...]
    yabm_ref[0, :, 0:BR_WIDTH] = (pa[:, :BR_WIDTH] * (cw[0:1] * s2 + cw[1:2] * s1 + cw[2:3] * ach)).astype(BF16)
    carry_ref[...] = ach[tm - 8:tm]
    ca_ref[0] = ach[tm - 2:tm]

    pg = proj(3 * BR_WIDTH, 5 * BR_WIDTH)
    vn = _layer_norm(pg[:, BR_WIDTH:], lng_ref[...], lnb_ref[...])
    tril = _row_iota((CHUNK, CHUNK)) >= _lane_iota((CHUNK, CHUNK))
    wsm = [jnp.where(tril, ws_ref[g], 0.0).astype(BF16) for g in range(N_GROUPS)]
    gb = gb_ref[...]
    for c in range(tm // CHUNK):
        vc = vn[c * CHUNK:(c + 1) * CHUNK].astype(BF16)
        mixed = _head_select([_dot(wsm[g], vc) for g in range(N_GROUPS)])
        yabm_ref[0, c * CHUNK:(c + 1) * CHUNK, BR_WIDTH:2 * BR_WIDTH] = (
            pg[c * CHUNK:(c + 1) * CHUNK, :BR_WIDTH] * (mixed + gb)).astype(BF16)

    pq = proj(5 * BR_WIDTH, 10 * BR_WIDTH)
    qsb_ref[0] = (pq[:, :BR_WIDTH] * (HEAD_DIM ** -0.5 * LOG2E)).astype(BF16)
    kvt = _dot_nt(wsbt_ref[...], hb)
    ksb_ref[0] = kvt[:BR_WIDTH]
    vsb_ref[0] = kvt[BR_WIDTH:]
    ksb16_ref[0] = pq[:, BR_WIDTH:2 * BR_WIDTH].astype(BF16)
    vsb16_ref[0] = kvt[BR_WIDTH:].astype(BF16)

    bd32 = bd32_ref[...]
    qd = _group_rms(pq[:, 2 * BR_WIDTH:3 * BR_WIDTH], dqg_ref[...], bd32, DIFF_SUB)
    qdf_ref[0] = (qd * (DIFF_SUB ** -0.5 * LOG2E)).astype(BF16)
    kvt = _dot_nt(wdft_ref[...], hb)
    kd = _group_rms_t(kvt[:BR_WIDTH], dkg_ref[...], bd32, DIFF_SUB)
    kdf_ref[0] = kd
    vdf_ref[0] = kvt[BR_WIDTH:]
    kdf16_ref[0] = _group_rms(pq[:, 3 * BR_WIDTH:4 * BR_WIDTH], dkgr_ref[...], bd32, DIFF_SUB).astype(BF16)
    vdf16_ref[0] = kvt[BR_WIDTH:].astype(BF16)

    qm = _group_rms(pq[:, 4 * BR_WIDTH:], mqg_ref[...], bd64_ref[...], HEAD_DIM)
    qm = (qm * (HEAD_DIM ** -0.5 * LOG2E)).astype(BF16)
    yabm_ref[0, :, 2 * BR_WIDTH:] = _mem_attention(qm, mk_ref[0], mv_ref[0]).astype(BF16)


def _inproj(x, p, mk16, mv16, tm):
    n, t, d = x.shape
    consts = [p["norm1_g"], p["w_in_tok"], p["w_sb_kv_t"], p["w_df_kv_t"], p["conv_a_w"], p["gmlp_ln_g"], p["gmlp_ln_b"],
              p["gmlp_ws"], p["gmlp_b_full"], p["diff_qn_g"], p["diff_kn_g_col"], p["diff_kn_g"], p["mem_qn_g"],
              p["bd32"], p["bd64"]]
    seq = lambda w: pl.BlockSpec((1, tm, w), lambda i, j: (i, j, 0))
    seq_t = pl.BlockSpec((1, BR_WIDTH, tm), lambda i, j: (i, 0, j))
    mem_spec = pl.BlockSpec((1,) + mk16.shape[1:], lambda i, j: (i, 0, 0))
    bf = lambda w: jax.ShapeDtypeStruct((n, t, w), BF16)
    bf_t = jax.ShapeDtypeStruct((n, BR_WIDTH, t), BF16)
    fl_t = jax.ShapeDtypeStruct((n, BR_WIDTH, t), F32)
    return pl.pallas_call(
        functools.partial(_inproj_kernel, tm=tm),
        grid=(n, t // tm),
        in_specs=[seq(d)] + [_const_spec(c) for c in consts] + [mem_spec, mem_spec],
        out_specs=[seq(3 * BR_WIDTH), seq(BR_WIDTH), seq(BR_WIDTH), seq_t, seq(BR_WIDTH), seq(BR_WIDTH)] + [seq_t] * 5
        + [pl.BlockSpec((1, 2, BR_WIDTH), lambda i, j: (i, 0, 0))],
        out_shape=[bf(3 * BR_WIDTH), bf(BR_WIDTH), bf(BR_WIDTH), bf_t, bf(BR_WIDTH), bf(BR_WIDTH), bf_t] + [fl_t] * 4
        + [jax.ShapeDtypeStruct((n, 2, BR_WIDTH), F32)],
        scratch_shapes=[pltpu.VMEM((8, BR_WIDTH), F32)],
        compiler_params=_params("arbitrary", "arbitrary"),
        name="in_proj",
    )(x, *map(_operand, consts), mk16, mv16)


def _sb_kernel(q_ref, k_ref, v_ref, uu_ref, eye_ref, o_ref, qm_ref, carry_ref, acc_ref, *, tb):
    qi = pl.program_id(1)
    q = q_ref[0]
    lane = _lane_iota(q.shape)
    for h in range(N_GROUPS):
        in_head = (lane >= HEAD_DIM * h) & (lane < HEAD_DIM * (h + 1))
        qm_ref[h * tb:(h + 1) * tb] = jnp.where(in_head, q, jnp.zeros_like(q))
    strictly_before = _row_iota((tb, tb)) < _lane_iota((tb, tb))
    heads = range(N_GROUPS)

    def block(kb, first):
        start = pl.multiple_of(kb * tb, tb)
        z_all = _dot_nt(k_ref[0, pl.ds(start, tb), :], qm_ref[...])
        lbs, l1ms = [], []
        for h in heads:
            lb, l1m = _log2_sigmoid_pair(z_all[:, h * tb:(h + 1) * tb])
            lbs.append(lb)
            l1ms.append(jnp.where(strictly_before, l1m, 0.0) if first else l1m)
        hi, lo = _split_bf16(jnp.concatenate(l1ms, axis=1))
        between = _dot(uu_ref[...], jnp.concatenate([hi, lo], axis=0))
        for h in heads:
            total = lbs[h] + between[:, h * tb:(h + 1) * tb]
            if not first:
                total = total + carry_ref[h:h + 1]
            a = jnp.exp2(total)
            if first:
                a = jnp.where(strictly_before, a, 0.0)
            pv = _dot(v_ref[0, HEAD_DIM * h:HEAD_DIM * (h + 1), pl.ds(start, tb)], a.astype(BF16))
            block_sum = jnp.sum(l1ms[h], axis=0, keepdims=True)
            if first:
                acc_ref[h] = pv
                carry_ref[h:h + 1] = block_sum
            else:
                acc_ref[h] += pv
                carry_ref[h:h + 1] += block_sum

    block(qi, True)

    def cond(state):
        i, live = state
        return (i < qi) & (live > SB_UNDERFLOW_LOG2)

    def body(state):
        i, _ = state
        block(qi - 1 - i, False)
        return i + 1, jnp.max(carry_ref[...])

    lax.while_loop(cond, body, (jnp.int32(0), jnp.max(carry_ref[...])))
    yt = jnp.concatenate([acc_ref[h] for h in heads], axis=0)
    o_ref[0] = _dot_nt(eye_ref[...], yt.astype(BF16)).astype(BF16)


def _sb_attention(q, k, v, tb):
    n, t, w = q.shape
    i = np.arange(tb)
    later = (i[None, :] > i[:, None])
    uu = jnp.asarray(np.concatenate([later, later], axis=1), dtype=BF16)
    eye = jnp.asarray(np.eye(tb), dtype=BF16)
    return pl.pallas_call(
        functools.partial(_sb_kernel, tb=tb),
        grid=(n, t // tb),
        in_specs=[pl.BlockSpec((1, tb, w), lambda i, j: (i, j, 0)),
                  pl.BlockSpec((1, t, w), lambda i, j: (i, 0, 0)),
                  pl.BlockSpec((1, w, t), lambda i, j: (i, 0, 0)),
                  _full_spec(uu.shape), _full_spec(eye.shape)],
        out_specs=pl.BlockSpec((1, tb, w), lambda i, j: (i, j, 0)),
        out_shape=jax.ShapeDtypeStruct((n, t, w), BF16),
        scratch_shapes=[pltpu.VMEM((N_GROUPS * tb, w), BF16), pltpu.VMEM((N_GROUPS, tb), F32),
                        pltpu.VMEM((N_GROUPS, HEAD_DIM, tb), F32)],
        compiler_params=_params("arbitrary", "arbitrary"),
        name="sb_attn",
    )(q, k, v, uu, eye)


def _diff_kernel(q_ref, k_ref, v_ref, lp_ref, og_ref, bd64_ref, eye_ref, o_ref,
                 qm_ref, bias_ref, m_ref, l_ref, acc_ref, s_ref, *, tb, lam_init):
    qi = pl.program_id(1)
    slopes = _alibi_slopes_log2()
    rel = _row_iota((tb, tb)) - _lane_iota((tb, tb))

    @pl.when(qi == 0)
    def _():
        for h in range(N_GROUPS):
            bias_ref[h] = slopes[h] * rel.astype(F32)

    q = q_ref[0]
    lane = _lane_iota(q.shape)
    for c in range(2):
        for h in range(N_GROUPS):
            lo = HEAD_DIM * h + DIFF_SUB * c
            ch = c * N_GROUPS + h
            qm_ref[ch * tb:(ch + 1) * tb] = jnp.where((lane >= lo) & (lane < lo + DIFF_SUB), q, jnp.zeros_like(q))

    def scores(kb):
        start = pl.multiple_of(kb * tb, tb)
        return _dot_nt(k_ref[0, pl.ds(start, tb), :], qm_ref[...])

    def block(kb, score_tile, first):
        start = pl.multiple_of(kb * tb, tb)
        off = ((kb - qi) * tb).astype(F32)
        probs, alphas = [], []
        for c in range(2):
            for h in range(N_GROUPS):
                ch = c * N_GROUPS + h
                s = score_tile(ch) + bias_ref[h]
                shift = slopes[h] * off
                if first:
                    s = jnp.where(rel <= 0, s, NEG_BIG)
                    m_new = jnp.max(s, axis=0, keepdims=True) + shift
                    p = jnp.exp2(s - (m_new - shift))
                    l_ref[ch:ch + 1] = jnp.sum(p, axis=0, keepdims=True)
                    alphas.append(None)
                else:
                    m_old = m_ref[ch:ch + 1]
                    m_new = jnp.maximum(m_old, jnp.max(s, axis=0, keepdims=True) + shift)
                    alpha = jnp.exp2(m_old - m_new)
                    p = jnp.exp2(s - (m_new - shift))
                    l_ref[ch:ch + 1] = alpha * l_ref[ch:ch + 1] + jnp.sum(p, axis=0, keepdims=True)
                    alphas.append(alpha)
                m_ref[ch:ch + 1] = m_new
                probs.append(p.astype(BF16))
        for h in range(N_GROUPS):
            vh = v_ref[0, HEAD_DIM * h:HEAD_DIM * (h + 1), pl.ds(start, tb)]
            pv = _dot(vh, jnp.concatenate([probs[h], probs[N_GROUPS + h]], axis=1))
            for c in range(2):
                ch = c * N_GROUPS + h
                new = pv[:, c * tb:(c + 1) * tb]
                acc_ref[ch] = new if first else alphas[ch] * acc_ref[ch] + new

    s_diag = scores(qi)
    s_ref[...] = scores(jnp.maximum(qi - 1, 0))
    block(qi, lambda ch: s_diag[:, ch * tb:(ch + 1) * tb], True)

    def body(i, _):
        kb = qi - 1 - i
        s_next = scores(jnp.maximum(kb - 1, 0))
        block(kb, lambda ch: s_ref[:, ch * tb:(ch + 1) * tb], False)
        s_ref[...] = s_next
        return 0

    lax.fori_loop(0, qi, body, 0)
    lam = _lam_value(lp_ref[...], lam_init)
    ys = []
    for h in range(N_GROUPS):
        y0 = acc_ref[h] / l_ref[h:h + 1]
        y1 = acc_ref[N_GROUPS + h] / l_ref[N_GROUPS + h:N_GROUPS + h + 1]
        ys.append(y0 - lam * y1)
    yt = _group_rms_t(jnp.concatenate(ys, axis=0), og_ref[...], bd64_ref[...], HEAD_DIM) * (1.0 - lam_init)
    o_ref[0] = _dot_nt(eye_ref[...], yt.astype(BF16)).astype(BF16)


def _diff_attention(q, k, v, lp, og_col, bd64, lam_init, tb):
    n, t, w = q.shape
    maps = 2 * N_GROUPS
    eye = jnp.asarray(np.eye(tb), dtype=BF16)
    return pl.pallas_call(
        functools.partial(_diff_kernel, tb=tb, lam_init=lam_init),
        grid=(n, t // tb),
        in_specs=[pl.BlockSpec((1, tb, w), lambda i, j: (i, j, 0)),
                  pl.BlockSpec((1, t, w), lambda i, j: (i, 0, 0)),
                  pl.BlockSpec((1, w, t), lambda i, j: (i, 0, 0)),
                  _full_spec(lp.shape), _full_spec(og_col.shape), _full_spec(bd64.shape), _full_spec(eye.shape)],
        out_specs=pl.BlockSpec((1, tb, w), lambda i, j: (i, j, 0)),
        out_shape=jax.ShapeDtypeStruct((n, t, w), BF16),
        scratch_shapes=[pltpu.VMEM((maps * tb, w), BF16), pltpu.VMEM((N_GROUPS, tb, tb), F32),
                        pltpu.VMEM((maps, tb), F32), pltpu.VMEM((maps, tb), F32),
                        pltpu.VMEM((maps, HEAD_DIM, tb), F32), pltpu.VMEM((tb, maps * tb), F32)],
        compiler_params=_params("arbitrary", "arbitrary"),
        name="diff_attn",
    )(q, k, v, lp, og_col, bd64, eye)


def _merge_kernel(x_ref, yabm_ref, yc_ref, yd_ref, g1_ref, wg_ref, bg_ref, wb_ref, wo_ref, o_ref):
    x = x_ref[0]
    hb = _rms(x, g1_ref[...]).astype(BF16)
    ys = [yabm_ref[0, :, 0:BR_WIDTH], yabm_ref[0, :, BR_WIDTH:2 * BR_WIDTH], yc_ref[0], yd_ref[0],
          yabm_ref[0, :, 2 * BR_WIDTH:]]
    acc = None
    for b in range(5):
        gate = jax.nn.sigmoid(_dot(hb, wg_ref[:, b * D_MODEL:(b + 1) * D_MODEL]) + bg_ref[:, b * D_MODEL:(b + 1) * D_MODEL])
        term = gate * _dot(ys[b], wb_ref[b])
        acc = term if acc is None else acc + term
    o_ref[0] = x + _dot(acc.astype(BF16), wo_ref[...])


def _merge(x, yabm, yc, yd, p, tm):
    n, t, d = x.shape
    consts = [p["norm1_g"], p["w_gate"], p["b_gate"], p["w_branch"], p["w_o"]]
    seq = lambda w: pl.BlockSpec((1, tm, w), lambda i, j: (i, j, 0))
    return pl.pallas_call(
        _merge_kernel,
        grid=(n, t // tm),
        in_specs=[seq(d), seq(3 * BR_WIDTH), seq(BR_WIDTH), seq(BR_WIDTH)] + [_const_spec(c) for c in consts],
        out_specs=seq(d),
        out_shape=jax.ShapeDtypeStruct((n, t, d), F32),
        compiler_params=_params("arbitrary", "arbitrary"),
        name="merge",
    )(x, yabm, yc, yd, *map(_operand, consts))


FFN_COLS = 1408


def _ffn_kernel(x_ref, g2_ref, wup_ref, cw_ref, cb_ref, wdn_ref, o_ref, st_ref, carry_ref, *, tm):
    t = pl.program_id(1)
    x = x_ref[0]
    hb = _rms(x, g2_ref[...]).astype(BF16)

    @pl.when(t == 0)
    def _():
        carry_ref[...] = jnp.zeros_like(carry_ref)

    row = _row_iota((tm, FFN_COLS))

    def conv(lo):
        u = _dot(hb, wup_ref[:, lo:lo + FFN_COLS])
        prev = carry_ref[:, lo:lo + FFN_COLS]
        p1, p2 = prev[7:8], prev[6:7]
        s1 = jnp.where(row == 0, p1, pltpu.roll(u, 1, axis=0))
        s2 = jnp.where(row == 0, p2, jnp.where(row == 1, p1, pltpu.roll(u, 2, axis=0)))
        carry_ref[:, lo:lo + FFN_COLS] = u[tm - 8:tm]
        st_ref[0, :, lo:lo + FFN_COLS] = u[tm - 2:tm]
        return (cw_ref[0:1, lo:lo + FFN_COLS] * s2 + cw_ref[1:2, lo:lo + FFN_COLS] * s1
                + cw_ref[2:3, lo:lo + FFN_COLS] * u + cb_ref[:, lo:lo + FFN_COLS])

    acc = x
    for j in range(D_FF // FFN_COLS):
        gate = conv(j * FFN_COLS)
        val = conv(D_FF + j * FFN_COLS)
        act = (gate * jax.nn.sigmoid(gate) * val).astype(BF16)
        acc = acc + _dot(act, wdn_ref[j * FFN_COLS:(j + 1) * FFN_COLS, :])
    o_ref[0] = acc


def _ffn(x, p, tm):
    n, t, d = x.shape
    consts = [p["norm2_g"], p["w_up"], p["conv_ffn_w"], p["conv_ffn_b"], p["w_down"]]
    seq = pl.BlockSpec((1, tm, d), lambda i, j: (i, j, 0))
    return pl.pallas_call(
        functools.partial(_ffn_kernel, tm=tm),
        grid=(n, t // tm),
        in_specs=[seq] + [_const_spec(c) for c in consts],
        out_specs=[seq, pl.BlockSpec((1, 2, 2 * D_FF), lambda i, j: (i, 0, 0))],
        out_shape=[jax.ShapeDtypeStruct((n, t, d), F32), jax.ShapeDtypeStruct((n, 2, 2 * D_FF), F32)],
        scratch_shapes=[pltpu.VMEM((8, 2 * D_FF), F32)],
        compiler_params=_params("arbitrary", "arbitrary"),
        name="conv_ffn",
    )(x, *map(_operand, consts))


def _ffn_step_kernel(x_ref, p0_ref, p1_ref, g2_ref, wup_ref, cw_ref, cb_ref, wdn_ref, o_ref, u_ref):
    x = x_ref[...]
    hb = _rms(x, g2_ref[...]).astype(BF16)

    def conv(lo):
        u = _dot(hb, wup_ref[:, lo:lo + FFN_COLS])
        u_ref[:, lo:lo + FFN_COLS] = u
        return (cw_ref[0:1, lo:lo + FFN_COLS] * p0_ref[:, lo:lo + FFN_COLS]
                + cw_ref[1:2, lo:lo + FFN_COLS] * p1_ref[:, lo:lo + FFN_COLS]
                + cw_ref[2:3, lo:lo + FFN_COLS] * u + cb_ref[:, lo:lo + FFN_COLS])

    acc = x
    for j in range(D_FF // FFN_COLS):
        gate = conv(j * FFN_COLS)
        val = conv(D_FF + j * FFN_COLS)
        act = (gate * jax.nn.sigmoid(gate) * val).astype(BF16)
        acc = acc + _dot(act, wdn_ref[j * FFN_COLS:(j + 1) * FFN_COLS, :])
    o_ref[...] = acc


def _ffn_step(x, prev0, prev1, p):
    m, d = x.shape
    consts = [p["norm2_g"], p["w_up"], p["conv_ffn_w"], p["conv_ffn_b"], p["w_down"]]
    args = [x, prev0, prev1] + consts
    return pl.pallas_call(
        _ffn_step_kernel,
        grid=(1,),
        in_specs=[_const_spec(a) for a in args],
        out_specs=[_full_spec((m, d)), _full_spec((m, 2 * D_FF))],
        out_shape=[jax.ShapeDtypeStruct((m, d), F32), jax.ShapeDtypeStruct((m, 2 * D_FF), F32)],
        compiler_params=_params("arbitrary"),
        name="conv_ffn_step",
    )(*map(_operand, args))


def _inproj_step_kernel(x_ref, c0_ref, c1_ref, g1_ref, win_ref, cw_ref, lng_ref, lnb_ref, ws_ref, gb_ref,
                        dqg_ref, dkg_ref, mqg_ref, bd32_ref, bd64_ref,
                        yab_ref, q_ref, ksb_ref, vsb_ref, kdf_ref, vdf_ref, vn_ref, ach_ref):
    hb = _rms(x_ref[...], g1_ref[...]).astype(BF16)

    def proj(lo, hi):
        return _dot(hb, win_ref[:, lo:hi])

    pa = proj(0, 3 * BR_WIDTH)
    ach = pa[:, BR_WIDTH:2 * BR_WIDTH] * pa[:, 2 * BR_WIDTH:]
    cw = cw_ref[...]
    yab_ref[:, 0:BR_WIDTH] = (pa[:, :BR_WIDTH] * (cw[0:1] * c0_ref[...] + cw[1:2] * c1_ref[...] + cw[2:3] * ach)).astype(BF16)
    ach_ref[...] = ach

    pg = proj(3 * BR_WIDTH, 5 * BR_WIDTH)
    vn = _layer_norm(pg[:, BR_WIDTH:], lng_ref[...], lnb_ref[...])
    vn_ref[...] = vn
    lane = _lane_iota((1, BR_WIDTH))
    w00 = jnp.zeros((1, BR_WIDTH), F32)
    for g in range(N_GROUPS):
        w00 = jnp.where((lane >= HEAD_DIM * g) & (lane < HEAD_DIM * (g + 1)), ws_ref[g, 0:1, 0:1], w00)
    yab_ref[:, BR_WIDTH:] = (pg[:, :BR_WIDTH] * (w00 * vn + gb_ref[0:1, :])).astype(BF16)

    ps = proj(5 * BR_WIDTH, 8 * BR_WIDTH)
    q_ref[:, 0:BR_WIDTH] = (ps[:, :BR_WIDTH] * (HEAD_DIM ** -0.5)).astype(BF16)
    ksb_ref[...] = ps[:, BR_WIDTH:2 * BR_WIDTH]
    vsb_ref[...] = ps[:, 2 * BR_WIDTH:]

    pd = proj(8 * BR_WIDTH, 11 * BR_WIDTH)
    bd32 = bd32_ref[...]
    qd = _group_rms(pd[:, :BR_WIDTH], dqg_ref[...], bd32, DIFF_SUB)
    q_ref[:, BR_WIDTH:2 * BR_WIDTH] = (qd * (DIFF_SUB ** -0.5 * LOG2E)).astype(BF16)
    kdf_ref[...] = _group_rms(pd[:, BR_WIDTH:2 * BR_WIDTH], dkg_ref[...], bd32, DIFF_SUB)
    vdf_ref[...] = pd[:, 2 * BR_WIDTH:]

    qm = _group_rms(proj(11 * BR_WIDTH, 12 * BR_WIDTH), mqg_ref[...], bd64_ref[...], HEAD_DIM)
    q_ref[:, 2 * BR_WIDTH:] = (qm * (HEAD_DIM ** -0.5 * LOG2E)).astype(BF16)


def _inproj_step(x, c0, c1, p):
    m, d = x.shape
    consts = [p["norm1_g"], p["w_in"], p["conv_a_w"], p["gmlp_ln_g"], p["gmlp_ln_b"], p["gmlp_ws"],
              p["gmlp_b_full"], p["diff_qn_g"], p["diff_kn_g"], p["mem_qn_g"], p["bd32"], p["bd64"]]
    args = [x, c0, c1] + consts
    fl = jax.ShapeDtypeStruct((m, BR_WIDTH), F32)
    out_shape = [jax.ShapeDtypeStruct((m, 2 * BR_WIDTH), BF16), jax.ShapeDtypeStruct((m, 3 * BR_WIDTH), BF16)] + [fl] * 6
    return pl.pallas_call(
        _inproj_step_kernel,
        grid=(1,),
        in_specs=[_const_spec(a) for a in args],
        out_specs=[_full_spec(s.shape) for s in out_shape],
        out_shape=out_shape,
        compiler_params=_params("arbitrary"),
        name="in_proj_step",
    )(*map(_operand, args))


def _decode_kernel(pt_ref, q_ref, kn_ref, vn_ref, mk_ref, mv_ref, lp_ref, og_ref, bd64_ref, ll_ref, ones_ref,
                   ksb_hbm, vsb_hbm, kdf_hbm, vdf_hbm, o_ref, buf_ref, sem_ref, *, layer, n_samples, n_pages, page,
                   lam_init):
    step = pl.program_id(0)
    pools = (ksb_hbm, vsb_hbm, kdf_hbm, vdf_hbm)

    def page_copy(sample, slot, c, j):
        return pltpu.make_async_copy(pools[c].at[layer, pt_ref[sample, j]], buf_ref.at[slot, c, j], sem_ref.at[slot])

    def for_each_page(sample, slot, fn):
        for c in range(len(pools)):
            for j in range(n_pages):
                fn(page_copy(sample, slot, c, j))

    ahead = DECODE_SLOTS - 1

    @pl.when(step == 0)
    def _():
        for s in range(min(ahead, n_samples)):
            for_each_page(s, s, lambda cp: cp.start())

    @pl.when(step + ahead < n_samples)
    def _():
        for_each_page(step + ahead, lax.rem(step + ahead, DECODE_SLOTS), lambda cp: cp.start())

    slot = lax.rem(step, DECODE_SLOTS)
    for_each_page(step, slot, lambda cp: cp.wait())
    ksb, vsb, kdf, vdf = ([buf_ref.at[slot, c, j] for j in range(n_pages)] for c in range(len(pools)))
    rows = 2 * N_GROUPS
    n_sub = HEAD_DIM // DIFF_SUB
    ones = ones_ref[...]

    def q_column(col):
        qrow = jnp.broadcast_to(q_ref[0, :, col * BR_WIDTH:(col + 1) * BR_WIDTH].astype(F32), (BR_WIDTH, BR_WIDTH))
        diag = jnp.where(_row_iota(qrow.shape) == _lane_iota(qrow.shape), qrow, 0.0).astype(BF16)
        return _dot(diag, ones)

    def group_scores(kt, qcol):
        return jnp.sum((kt * qcol).reshape(rows, DIFF_SUB, kt.shape[1]), axis=1)

    def pair_sum(z):
        r = z.shape[0]
        even = (_row_iota(z.shape) & 1) == 0
        return z + jnp.where(even, pltpu.roll(z, r - 1, axis=0), pltpu.roll(z, 1, axis=0))

    def head_rows(w):
        return jnp.concatenate([jnp.broadcast_to(w[n_sub * h:n_sub * h + 1], (HEAD_DIM, w.shape[1]))
                                for h in range(N_GROUPS)], axis=0)

    def reduce_positions(acc):
        hi, lo = _split_bf16(acc)
        ones_row = jnp.ones((rows, acc.shape[1]), BF16)
        return _dot_nt(ones_row, hi) + _dot_nt(ones_row, lo)

    qc = q_column(0)
    z = jnp.concatenate([pair_sum(group_scores(ksb[pg][...], qc)) for pg in range(n_pages)], axis=0)
    lb, l1m = _log_sigmoid_pair(z)
    hi, lo = _split_bf16(l1m)
    hl = jnp.concatenate([hi, lo], axis=1)
    within = _dot(hl, ll_ref[...])
    total = _dot(hl, ones)
    carry = jnp.zeros((rows, page), F32)
    carries = [None] * n_pages
    for pg in reversed(range(n_pages)):
        carries[pg] = carry
        carry = carry + total[rows * pg:rows * (pg + 1)]
    a = jnp.exp(lb + within + jnp.concatenate(carries, axis=0))
    acc = jnp.zeros((BR_WIDTH, page), F32)
    for pg in range(n_pages):
        acc = acc + head_rows(a[rows * pg:rows * (pg + 1)]) * vsb[pg][...]
    y_c = reduce_positions(acc)[0:1]

    qc = q_column(1)
    s = jnp.concatenate([group_scores(kdf[pg][...], qc) for pg in range(n_pages)], axis=0)
    slopes = _alibi_slopes_log2()
    all_rows = _row_iota((rows * n_pages, page))
    head = _div_pow2(_rem_pow2(all_rows, rows), n_sub)
    sl = jnp.full(all_rows.shape, slopes[0], F32)
    for h in range(1, N_GROUPS):
        sl = jnp.where(head == h, slopes[h], sl)
    past = n_pages * page
    k_pos = _div_pow2(all_rows, rows) * page + _lane_iota(all_rows.shape)
    s = s + sl * (k_pos - past).astype(F32)
    row8 = _row_iota((rows, BR_WIDTH))
    group_lanes = _div_pow2(_lane_iota((rows, BR_WIDTH)), DIFF_SUB) == row8
    q8 = jnp.broadcast_to(q_ref[0, :, BR_WIDTH:2 * BR_WIDTH].astype(F32), (rows, BR_WIDTH))
    kn8 = jnp.broadcast_to(kn_ref[0], (rows, BR_WIDTH))
    s_new = jnp.sum(jnp.where(group_lanes, q8 * kn8, 0.0), axis=1, keepdims=True)
    m8 = s[0:rows]
    for pg in range(1, n_pages):
        m8 = jnp.maximum(m8, s[rows * pg:rows * (pg + 1)])
    m = jnp.maximum(jnp.max(m8, axis=1, keepdims=True), s_new)
    p_new = jnp.exp2(s_new - m)
    m_all = jnp.concatenate([jnp.broadcast_to(m, (rows, page))] * n_pages, axis=0)
    p = jnp.exp2(s - m_all)
    l8 = p[0:rows]
    for pg in range(1, n_pages):
        l8 = l8 + p[rows * pg:rows * (pg + 1)]
    l = jnp.sum(l8, axis=1, keepdims=True) + p_new
    lam = _lam_value(lp_ref[...], lam_init)
    first_map = (_row_iota((rows, 1)) & 1) == 0
    coef = jnp.where(first_map, 1.0, -lam) / l
    w = pair_sum(p * jnp.concatenate([jnp.broadcast_to(coef, (rows, page))] * n_pages, axis=0))
    acc = jnp.zeros((BR_WIDTH, page), F32)
    for pg in range(n_pages):
        acc = acc + head_rows(w[rows * pg:rows * (pg + 1)]) * vdf[pg][...]
    w_new = pair_sum(p_new * coef)
    head_lanes = (_div_pow2(_lane_iota((rows, BR_WIDTH)), HEAD_DIM) * n_sub) == row8
    vn8 = jnp.broadcast_to(vn_ref[0], (rows, BR_WIDTH))
    y_new = jnp.sum(jnp.where(head_lanes, w_new * vn8, 0.0), axis=0, keepdims=True)
    y_d = reduce_positions(acc)[0:1] + y_new
    y_d = _group_rms(jnp.broadcast_to(y_d, (rows, BR_WIDTH)), og_ref[...], bd64_ref[...], HEAD_DIM)[0:1]
    y_d = y_d * (1.0 - lam_init)

    qc = q_column(2)
    n_mem = mk_ref.shape[2]
    qc = jnp.concatenate([qc] * (n_mem // page), axis=1)
    s = pair_sum(group_scores(mk_ref[0], qc))
    p = jnp.exp2(s - jnp.max(s, axis=1, keepdims=True))
    p = p / jnp.sum(p, axis=1, keepdims=True)
    y_m = reduce_positions(head_rows(p) * mv_ref[0])[0:1]

    o_ref[0, :, 0:BR_WIDTH] = y_c.astype(BF16)
    o_ref[0, :, BR_WIDTH:2 * BR_WIDTH] = y_d.astype(BF16)
    o_ref[0, :, 2 * BR_WIDTH:] = y_m.astype(BF16)


def _decode_attention(layer, page_table, q, k_new, v_new, caches, mem_k, mem_v, p, lam_init):
    b = q.shape[0]
    n_pages = page_table.shape[1]
    page = caches[0].shape[3]
    ll = _suffix_ones(page)
    row_spec = lambda w: pl.BlockSpec((1, 1, w), lambda i, pt: (i, 0, 0))
    mem_spec = pl.BlockSpec((None, 1) + mem_k.shape[2:], lambda i, pt: (layer, i, 0, 0))
    const = lambda a: pl.BlockSpec(a.shape, lambda i, pt: (0,) * a.ndim)
    consts = [p["diff_lambda"], p["diff_out_g"], p["bd64"], ll, jnp.ones((BR_WIDTH, page), BF16)]
    grid_spec = pltpu.PrefetchScalarGridSpec(
        num_scalar_prefetch=1,
        grid=(b,),
        in_specs=[row_spec(3 * BR_WIDTH), row_spec(BR_WIDTH), row_spec(BR_WIDTH), mem_spec, mem_spec]
        + [const(c) for c in consts] + [pl.BlockSpec(memory_space=pl.ANY)] * len(caches),
        out_specs=row_spec(3 * BR_WIDTH),
        scratch_shapes=[pltpu.VMEM((DECODE_SLOTS, len(caches), n_pages, BR_WIDTH, page), F32),
                        pltpu.SemaphoreType.DMA((DECODE_SLOTS,))],
    )
    return pl.pallas_call(
        functools.partial(_decode_kernel, layer=layer, n_samples=b, n_pages=n_pages, page=page, lam_init=lam_init),
        grid_spec=grid_spec,
        out_shape=jax.ShapeDtypeStruct((b, 1, 3 * BR_WIDTH), BF16),
        compiler_params=_params("arbitrary"),
        name="decode_attn",
    )(page_table, q, k_new, v_new, mem_k, mem_v, *consts, *caches)


PROMPT_TILE = 512
DECODE_SLOTS = 3
ATTN_BLOCK = 256


def _cast_weights(w):
    big = {k: w[k].astype(BF16) for k in ("w_in", "w_gate", "w_branch", "w_o", "w_up", "w_down")}
    w_in = big["w_in"]
    big["w_in_tok"] = jnp.concatenate([w_in[:, :, :7 * BR_WIDTH], w_in[:, :, 8 * BR_WIDTH:10 * BR_WIDTH],
                                       w_in[:, :, 11 * BR_WIDTH:]], axis=2)
    return big


def _layer_params(l, w, big):
    row = lambda a: a[l].reshape(1, -1).astype(F32)
    tile4 = lambda a: jnp.tile(a[l].astype(F32), N_GROUPS).reshape(1, -1)
    col4 = lambda a: jnp.tile(a[l].astype(F32), N_GROUPS).reshape(-1, 1)
    w_in = big["w_in"][l]
    return dict(
        norm1_g=row(w["norm1_g"]), w_in=_Layered(big["w_in"], l), conv_a_w=w["conv_a_w"][l].astype(F32),
        w_sb_kv_t=w_in[:, 6 * BR_WIDTH:8 * BR_WIDTH].T, w_df_kv_t=w_in[:, 9 * BR_WIDTH:11 * BR_WIDTH].T,
        w_in_tok=_Layered(big["w_in_tok"], l),
        diff_kn_g_col=col4(w["diff_kn_g"]), mem_kn_g_col=col4(w["mem_kn_g"]), diff_out_g_col=col4(w["diff_out_g"]),
        w_mem_kv_t=w["w_mem_kv"][l].astype(BF16).T,
        gmlp_ln_g=row(w["gmlp_ln_g"]), gmlp_ln_b=row(w["gmlp_ln_b"]), gmlp_ws=w["gmlp_ws"][l].astype(F32),
        gmlp_b_full=jnp.repeat(w["gmlp_b"][l].astype(F32).T, HEAD_DIM, axis=1),
        diff_qn_g=tile4(w["diff_qn_g"]), diff_kn_g=tile4(w["diff_kn_g"]), diff_out_g=tile4(w["diff_out_g"]),
        diff_lambda=w["diff_lambda"][l].astype(F32),
        mem_norm_g=row(w["mem_norm_g"]), mem_qn_g=tile4(w["mem_qn_g"]),
        w_branch=_Layered(big["w_branch"], l), w_gate=_Layered(big["w_gate"], l), b_gate=row(w["b_gate"]),
        w_o=_Layered(big["w_o"], l), norm2_g=row(w["norm2_g"]), w_up=_Layered(big["w_up"], l),
        conv_ffn_w=w["conv_ffn_w"][l].astype(F32), conv_ffn_b=row(w["conv_ffn_b"]),
        w_down=_Layered(big["w_down"], l),
        bd32=_block_ones(DIFF_SUB), bd64=_block_ones(HEAD_DIM),
    )


def _prompt_layer(x, mem, p, lam_init):
    n, t, _ = x.shape
    tm = min(PROMPT_TILE, t)
    tb = min(ATTN_BLOCK, t)
    mk, mv, mk16, mv16 = _mem_kv(mem, p["mem_norm_g"], p["w_mem_kv_t"], p["mem_kn_g_col"], p["bd64"])
    (yabm, qsb, ksb16, vsb16, qdf, kdf16, vdf16, ksb, vsb, kdf, vdf, ca) = _inproj(x, p, mk16, mv16, tm)
    yc = _sb_attention(qsb, ksb16, vsb16, tb)
    yd = _diff_attention(qdf, kdf16, vdf16, p["diff_lambda"], p["diff_out_g_col"], p["bd64"], lam_init, tb)
    x1 = _merge(x, yabm, yc, yd, p, tm)
    x2, cf = _ffn(x1, p, tm)
    return x2, ca, ksb, vsb, kdf, vdf, mk, mv, cf


def _sample_layer(l, x, conv_a, conv_ffn, page_table, caches, mem_k, mem_v, p, lam_init):
    b = x.shape[0]
    yab, q, ksb, vsb, kdf, vdf, vn, ach = _inproj_step(x, conv_a[:, 0], conv_a[:, 1], p)
    row3 = lambda a: a.reshape(b, 1, -1)
    ycdm = _decode_attention(l, page_table, row3(q), row3(kdf), row3(vdf), caches, mem_k, mem_v, p, lam_init)
    yabm = jnp.concatenate([yab, ycdm[:, 0, 2 * BR_WIDTH:]], axis=1)
    x1 = _merge(x[None], yabm[None], ycdm[None, :, 0, 0:BR_WIDTH], ycdm[None, :, 0, BR_WIDTH:2 * BR_WIDTH], p, b)[0]
    x2, u = _ffn_step(x1, conv_ffn[:, 0], conv_ffn[:, 1], p)
    ca_new = jnp.stack([conv_a[:, 1], ach], axis=1)
    cf_new = jnp.stack([conv_ffn[:, 1], u], axis=1)
    return x2, ca_new, vn, ksb, vsb, kdf, vdf, cf_new


def kernel(x_prompt, x_sample, state_conv_a, cache_k_sb, cache_v_sb, cache_k_diff, cache_v_diff, cache_mem_k,
           cache_mem_v, state_conv_ffn, page_table, mem_prompt, norm1_g, w_in, conv_a_w, gmlp_ln_g, gmlp_ln_b,
           gmlp_ws, gmlp_b, diff_qn_g, diff_kn_g, diff_lambda, diff_out_g, mem_norm_g, w_mem_kv, mem_qn_g,
           mem_kn_g, w_branch, w_gate, b_gate, w_o, norm2_g, w_up, conv_ffn_w, conv_ffn_b, w_down):
    w = dict(norm1_g=norm1_g, w_in=w_in, conv_a_w=conv_a_w, gmlp_ln_g=gmlp_ln_g, gmlp_ln_b=gmlp_ln_b,
             gmlp_ws=gmlp_ws, gmlp_b=gmlp_b, diff_qn_g=diff_qn_g, diff_kn_g=diff_kn_g, diff_lambda=diff_lambda,
             diff_out_g=diff_out_g, mem_norm_g=mem_norm_g, w_mem_kv=w_mem_kv, mem_qn_g=mem_qn_g,
             mem_kn_g=mem_kn_g, w_branch=w_branch, w_gate=w_gate, b_gate=b_gate, w_o=w_o, norm2_g=norm2_g,
             w_up=w_up, conv_ffn_w=conv_ffn_w, conv_ffn_b=conv_ffn_b, w_down=w_down)
    depth = w_in.shape[0]
    n_p, t_p, _ = x_prompt.shape
    n_s = x_sample.shape[0]
    flat = lambda c: jnp.transpose(c, (0, 1, 3, 4, 2)).reshape(c.shape[:2] + (BR_WIDTH, c.shape[2]))
    caches = [flat(cache_k_sb), flat(cache_v_sb), flat(cache_k_diff), flat(cache_v_diff)]
    mem_k, mem_v = flat(cache_mem_k), flat(cache_mem_v)
    xp, xs = x_prompt, x_sample[:, 0]
    outs_p, outs_s = [], []
    big = _cast_weights(w)
    for l in range(depth):
        p = _layer_params(l, w, big)
        lam_init = 0.8 - 0.6 * math.exp(-0.3 * l)
        xp, *rest_p = _prompt_layer(xp, mem_prompt, p, lam_init)
        outs_p.append(rest_p)
        xs, *rest_s = _sample_layer(l, xs, state_conv_a[l], state_conv_ffn[l], page_table, caches, mem_k, mem_v,
                                    p, lam_init)
        outs_s.append(rest_s)
    heads = lambda a: a.reshape(a.shape[:-1] + (N_GROUPS, HEAD_DIM))
    stack_p = lambda i: jnp.stack([o[i] for o in outs_p], axis=0)
    stack_s = lambda i: jnp.stack([o[i] for o in outs_s], axis=0)
    step = lambda a: a.reshape(depth, n_s, 1, -1)

    def heads_t(a):
        d0, n, _, t = a.shape
        return jnp.transpose(a.reshape(d0, n, N_GROUPS, HEAD_DIM, t), (0, 1, 4, 2, 3))

    return (xp, xs[:, None, :],
            stack_p(0), stack_s(0), step(stack_s(1)),
            heads_t(stack_p(1)), heads_t(stack_p(2)), heads(step(stack_s(2))), heads(step(stack_s(3))),
            heads_t(stack_p(3)), heads_t(stack_p(4)), heads(step(stack_s(4))), heads(step(stack_s(5))),
            heads_t(stack_p(5)), heads_t(stack_p(6)),
            stack_p(7), stack_s(6))
```

```python
import functools
import math
from typing import NamedTuple

import numpy as np
import jax
import jax.numpy as jnp
from jax import lax
from jax.experimental import pallas as pl
from jax.experimental.pallas import tpu as pltpu

D_MODEL = 1024
HEAD_DIM = 64
BR_WIDTH = 256
N_GROUPS = 4
DIFF_SUB = 32
CHUNK = 128
D_FF = 2816
EPS = 1e-6
NEG_BIG = -1e30
SB_UNDERFLOW_LOG2 = -150.0
LOG2E = 1.4426950408889634
VMEM_LIMIT_BYTES = 56 * 1024 * 1024

F32 = jnp.float32
BF16 = jnp.bfloat16


def _dot(a, b):
    return jnp.dot(a, b, preferred_element_type=F32)


def _dot_nt(a, b):
    return lax.dot_general(a, b, (((1,), (1,)), ((), ())), preferred_element_type=F32)


def _split_bf16(x):
    hi = x.astype(BF16)
    lo = (x - hi.astype(F32)).astype(BF16)
    return hi, lo


def _rms(x, g):
    ms = jnp.mean(x * x, axis=-1, keepdims=True)
    return x * lax.rsqrt(ms + EPS) * g


def _group_rms(z, g, ones_bd, group):
    ms = _dot((z * z).astype(BF16), ones_bd) * (1.0 / group)
    return z * lax.rsqrt(ms + EPS) * g


def _lane_iota(shape):
    return lax.broadcasted_iota(jnp.int32, shape, len(shape) - 1)


def _row_iota(shape):
    return lax.broadcasted_iota(jnp.int32, shape, len(shape) - 2)


def _div_pow2(x, d):
    assert d & (d - 1) == 0
    return lax.shift_right_logical(x, d.bit_length() - 1)


def _rem_pow2(x, d):
    assert d & (d - 1) == 0
    return x & (d - 1)


def _head_select(parts):
    lane = _lane_iota(parts[0].shape)
    out = parts[0]
    for h in range(1, N_GROUPS):
        out = jnp.where(lane >= HEAD_DIM * h, parts[h], out)
    return out


def _stack_heads(q, n_sub):
    lane = _lane_iota(q.shape)
    width = HEAD_DIM // n_sub
    zero = jnp.zeros_like(q)
    parts = []
    for c in range(n_sub):
        for h in range(N_GROUPS):
            lo = HEAD_DIM * h + width * c
            parts.append(jnp.where((lane >= lo) & (lane < lo + width), q, zero))
    return jnp.concatenate(parts, axis=0)


def _log_sigmoid_pair(z):
    lb = jnp.minimum(z, 0.0) - jnp.log(1.0 + jnp.exp(-jnp.abs(z)))
    return lb, lb - z


def _log2_sigmoid_pair(z2):
    lb = jnp.minimum(z2, 0.0) - jnp.log2(1.0 + jnp.exp2(-jnp.abs(z2)))
    return lb, lb - z2


def _lam_value(lp, lam_init):
    a = jnp.sum(lp[0:1] * lp[1:2], axis=1, keepdims=True)
    b = jnp.sum(lp[2:3] * lp[3:4], axis=1, keepdims=True)
    return jnp.exp(a) - jnp.exp(b) + lam_init


def _block_ones(group):
    i = np.arange(BR_WIDTH)
    return jnp.asarray((i[:, None] // group) == (i[None, :] // group), dtype=BF16)


def _suffix_ones(n):
    i = np.arange(n)
    l = (i[:, None] > i[None, :])
    return jnp.asarray(np.concatenate([l, l], axis=0), dtype=BF16)


def _alibi_slopes_log2():
    return [LOG2E * 2.0 ** (-8.0 * (h + 1) / N_GROUPS) for h in range(N_GROUPS)]


def _full_spec(shape):
    nd = len(shape)
    return pl.BlockSpec(shape, lambda *_: (0,) * nd, pipeline_mode=pl.Buffered(1))


class _Layered(NamedTuple):
    array: jax.Array
    layer: int

    @property
    def shape(self):
        return self.array.shape[1:]


def _const_spec(c):
    if isinstance(c, _Layered):
        rest = (0,) * len(c.shape)
        return pl.BlockSpec((None,) + c.shape, lambda *_: (c.layer,) + rest, pipeline_mode=pl.Buffered(1))
    return _full_spec(c.shape)


def _operand(c):
    return c.array if isinstance(c, _Layered) else c


def _params(*sem):
    return pltpu.CompilerParams(dimension_semantics=sem, vmem_limit_bytes=VMEM_LIMIT_BYTES)


def _group_rms_t(zt, g_col, ones_bd, group):
    hi, lo = _split_bf16(zt * zt)
    ms = (_dot(ones_bd, hi) + _dot(ones_bd, lo)) * (1.0 / group)
    return zt * lax.rsqrt(ms + EPS) * g_col


def _transpose_bf16(xt, eye):
    return _dot_nt(eye, xt)


def _memkv_kernel(mem_ref, g_ref, wt_ref, kg_ref, bd64_ref, eye_ref, k_ref, v_ref, k16_ref, v16_ref):
    hb = _rms(mem_ref[0], g_ref[...]).astype(BF16)
    kvt = _dot_nt(wt_ref[...], hb)
    k = _group_rms_t(kvt[:BR_WIDTH], kg_ref[...], bd64_ref[...], HEAD_DIM)
    v = kvt[BR_WIDTH:]
    k_ref[0] = k
    v_ref[0] = v
    k16_ref[0] = _transpose_bf16(k.astype(BF16), eye_ref[...]).astype(BF16)
    v16_ref[0] = v.astype(BF16)


def _mem_kv(mem, g, w, kg, bd64):
    n, m, d = mem.shape
    blk = pl.BlockSpec((1, BR_WIDTH, m), lambda i: (i, 0, 0))
    eye = jnp.asarray(np.eye(m), dtype=BF16)
    return pl.pallas_call(
        _memkv_kernel,
        grid=(n,),
        in_specs=[pl.BlockSpec((1, m, d), lambda i: (i, 0, 0)), _full_spec(g.shape), _full_spec(w.shape),
                  _full_spec(kg.shape), _full_spec(bd64.shape), _full_spec(eye.shape)],
        out_specs=[blk, blk, pl.BlockSpec((1, m, BR_WIDTH), lambda i: (i, 0, 0)), blk],
        out_shape=[jax.ShapeDtypeStruct((n, BR_WIDTH, m), F32)] * 2
        + [jax.ShapeDtypeStruct((n, m, BR_WIDTH), BF16), jax.ShapeDtypeStruct((n, BR_WIDTH, m), BF16)],
        compiler_params=_params("arbitrary"),
        name="mem_kv",
    )(mem, g, w, kg, bd64, eye)


def _layer_norm(x, g, b):
    mu = jnp.mean(x, axis=-1, keepdims=True)
    xc = x - mu
    return xc * lax.rsqrt(jnp.mean(xc * xc, axis=-1, keepdims=True) + EPS) * g + b


def _mem_attention(qm, mk, mvt, eye):
    r = qm.shape[0]
    st = _dot_nt(mk, _stack_heads(qm, 1))
    p = jnp.exp2(st - jnp.max(st, axis=0, keepdims=True))
    inv = 1.0 / jnp.sum(p, axis=0, keepdims=True)
    ot = _dot(mvt, p.astype(BF16))
    yt = jnp.concatenate([ot[HEAD_DIM * h:HEAD_DIM * (h + 1), h * r:(h + 1) * r] * inv[:, h * r:(h + 1) * r]
                          for h in range(N_GROUPS)], axis=0)
    return _transpose_bf16(yt.astype(BF16), eye)


def _inproj_kernel(x_ref, g1_ref, win_ref, wsbt_ref, wdft_ref, cw_ref, lng_ref, lnb_ref, ws_ref, gb_ref, dqg_ref,
                   dkg_ref, dkgr_ref, mqg_ref, bd32_ref, bd64_ref, eye_ref, mk_ref, mv_ref,
                   yabm_ref, qsb_ref, ksb16_ref, vsb16_ref, qdf_ref, kdf16_ref, vdf16_ref,
                   ksb_ref, vsb_ref, kdf_ref, vdf_ref, ca_ref, carry_ref, *, tm):
    t = pl.program_id(1)
    hb = _rms(x_ref[0], g1_ref[...]).astype(BF16)

    p_all = _dot(hb, win_ref[...])

    def proj(lo, hi):
        return p_all[:, lo:hi]

    pa = proj(0, 3 * BR_WIDTH)
    ach = pa[:, BR_WIDTH:2 * BR_WIDTH] * pa[:, 2 * BR_WIDTH:]

    @pl.when(t == 0)
    def _():
        carry_ref[...] = jnp.zeros_like(carry_ref)

    prev = carry_ref[...]
    p1, p2 = prev[7:8], prev[6:7]
    row = _row_iota(ach.shape)
    s1 = jnp.where(row == 0, p1, pltpu.roll(ach, 1, axis=0))
    s2 = jnp.where(row == 0, p2, jnp.where(row == 1, p1, pltpu.roll(ach, 2, axis=0)))
    cw = cw_ref[...]
    yabm_ref[0, :, 0:BR_WIDTH] = (pa[:, :BR_WIDTH] * (cw[0:1] * s2 + cw[1:2] * s1 + cw[2:3] * ach)).astype(BF16)
    carry_ref[...] = ach[tm - 8:tm]
    ca_ref[0] = ach[tm - 2:tm]

    pg = proj(3 * BR_WIDTH, 5 * BR_WIDTH)
    vn = _layer_norm(pg[:, BR_WIDTH:], lng_ref[...], lnb_ref[...])
    tril = _row_iota((CHUNK, CHUNK)) >= _lane_iota((CHUNK, CHUNK))
    wsm = [jnp.where(tril, ws_ref[g], 0.0).astype(BF16) for g in range(N_GROUPS)]
    gb = gb_ref[...]
    n_chunks = tm // CHUNK
    vc = jnp.concatenate([vn[c * CHUNK:(c + 1) * CHUNK] for c in range(n_chunks)], axis=1).astype(BF16)
    channel = _rem_pow2(_lane_iota(vc.shape), BR_WIDTH)
    mixed = _dot(wsm[0], vc)
    for g in range(1, N_GROUPS):
        mixed = jnp.where(channel >= HEAD_DIM * g, _dot(wsm[g], vc), mixed)
    for c in range(n_chunks):
        yabm_ref[0, c * CHUNK:(c + 1) * CHUNK, BR_WIDTH:2 * BR_WIDTH] = (
            pg[c * CHUNK:(c + 1) * CHUNK, :BR_WIDTH] * (mixed[:, c * BR_WIDTH:(c + 1) * BR_WIDTH] + gb)).astype(BF16)

    pq = proj(5 * BR_WIDTH, 10 * BR_WIDTH)
    qsb_ref[0] = (pq[:, :BR_WIDTH] * (HEAD_DIM ** -0.5 * LOG2E)).astype(BF16)
    kvt = _dot_nt(wsbt_ref[...], hb)
    ksb_ref[0] = kvt[:BR_WIDTH]
    vsb_ref[0] = kvt[BR_WIDTH:]
    ksb16_ref[0] = pq[:, BR_WIDTH:2 * BR_WIDTH].astype(BF16)
    vsb16_ref[0] = kvt[BR_WIDTH:].astype(BF16)

    bd32 = bd32_ref[...]
    qd = _group_rms(pq[:, 2 * BR_WIDTH:3 * BR_WIDTH], dqg_ref[...], bd32, DIFF_SUB)
    qdf_ref[0] = (qd * (DIFF_SUB ** -0.5 * LOG2E)).astype(BF16)
    kvt = _dot_nt(wdft_ref[...], hb)
    kd = _group_rms_t(kvt[:BR_WIDTH], dkg_ref[...], bd32, DIFF_SUB)
    kdf_ref[0] = kd
    vdf_ref[0] = kvt[BR_WIDTH:]
    kdf16_ref[0] = _group_rms(pq[:, 3 * BR_WIDTH:4 * BR_WIDTH], dkgr_ref[...], bd32, DIFF_SUB).astype(BF16)
    vdf16_ref[0] = kvt[BR_WIDTH:].astype(BF16)

    qm = _group_rms(pq[:, 4 * BR_WIDTH:], mqg_ref[...], bd64_ref[...], HEAD_DIM)
    qm = (qm * (HEAD_DIM ** -0.5 * LOG2E)).astype(BF16)
    yabm_ref[0, :, 2 * BR_WIDTH:] = _mem_attention(qm, mk_ref[0], mv_ref[0], eye_ref[...]).astype(BF16)


def _inproj(x, p, mk16, mv16, tm):
    n, t, d = x.shape
    consts = [p["norm1_g"], p["w_in_tok"], p["w_sb_kv_t"], p["w_df_kv_t"], p["conv_a_w"], p["gmlp_ln_g"], p["gmlp_ln_b"],
              p["gmlp_ws"], p["gmlp_b_full"], p["diff_qn_g"], p["diff_kn_g_col"], p["diff_kn_g"], p["mem_qn_g"],
              p["bd32"], p["bd64"], jnp.asarray(np.eye(tm), dtype=BF16)]
    seq = lambda w: pl.BlockSpec((1, tm, w), lambda i, j: (i, j, 0))
    seq_t = pl.BlockSpec((1, BR_WIDTH, tm), lambda i, j: (i, 0, j))
    mem_spec = lambda a: pl.BlockSpec((1,) + a.shape[1:], lambda i, j: (i, 0, 0))
    bf = lambda w: jax.ShapeDtypeStruct((n, t, w), BF16)
    bf_t = jax.ShapeDtypeStruct((n, BR_WIDTH, t), BF16)
    fl_t = jax.ShapeDtypeStruct((n, BR_WIDTH, t), F32)
    return pl.pallas_call(
        functools.partial(_inproj_kernel, tm=tm),
        grid=(n, t // tm),
        in_specs=[seq(d)] + [_const_spec(c) for c in consts] + [mem_spec(mk16), mem_spec(mv16)],
        out_specs=[seq(3 * BR_WIDTH), seq(BR_WIDTH), seq(BR_WIDTH), seq_t, seq(BR_WIDTH), seq(BR_WIDTH)] + [seq_t] * 5
        + [pl.BlockSpec((1, 2, BR_WIDTH), lambda i, j: (i, 0, 0))],
        out_shape=[bf(3 * BR_WIDTH), bf(BR_WIDTH), bf(BR_WIDTH), bf_t, bf(BR_WIDTH), bf(BR_WIDTH), bf_t] + [fl_t] * 4
        + [jax.ShapeDtypeStruct((n, 2, BR_WIDTH), F32)],
        scratch_shapes=[pltpu.VMEM((8, BR_WIDTH), F32)],
        compiler_params=_params("arbitrary", "arbitrary"),
        name="in_proj",
    )(x, *map(_operand, consts), mk16, mv16)


def _sb_kernel(q_ref, k_ref, v_ref, uu_ref, eye_ref, o_ref, qm_ref, carry_ref, acc_ref, *, tb):
    qi = pl.program_id(1)
    q = q_ref[0]
    lane = _lane_iota(q.shape)
    for h in range(N_GROUPS):
        in_head = (lane >= HEAD_DIM * h) & (lane < HEAD_DIM * (h + 1))
        qm_ref[h * tb:(h + 1) * tb] = jnp.where(in_head, q, jnp.zeros_like(q))
    strictly_before = _row_iota((tb, tb)) < _lane_iota((tb, tb))
    heads = range(N_GROUPS)

    def block(kb, first):
        start = pl.multiple_of(kb * tb, tb)
        z_all = _dot_nt(k_ref[0, pl.ds(start, tb), :], qm_ref[...])
        lbs, l1ms = [], []
        for h in heads:
            lb, l1m = _log2_sigmoid_pair(z_all[:, h * tb:(h + 1) * tb])
            lbs.append(lb)
            l1ms.append(jnp.where(strictly_before, l1m, 0.0) if first else l1m)
        hi, lo = _split_bf16(jnp.concatenate(l1ms, axis=1))
        between = _dot(uu_ref[...], jnp.concatenate([hi, lo], axis=0))
        for h in heads:
            total = lbs[h] + between[:, h * tb:(h + 1) * tb]
            if not first:
                total = total + carry_ref[h:h + 1]
            a = jnp.exp2(total)
            if first:
                a = jnp.where(strictly_before, a, 0.0)
            pv = _dot(v_ref[0, HEAD_DIM * h:HEAD_DIM * (h + 1), pl.ds(start, tb)], a.astype(BF16))
            block_sum = jnp.sum(l1ms[h], axis=0, keepdims=True)
            if first:
                acc_ref[h] = pv
                carry_ref[h:h + 1] = block_sum
            else:
                acc_ref[h] += pv
                carry_ref[h:h + 1] += block_sum

    block(qi, True)

    def cond(state):
        i, live = state
        return (i < qi) & (live > SB_UNDERFLOW_LOG2)

    def body(state):
        i, _ = state
        block(qi - 1 - i, False)
        return i + 1, jnp.max(carry_ref[...])

    lax.while_loop(cond, body, (jnp.int32(0), jnp.max(carry_ref[...])))
    yt = jnp.concatenate([acc_ref[h] for h in heads], axis=0)
    o_ref[0] = _dot_nt(eye_ref[...], yt.astype(BF16)).astype(BF16)


def _sb_attention(q, k, v, tb):
    n, t, w = q.shape
    i = np.arange(tb)
    later = (i[None, :] > i[:, None])
    uu = jnp.asarray(np.concatenate([later, later], axis=1), dtype=BF16)
    eye = jnp.asarray(np.eye(tb), dtype=BF16)
    return pl.pallas_call(
        functools.partial(_sb_kernel, tb=tb),
        grid=(n, t // tb),
        in_specs=[pl.BlockSpec((1, tb, w), lambda i, j: (i, j, 0)),
                  pl.BlockSpec((1, t, w), lambda i, j: (i, 0, 0)),
                  pl.BlockSpec((1, w, t), lambda i, j: (i, 0, 0)),
                  _full_spec(uu.shape), _full_spec(eye.shape)],
        out_specs=pl.BlockSpec((1, tb, w), lambda i, j: (i, j, 0)),
        out_shape=jax.ShapeDtypeStruct((n, t, w), BF16),
        scratch_shapes=[pltpu.VMEM((N_GROUPS * tb, w), BF16), pltpu.VMEM((N_GROUPS, tb), F32),
                        pltpu.VMEM((N_GROUPS, HEAD_DIM, tb), F32)],
        compiler_params=_params("arbitrary", "arbitrary"),
        name="sb_attn",
    )(q, k, v, uu, eye)


def _diff_kernel(q_ref, k_ref, v_ref, lp_ref, og_ref, bd64_ref, eye_ref, o_ref,
                 qm_ref, bias_ref, m_ref, l_ref, acc_ref, *, tb, lam_init):
    qi = pl.program_id(1)
    slopes = _alibi_slopes_log2()
    rel = _row_iota((tb, tb)) - _lane_iota((tb, tb))

    @pl.when(qi == 0)
    def _():
        for h in range(N_GROUPS):
            bias_ref[h] = slopes[h] * rel.astype(F32)

    q = q_ref[0]
    lane = _lane_iota(q.shape)
    for c in range(2):
        for h in range(N_GROUPS):
            lo = HEAD_DIM * h + DIFF_SUB * c
            ch = c * N_GROUPS + h
            qm_ref[ch * tb:(ch + 1) * tb] = jnp.where((lane >= lo) & (lane < lo + DIFF_SUB), q, jnp.zeros_like(q))

    def scores(kb):
        start = pl.multiple_of(kb * tb, tb)
        return _dot_nt(k_ref[0, pl.ds(start, tb), :], qm_ref[...])

    def block(kb, score_tile, first):
        start = pl.multiple_of(kb * tb, tb)
        off = ((kb - qi) * tb).astype(F32)
        probs, alphas = [], []
        for c in range(2):
            for h in range(N_GROUPS):
                ch = c * N_GROUPS + h
                s = score_tile(ch) + bias_ref[h]
                shift = slopes[h] * off
                if first:
                    s = jnp.where(rel <= 0, s, NEG_BIG)
                    m_new = jnp.max(s, axis=0, keepdims=True) + shift
                    p = jnp.exp2(s - (m_new - shift))
                    l_ref[ch:ch + 1] = jnp.sum(p, axis=0, keepdims=True)
                    alphas.append(None)
                else:
                    m_old = m_ref[ch:ch + 1]
                    m_new = jnp.maximum(m_old, jnp.max(s, axis=0, keepdims=True) + shift)
                    alpha = jnp.exp2(m_old - m_new)
                    p = jnp.exp2(s - (m_new - shift))
                    l_ref[ch:ch + 1] = alpha * l_ref[ch:ch + 1] + jnp.sum(p, axis=0, keepdims=True)
                    alphas.append(alpha)
                m_ref[ch:ch + 1] = m_new
                probs.append(p.astype(BF16))
        for h in range(N_GROUPS):
            vh = v_ref[0, HEAD_DIM * h:HEAD_DIM * (h + 1), pl.ds(start, tb)]
            pv = _dot(vh, jnp.concatenate([probs[h], probs[N_GROUPS + h]], axis=1))
            for c in range(2):
                ch = c * N_GROUPS + h
                new = pv[:, c * tb:(c + 1) * tb]
                acc_ref[ch] = new if first else alphas[ch] * acc_ref[ch] + new

    def run_block(kb, first):
        s_all = scores(kb)
        block(kb, lambda ch: s_all[:, ch * tb:(ch + 1) * tb], first)

    run_block(qi, True)

    def body(i, _):
        run_block(qi - 1 - i, False)
        return 0

    lax.fori_loop(0, qi, body, 0)
    lam = _lam_value(lp_ref[...], lam_init)
    ys = []
    for h in range(N_GROUPS):
        y0 = acc_ref[h] / l_ref[h:h + 1]
        y1 = acc_ref[N_GROUPS + h] / l_ref[N_GROUPS + h:N_GROUPS + h + 1]
        ys.append(y0 - lam * y1)
    yt = _group_rms_t(jnp.concatenate(ys, axis=0), og_ref[...], bd64_ref[...], HEAD_DIM) * (1.0 - lam_init)
    o_ref[0] = _dot_nt(eye_ref[...], yt.astype(BF16)).astype(BF16)


def _diff_attention(q, k, v, lp, og_col, bd64, lam_init, tb):
    n, t, w = q.shape
    maps = 2 * N_GROUPS
    eye = jnp.asarray(np.eye(tb), dtype=BF16)
    return pl.pallas_call(
        functools.partial(_diff_kernel, tb=tb, lam_init=lam_init),
        grid=(n, t // tb),
        in_specs=[pl.BlockSpec((1, tb, w), lambda i, j: (i, j, 0)),
                  pl.BlockSpec((1, t, w), lambda i, j: (i, 0, 0)),
                  pl.BlockSpec((1, w, t), lambda i, j: (i, 0, 0)),
                  _full_spec(lp.shape), _full_spec(og_col.shape), _full_spec(bd64.shape), _full_spec(eye.shape)],
        out_specs=pl.BlockSpec((1, tb, w), lambda i, j: (i, j, 0)),
        out_shape=jax.ShapeDtypeStruct((n, t, w), BF16),
        scratch_shapes=[pltpu.VMEM((maps * tb, w), BF16), pltpu.VMEM((N_GROUPS, tb, tb), F32),
                        pltpu.VMEM((maps, tb), F32), pltpu.VMEM((maps, tb), F32),
                        pltpu.VMEM((maps, HEAD_DIM, tb), F32)],
        compiler_params=_params("arbitrary", "arbitrary"),
        name="diff_attn",
    )(q, k, v, lp, og_col, bd64, eye)


def _merge_kernel(x_ref, yabm_ref, yc_ref, yd_ref, g1_ref, wg_ref, bg_ref, wb_ref, wo_ref, o_ref):
    x = x_ref[0]
    hb = _rms(x, g1_ref[...]).astype(BF16)
    ys = [yabm_ref[0, :, 0:BR_WIDTH], yabm_ref[0, :, BR_WIDTH:2 * BR_WIDTH], yc_ref[0], yd_ref[0],
          yabm_ref[0, :, 2 * BR_WIDTH:]]
    acc = None
    for b in range(5):
        gate = jax.nn.sigmoid(_dot(hb, wg_ref[:, b * D_MODEL:(b + 1) * D_MODEL]) + bg_ref[:, b * D_MODEL:(b + 1) * D_MODEL])
        term = gate * _dot(ys[b], wb_ref[b])
        acc = term if acc is None else acc + term
    o_ref[0] = x + _dot(acc.astype(BF16), wo_ref[...])


def _merge(x, yabm, yc, yd, p, tm):
    n, t, d = x.shape
    consts = [p["norm1_g"], p["w_gate"], p["b_gate"], p["w_branch"], p["w_o"]]
    seq = lambda w: pl.BlockSpec((1, tm, w), lambda i, j: (i, j, 0))
    return pl.pallas_call(
        _merge_kernel,
        grid=(n, t // tm),
        in_specs=[seq(d), seq(3 * BR_WIDTH), seq(BR_WIDTH), seq(BR_WIDTH)] + [_const_spec(c) for c in consts],
        out_specs=seq(d),
        out_shape=jax.ShapeDtypeStruct((n, t, d), F32),
        compiler_params=_params("arbitrary", "arbitrary"),
        name="merge",
    )(x, yabm, yc, yd, *map(_operand, consts))


MXU_WIDTH = 256
FFN_CHUNKS = ((0, 6 * MXU_WIDTH), (6 * MXU_WIDTH, D_FF - 6 * MXU_WIDTH))


def _ffn_kernel(x_ref, g2_ref, wup_ref, cw_ref, cb_ref, wdn_ref, o_ref, st_ref, carry_ref, *, tm):
    t = pl.program_id(1)
    x = x_ref[0]
    hb = _rms(x, g2_ref[...]).astype(BF16)

    @pl.when(t == 0)
    def _():
        carry_ref[...] = jnp.zeros_like(carry_ref)

    def conv(lo, width):
        cols = slice(lo, lo + width)
        row = _row_iota((tm, width))
        u = _dot(hb, wup_ref[:, cols])
        prev = carry_ref[:, cols]
        p1, p2 = prev[7:8], prev[6:7]
        s1 = jnp.where(row == 0, p1, pltpu.roll(u, 1, axis=0))
        s2 = jnp.where(row == 0, p2, jnp.where(row == 1, p1, pltpu.roll(u, 2, axis=0)))
        carry_ref[:, cols] = u[tm - 8:tm]
        st_ref[0, :, cols] = u[tm - 2:tm]
        return cw_ref[0:1, cols] * s2 + cw_ref[1:2, cols] * s1 + cw_ref[2:3, cols] * u + cb_ref[:, cols]

    acc = x
    for lo, width in FFN_CHUNKS:
        gate = conv(lo, width)
        val = conv(D_FF + lo, width)
        act = (gate * jax.nn.sigmoid(gate) * val).astype(BF16)
        acc = acc + _dot(act, wdn_ref[lo:lo + width, :])
    o_ref[0] = acc


def _ffn(x, p, tm):
    n, t, d = x.shape
    consts = [p["norm2_g"], p["w_up"], p["conv_ffn_w"], p["conv_ffn_b"], p["w_down"]]
    seq = pl.BlockSpec((1, tm, d), lambda i, j: (i, j, 0))
    return pl.pallas_call(
        functools.partial(_ffn_kernel, tm=tm),
        grid=(n, t // tm),
        in_specs=[seq] + [_const_spec(c) for c in consts],
        out_specs=[seq, pl.BlockSpec((1, 2, 2 * D_FF), lambda i, j: (i, 0, 0))],
        out_shape=[jax.ShapeDtypeStruct((n, t, d), F32), jax.ShapeDtypeStruct((n, 2, 2 * D_FF), F32)],
        scratch_shapes=[pltpu.VMEM((8, 2 * D_FF), F32)],
        compiler_params=_params("arbitrary", "arbitrary"),
        name="conv_ffn",
    )(x, *map(_operand, consts))


def _ffn_step_kernel(x_ref, p0_ref, p1_ref, g2_ref, wup_ref, cw_ref, cb_ref, wdn_ref, o_ref, u_ref):
    x = x_ref[...]
    hb = _rms(x, g2_ref[...]).astype(BF16)

    def conv(lo, width):
        cols = slice(lo, lo + width)
        u = _dot(hb, wup_ref[:, cols])
        u_ref[:, cols] = u
        return (cw_ref[0:1, cols] * p0_ref[:, cols] + cw_ref[1:2, cols] * p1_ref[:, cols]
                + cw_ref[2:3, cols] * u + cb_ref[:, cols])

    acc = x
    for lo, width in FFN_CHUNKS:
        gate = conv(lo, width)
        val = conv(D_FF + lo, width)
        act = (gate * jax.nn.sigmoid(gate) * val).astype(BF16)
        acc = acc + _dot(act, wdn_ref[lo:lo + width, :])
    o_ref[...] = acc


def _ffn_step(x, prev0, prev1, p):
    m, d = x.shape
    consts = [p["norm2_g"], p["w_up"], p["conv_ffn_w"], p["conv_ffn_b"], p["w_down"]]
    args = [x, prev0, prev1] + consts
    return pl.pallas_call(
        _ffn_step_kernel,
        grid=(1,),
        in_specs=[_const_spec(a) for a in args],
        out_specs=[_full_spec((m, d)), _full_spec((m, 2 * D_FF))],
        out_shape=[jax.ShapeDtypeStruct((m, d), F32), jax.ShapeDtypeStruct((m, 2 * D_FF), F32)],
        compiler_params=_params("arbitrary"),
        name="conv_ffn_step",
    )(*map(_operand, args))


def _inproj_step_kernel(x_ref, c0_ref, c1_ref, g1_ref, win_ref, cw_ref, lng_ref, lnb_ref, ws_ref, gb_ref,
                        dqg_ref, dkg_ref, mqg_ref, bd32_ref, bd64_ref,
                        yab_ref, q_ref, ksb_ref, vsb_ref, kdf_ref, vdf_ref, vn_ref, ach_ref):
    hb = _rms(x_ref[...], g1_ref[...]).astype(BF16)

    def proj(lo, hi):
        return _dot(hb, win_ref[:, lo:hi])

    pa = proj(0, 3 * BR_WIDTH)
    ach = pa[:, BR_WIDTH:2 * BR_WIDTH] * pa[:, 2 * BR_WIDTH:]
    cw = cw_ref[...]
    yab_ref[:, 0:BR_WIDTH] = (pa[:, :BR_WIDTH] * (cw[0:1] * c0_ref[...] + cw[1:2] * c1_ref[...] + cw[2:3] * ach)).astype(BF16)
    ach_ref[...] = ach

    pg = proj(3 * BR_WIDTH, 5 * BR_WIDTH)
    vn = _layer_norm(pg[:, BR_WIDTH:], lng_ref[...], lnb_ref[...])
    vn_ref[...] = vn
    lane = _lane_iota((1, BR_WIDTH))
    w00 = jnp.zeros((1, BR_WIDTH), F32)
    for g in range(N_GROUPS):
        w00 = jnp.where((lane >= HEAD_DIM * g) & (lane < HEAD_DIM * (g + 1)), ws_ref[g, 0:1, 0:1], w00)
    yab_ref[:, BR_WIDTH:] = (pg[:, :BR_WIDTH] * (w00 * vn + gb_ref[0:1, :])).astype(BF16)

    ps = proj(5 * BR_WIDTH, 8 * BR_WIDTH)
    q_ref[:, 0:BR_WIDTH] = (ps[:, :BR_WIDTH] * (HEAD_DIM ** -0.5)).astype(BF16)
    ksb_ref[...] = ps[:, BR_WIDTH:2 * BR_WIDTH]
    vsb_ref[...] = ps[:, 2 * BR_WIDTH:]

    pd = proj(8 * BR_WIDTH, 11 * BR_WIDTH)
    bd32 = bd32_ref[...]
    qd = _group_rms(pd[:, :BR_WIDTH], dqg_ref[...], bd32, DIFF_SUB)
    q_ref[:, BR_WIDTH:2 * BR_WIDTH] = (qd * (DIFF_SUB ** -0.5 * LOG2E)).astype(BF16)
    kdf_ref[...] = _group_rms(pd[:, BR_WIDTH:2 * BR_WIDTH], dkg_ref[...], bd32, DIFF_SUB)
    vdf_ref[...] = pd[:, 2 * BR_WIDTH:]

    qm = _group_rms(proj(11 * BR_WIDTH, 12 * BR_WIDTH), mqg_ref[...], bd64_ref[...], HEAD_DIM)
    q_ref[:, 2 * BR_WIDTH:] = (qm * (HEAD_DIM ** -0.5 * LOG2E)).astype(BF16)


def _inproj_step(x, c0, c1, p):
    m, d = x.shape
    consts = [p["norm1_g"], p["w_in"], p["conv_a_w"], p["gmlp_ln_g"], p["gmlp_ln_b"], p["gmlp_ws"],
              p["gmlp_b_full"], p["diff_qn_g"], p["diff_kn_g"], p["mem_qn_g"], p["bd32"], p["bd64"]]
    args = [x, c0, c1] + consts
    fl = jax.ShapeDtypeStruct((m, BR_WIDTH), F32)
    out_shape = [jax.ShapeDtypeStruct((m, 2 * BR_WIDTH), BF16), jax.ShapeDtypeStruct((m, 3 * BR_WIDTH), BF16)] + [fl] * 6
    return pl.pallas_call(
        _inproj_step_kernel,
        grid=(1,),
        in_specs=[_const_spec(a) for a in args],
        out_specs=[_full_spec(s.shape) for s in out_shape],
        out_shape=out_shape,
        compiler_params=_params("arbitrary"),
        name="in_proj_step",
    )(*map(_operand, args))


def _decode_kernel(pt_ref, q_ref, kn_ref, vn_ref, mk_ref, mv_ref, lp_ref, og_ref, bd64_ref, ll_ref, ones_ref,
                   ksb_hbm, vsb_hbm, kdf_hbm, vdf_hbm, o_ref, buf_ref, sem_ref, *, layer, n_samples, n_pages, page,
                   lam_init):
    step = pl.program_id(0)
    pools = (ksb_hbm, vsb_hbm, kdf_hbm, vdf_hbm)

    def page_copy(sample, slot, c, j):
        return pltpu.make_async_copy(pools[c].at[layer, pt_ref[sample, j]], buf_ref.at[slot, c, j], sem_ref.at[slot])

    def for_each_page(sample, slot, fn):
        for c in range(len(pools)):
            for j in range(n_pages):
                fn(page_copy(sample, slot, c, j))

    ahead = DECODE_SLOTS - 1

    @pl.when(step == 0)
    def _():
        for s in range(min(ahead, n_samples)):
            for_each_page(s, s, lambda cp: cp.start())

    @pl.when(step + ahead < n_samples)
    def _():
        for_each_page(step + ahead, lax.rem(step + ahead, DECODE_SLOTS), lambda cp: cp.start())

    slot = lax.rem(step, DECODE_SLOTS)
    for_each_page(step, slot, lambda cp: cp.wait())
    ksb, vsb, kdf, vdf = ([buf_ref.at[slot, c, j] for j in range(n_pages)] for c in range(len(pools)))
    rows = 2 * N_GROUPS
    n_sub = HEAD_DIM // DIFF_SUB
    ones = ones_ref[...]

    def q_column(col):
        qrow = jnp.broadcast_to(q_ref[0, :, col * BR_WIDTH:(col + 1) * BR_WIDTH].astype(F32), (BR_WIDTH, BR_WIDTH))
        diag = jnp.where(_row_iota(qrow.shape) == _lane_iota(qrow.shape), qrow, 0.0).astype(BF16)
        return _dot(diag, ones)

    def group_scores(kt, qcol):
        return jnp.sum((kt * qcol).reshape(rows, DIFF_SUB, kt.shape[1]), axis=1)

    def pair_sum(z):
        r = z.shape[0]
        even = (_row_iota(z.shape) & 1) == 0
        return z + jnp.where(even, pltpu.roll(z, r - 1, axis=0), pltpu.roll(z, 1, axis=0))

    def head_rows(w):
        return jnp.concatenate([jnp.broadcast_to(w[n_sub * h:n_sub * h + 1], (HEAD_DIM, w.shape[1]))
                                for h in range(N_GROUPS)], axis=0)

    def reduce_positions(acc):
        hi, lo = _split_bf16(acc)
        ones_row = jnp.ones((rows, acc.shape[1]), BF16)
        return _dot_nt(ones_row, hi) + _dot_nt(ones_row, lo)

    qc = q_column(0)
    z = jnp.concatenate([pair_sum(group_scores(ksb[pg][...], qc)) for pg in range(n_pages)], axis=0)
    lb, l1m = _log_sigmoid_pair(z)
    hi, lo = _split_bf16(l1m)
    hl = jnp.concatenate([hi, lo], axis=1)
    within = _dot(hl, ll_ref[...])
    total = _dot(hl, ones)
    carry = jnp.zeros((rows, page), F32)
    carries = [None] * n_pages
    for pg in reversed(range(n_pages)):
        carries[pg] = carry
        carry = carry + total[rows * pg:rows * (pg + 1)]
    a = jnp.exp(lb + within + jnp.concatenate(carries, axis=0))
    acc = jnp.zeros((BR_WIDTH, page), F32)
    for pg in range(n_pages):
        acc = acc + head_rows(a[rows * pg:rows * (pg + 1)]) * vsb[pg][...]
    y_c = reduce_positions(acc)[0:1]

    qc = q_column(1)
    s = jnp.concatenate([group_scores(kdf[pg][...], qc) for pg in range(n_pages)], axis=0)
    slopes = _alibi_slopes_log2()
    all_rows = _row_iota((rows * n_pages, page))
    head = _div_pow2(_rem_pow2(all_rows, rows), n_sub)
    sl = jnp.full(all_rows.shape, slopes[0], F32)
    for h in range(1, N_GROUPS):
        sl = jnp.where(head == h, slopes[h], sl)
    past = n_pages * page
    k_pos = _div_pow2(all_rows, rows) * page + _lane_iota(all_rows.shape)
    s = s + sl * (k_pos - past).astype(F32)
    row8 = _row_iota((rows, BR_WIDTH))
    group_lanes = _div_pow2(_lane_iota((rows, BR_WIDTH)), DIFF_SUB) == row8
    q8 = jnp.broadcast_to(q_ref[0, :, BR_WIDTH:2 * BR_WIDTH].astype(F32), (rows, BR_WIDTH))
    kn8 = jnp.broadcast_to(kn_ref[0], (rows, BR_WIDTH))
    s_new = jnp.sum(jnp.where(group_lanes, q8 * kn8, 0.0), axis=1, keepdims=True)
    m8 = s[0:rows]
    for pg in range(1, n_pages):
        m8 = jnp.maximum(m8, s[rows * pg:rows * (pg + 1)])
    m = jnp.maximum(jnp.max(m8, axis=1, keepdims=True), s_new)
    p_new = jnp.exp2(s_new - m)
    m_all = jnp.concatenate([jnp.broadcast_to(m, (rows, page))] * n_pages, axis=0)
    p = jnp.exp2(s - m_all)
    l8 = p[0:rows]
    for pg in range(1, n_pages):
        l8 = l8 + p[rows * pg:rows * (pg + 1)]
    l = jnp.sum(l8, axis=1, keepdims=True) + p_new
    lam = _lam_value(lp_ref[...], lam_init)
    first_map = (_row_iota((rows, 1)) & 1) == 0
    coef = jnp.where(first_map, 1.0, -lam) / l
    w = pair_sum(p * jnp.concatenate([jnp.broadcast_to(coef, (rows, page))] * n_pages, axis=0))
    acc = jnp.zeros((BR_WIDTH, page), F32)
    for pg in range(n_pages):
        acc = acc + head_rows(w[rows * pg:rows * (pg + 1)]) * vdf[pg][...]
    w_new = pair_sum(p_new * coef)
    head_lanes = (_div_pow2(_lane_iota((rows, BR_WIDTH)), HEAD_DIM) * n_sub) == row8
    vn8 = jnp.broadcast_to(vn_ref[0], (rows, BR_WIDTH))
    y_new = jnp.sum(jnp.where(head_lanes, w_new * vn8, 0.0), axis=0, keepdims=True)
    y_d = reduce_positions(acc)[0:1] + y_new
    y_d = _group_rms(jnp.broadcast_to(y_d, (rows, BR_WIDTH)), og_ref[...], bd64_ref[...], HEAD_DIM)[0:1]
    y_d = y_d * (1.0 - lam_init)

    qc = q_column(2)
    n_mem = mk_ref.shape[2]
    qc = jnp.concatenate([qc] * (n_mem // page), axis=1)
    s = pair_sum(group_scores(mk_ref[0], qc))
    p = jnp.exp2(s - jnp.max(s, axis=1, keepdims=True))
    p = p / jnp.sum(p, axis=1, keepdims=True)
    y_m = reduce_positions(head_rows(p) * mv_ref[0])[0:1]

    o_ref[0, :, 0:BR_WIDTH] = y_c.astype(BF16)
    o_ref[0, :, BR_WIDTH:2 * BR_WIDTH] = y_d.astype(BF16)
    o_ref[0, :, 2 * BR_WIDTH:] = y_m.astype(BF16)


def _decode_attention(layer, page_table, q, k_new, v_new, caches, mem_k, mem_v, p, lam_init):
    b = q.shape[0]
    n_pages = page_table.shape[1]
    page = caches[0].shape[3]
    ll = _suffix_ones(page)
    row_spec = lambda w: pl.BlockSpec((1, 1, w), lambda i, pt: (i, 0, 0))
    mem_spec = pl.BlockSpec((None, 1) + mem_k.shape[2:], lambda i, pt: (layer, i, 0, 0))
    const = lambda a: pl.BlockSpec(a.shape, lambda i, pt: (0,) * a.ndim)
    consts = [p["diff_lambda"], p["diff_out_g"], p["bd64"], ll, jnp.ones((BR_WIDTH, page), BF16)]
    grid_spec = pltpu.PrefetchScalarGridSpec(
        num_scalar_prefetch=1,
        grid=(b,),
        in_specs=[row_spec(3 * BR_WIDTH), row_spec(BR_WIDTH), row_spec(BR_WIDTH), mem_spec, mem_spec]
        + [const(c) for c in consts] + [pl.BlockSpec(memory_space=pl.ANY)] * len(caches),
        out_specs=row_spec(3 * BR_WIDTH),
        scratch_shapes=[pltpu.VMEM((DECODE_SLOTS, len(caches), n_pages, BR_WIDTH, page), F32),
                        pltpu.SemaphoreType.DMA((DECODE_SLOTS,))],
    )
    return pl.pallas_call(
        functools.partial(_decode_kernel, layer=layer, n_samples=b, n_pages=n_pages, page=page, lam_init=lam_init),
        grid_spec=grid_spec,
        out_shape=jax.ShapeDtypeStruct((b, 1, 3 * BR_WIDTH), BF16),
        compiler_params=_params("arbitrary"),
        name="decode_attn",
    )(page_table, q, k_new, v_new, mem_k, mem_v, *consts, *caches)


PROMPT_TILE = 512
DECODE_SLOTS = 3
ATTN_BLOCK = 256


def _cast_weights(w):
    big = {k: w[k].astype(BF16) for k in ("w_in", "w_gate", "w_branch", "w_o", "w_up", "w_down")}
    w_in = big["w_in"]
    big["w_in_tok"] = jnp.concatenate([w_in[:, :, :7 * BR_WIDTH], w_in[:, :, 8 * BR_WIDTH:10 * BR_WIDTH],
                                       w_in[:, :, 11 * BR_WIDTH:]], axis=2)
    return big


def _layer_params(l, w, big):
    row = lambda a: a[l].reshape(1, -1).astype(F32)
    tile4 = lambda a: jnp.tile(a[l].astype(F32), N_GROUPS).reshape(1, -1)
    col4 = lambda a: jnp.tile(a[l].astype(F32), N_GROUPS).reshape(-1, 1)
    w_in = big["w_in"][l]
    return dict(
        norm1_g=row(w["norm1_g"]), w_in=_Layered(big["w_in"], l), conv_a_w=w["conv_a_w"][l].astype(F32),
        w_sb_kv_t=w_in[:, 6 * BR_WIDTH:8 * BR_WIDTH].T, w_df_kv_t=w_in[:, 9 * BR_WIDTH:11 * BR_WIDTH].T,
        w_in_tok=_Layered(big["w_in_tok"], l),
        diff_kn_g_col=col4(w["diff_kn_g"]), mem_kn_g_col=col4(w["mem_kn_g"]), diff_out_g_col=col4(w["diff_out_g"]),
        w_mem_kv_t=w["w_mem_kv"][l].astype(BF16).T,
        gmlp_ln_g=row(w["gmlp_ln_g"]), gmlp_ln_b=row(w["gmlp_ln_b"]), gmlp_ws=w["gmlp_ws"][l].astype(F32),
        gmlp_b_full=jnp.repeat(w["gmlp_b"][l].astype(F32).T, HEAD_DIM, axis=1),
        diff_qn_g=tile4(w["diff_qn_g"]), diff_kn_g=tile4(w["diff_kn_g"]), diff_out_g=tile4(w["diff_out_g"]),
        diff_lambda=w["diff_lambda"][l].astype(F32),
        mem_norm_g=row(w["mem_norm_g"]), mem_qn_g=tile4(w["mem_qn_g"]),
        w_branch=_Layered(big["w_branch"], l), w_gate=_Layered(big["w_gate"], l), b_gate=row(w["b_gate"]),
        w_o=_Layered(big["w_o"], l), norm2_g=row(w["norm2_g"]), w_up=_Layered(big["w_up"], l),
        conv_ffn_w=w["conv_ffn_w"][l].astype(F32), conv_ffn_b=row(w["conv_ffn_b"]),
        w_down=_Layered(big["w_down"], l),
        bd32=_block_ones(DIFF_SUB), bd64=_block_ones(HEAD_DIM),
    )


def _prompt_layer(x, mem, p, lam_init):
    n, t, _ = x.shape
    tm = min(PROMPT_TILE, t)
    tb = min(ATTN_BLOCK, t)
    mk, mv, mk16, mv16 = _mem_kv(mem, p["mem_norm_g"], p["w_mem_kv_t"], p["mem_kn_g_col"], p["bd64"])
    (yabm, qsb, ksb16, vsb16, qdf, kdf16, vdf16, ksb, vsb, kdf, vdf, ca) = _inproj(x, p, mk16, mv16, tm)
    yc = _sb_attention(qsb, ksb16, vsb16, tb)
    yd = _diff_attention(qdf, kdf16, vdf16, p["diff_lambda"], p["diff_out_g_col"], p["bd64"], lam_init, tb)
    x1 = _merge(x, yabm, yc, yd, p, tm)
    x2, cf = _ffn(x1, p, tm)
    return x2, ca, ksb, vsb, kdf, vdf, mk, mv, cf


def _sample_layer(l, x, conv_a, conv_ffn, page_table, caches, mem_k, mem_v, p, lam_init):
    b = x.shape[0]
    yab, q, ksb, vsb, kdf, vdf, vn, ach = _inproj_step(x, conv_a[:, 0], conv_a[:, 1], p)
    row3 = lambda a: a.reshape(b, 1, -1)
    ycdm = _decode_attention(l, page_table, row3(q), row3(kdf), row3(vdf), caches, mem_k, mem_v, p, lam_init)
    yabm = jnp.concatenate([yab, ycdm[:, 0, 2 * BR_WIDTH:]], axis=1)
    x1 = _merge(x[None], yabm[None], ycdm[None, :, 0, 0:BR_WIDTH], ycdm[None, :, 0, BR_WIDTH:2 * BR_WIDTH], p, b)[0]
    x2, u = _ffn_step(x1, conv_ffn[:, 0], conv_ffn[:, 1], p)
    ca_new = jnp.stack([conv_a[:, 1], ach], axis=1)
    cf_new = jnp.stack([conv_ffn[:, 1], u], axis=1)
    return x2, ca_new, vn, ksb, vsb, kdf, vdf, cf_new


def kernel(x_prompt, x_sample, state_conv_a, cache_k_sb, cache_v_sb, cache_k_diff, cache_v_diff, cache_mem_k,
           cache_mem_v, state_conv_ffn, page_table, mem_prompt, norm1_g, w_in, conv_a_w, gmlp_ln_g, gmlp_ln_b,
           gmlp_ws, gmlp_b, diff_qn_g, diff_kn_g, diff_lambda, diff_out_g, mem_norm_g, w_mem_kv, mem_qn_g,
           mem_kn_g, w_branch, w_gate, b_gate, w_o, norm2_g, w_up, conv_ffn_w, conv_ffn_b, w_down):
    w = dict(norm1_g=norm1_g, w_in=w_in, conv_a_w=conv_a_w, gmlp_ln_g=gmlp_ln_g, gmlp_ln_b=gmlp_ln_b,
             gmlp_ws=gmlp_ws, gmlp_b=gmlp_b, diff_qn_g=diff_qn_g, diff_kn_g=diff_kn_g, diff_lambda=diff_lambda,
             diff_out_g=diff_out_g, mem_norm_g=mem_norm_g, w_mem_kv=w_mem_kv, mem_qn_g=mem_qn_g,
             mem_kn_g=mem_kn_g, w_branch=w_branch, w_gate=w_gate, b_gate=b_gate, w_o=w_o, norm2_g=norm2_g,
             w_up=w_up, conv_ffn_w=conv_ffn_w, conv_ffn_b=conv_ffn_b, w_down=w_down)
    depth = w_in.shape[0]
    n_p, t_p, _ = x_prompt.shape
    n_s = x_sample.shape[0]
    flat = lambda c: jnp.transpose(c, (0, 1, 3, 4, 2)).reshape(c.shape[:2] + (BR_WIDTH, c.shape[2]))
    caches = [flat(cache_k_sb), flat(cache_v_sb), flat(cache_k_diff), flat(cache_v_diff)]
    mem_k, mem_v = flat(cache_mem_k), flat(cache_mem_v)
    xp, xs = x_prompt, x_sample[:, 0]
    outs_p, outs_s = [], []
    big = _cast_weights(w)
    for l in range(depth):
        p = _layer_params(l, w, big)
        lam_init = 0.8 - 0.6 * math.exp(-0.3 * l)
        xp, *rest_p = _prompt_layer(xp, mem_prompt, p, lam_init)
        outs_p.append(rest_p)
        xs, *rest_s = _sample_layer(l, xs, state_conv_a[l], state_conv_ffn[l], page_table, caches, mem_k, mem_v,
                                    p, lam_init)
        outs_s.append(rest_s)
    heads = lambda a: a.reshape(a.shape[:-1] + (N_GROUPS, HEAD_DIM))
    stack_p = lambda i: jnp.stack([o[i] for o in outs_p], axis=0)
    stack_s = lambda i: jnp.stack([o[i] for o in outs_s], axis=0)
    step = lambda a: a.reshape(depth, n_s, 1, -1)

    def heads_t(a):
        d0, n, _, t = a.shape
        return jnp.transpose(a.reshape(d0, n, N_GROUPS, HEAD_DIM, t), (0, 1, 4, 2, 3))

    return (xp, xs[:, None, :],
            stack_p(0), stack_s(0), step(stack_s(1)),
            heads_t(stack_p(1)), heads_t(stack_p(2)), heads(step(stack_s(2))), heads(step(stack_s(3))),
            heads_t(stack_p(3)), heads_t(stack_p(4)), heads(step(stack_s(4))), heads(step(stack_s(5))),
            heads_t(stack_p(5)), heads_t(stack_p(6)),
            stack_p(7), stack_s(6))
```

```python
import functools
import math
from typing import NamedTuple

import numpy as np
import jax
import jax.numpy as jnp
from jax import lax
from jax.experimental import pallas as pl
from jax.experimental.pallas import tpu as pltpu

D_MODEL = 1024
HEAD_DIM = 64
BR_WIDTH = 256
N_GROUPS = 4
DIFF_SUB = 32
CHUNK = 128
D_FF = 2816
EPS = 1e-6
NEG_BIG = -1e30
SB_UNDERFLOW_LOG2 = -150.0
LOG2E = 1.4426950408889634
VMEM_LIMIT_BYTES = 56 * 1024 * 1024

F32 = jnp.float32
BF16 = jnp.bfloat16


def _dot(a, b):
    return jnp.dot(a, b, preferred_element_type=F32)


def _dot_nt(a, b):
    return lax.dot_general(a, b, (((1,), (1,)), ((), ())), preferred_element_type=F32)


def _split_bf16(x):
    hi = x.astype(BF16)
    lo = (x - hi.astype(F32)).astype(BF16)
    return hi, lo


def _rms(x, g):
    ms = jnp.mean(x * x, axis=-1, keepdims=True)
    return x * lax.rsqrt(ms + EPS) * g


def _group_rms(z, g, ones_bd, group):
    ms = _dot((z * z).astype(BF16), ones_bd) * (1.0 / group)
    return z * lax.rsqrt(ms + EPS) * g


def _lane_iota(shape):
    return lax.broadcasted_iota(jnp.int32, shape, len(shape) - 1)


def _row_iota(shape):
    return lax.broadcasted_iota(jnp.int32, shape, len(shape) - 2)


def _div_pow2(x, d):
    assert d & (d - 1) == 0
    return lax.shift_right_logical(x, d.bit_length() - 1)


def _rem_pow2(x, d):
    assert d & (d - 1) == 0
    return x & (d - 1)


def _head_select(parts):
    lane = _lane_iota(parts[0].shape)
    out = parts[0]
    for h in range(1, N_GROUPS):
        out = jnp.where(lane >= HEAD_DIM * h, parts[h], out)
    return out


def _stack_heads(q, n_sub):
    lane = _lane_iota(q.shape)
    width = HEAD_DIM // n_sub
    zero = jnp.zeros_like(q)
    parts = []
    for c in range(n_sub):
        for h in range(N_GROUPS):
            lo = HEAD_DIM * h + width * c
            parts.append(jnp.where((lane >= lo) & (lane < lo + width), q, zero))
    return jnp.concatenate(parts, axis=0)


def _log_sigmoid_pair(z):
    lb = jnp.minimum(z, 0.0) - jnp.log(1.0 + jnp.exp(-jnp.abs(z)))
    return lb, lb - z


def _log2_sigmoid_pair(z2):
    lb = jnp.minimum(z2, 0.0) - jnp.log2(1.0 + jnp.exp2(-jnp.abs(z2)))
    return lb, lb - z2


def _lam_value(lp, lam_init):
    a = jnp.sum(lp[0:1] * lp[1:2], axis=1, keepdims=True)
    b = jnp.sum(lp[2:3] * lp[3:4], axis=1, keepdims=True)
    return jnp.exp(a) - jnp.exp(b) + lam_init


def _block_ones(group):
    i = np.arange(BR_WIDTH)
    return jnp.asarray((i[:, None] // group) == (i[None, :] // group), dtype=BF16)


def _suffix_ones(n):
    i = np.arange(n)
    l = (i[:, None] > i[None, :])
    return jnp.asarray(np.concatenate([l, l], axis=0), dtype=BF16)


def _alibi_slopes_log2():
    return [LOG2E * 2.0 ** (-8.0 * (h + 1) / N_GROUPS) for h in range(N_GROUPS)]


def _full_spec(shape):
    nd = len(shape)
    return pl.BlockSpec(shape, lambda *_: (0,) * nd, pipeline_mode=pl.Buffered(1))


class _Layered(NamedTuple):
    array: jax.Array
    layer: int

    @property
    def shape(self):
        return self.array.shape[1:]


def _const_spec(c):
    if isinstance(c, _Layered):
        rest = (0,) * len(c.shape)
        return pl.BlockSpec((None,) + c.shape, lambda *_: (c.layer,) + rest, pipeline_mode=pl.Buffered(1))
    return _full_spec(c.shape)


def _operand(c):
    return c.array if isinstance(c, _Layered) else c


def _params(*sem):
    return pltpu.CompilerParams(dimension_semantics=sem, vmem_limit_bytes=VMEM_LIMIT_BYTES)


def _group_rms_t(zt, g_col, ones_bd, group):
    hi, lo = _split_bf16(zt * zt)
    ms = (_dot(ones_bd, hi) + _dot(ones_bd, lo)) * (1.0 / group)
    return zt * lax.rsqrt(ms + EPS) * g_col


def _transpose_bf16(xt, eye):
    return _dot_nt(eye, xt)


def _memkv_kernel(mem_ref, g_ref, wt_ref, kg_ref, bd64_ref, eye_ref, k_ref, v_ref, k16_ref, v16_ref):
    hb = _rms(mem_ref[0], g_ref[...]).astype(BF16)
    kvt = _dot_nt(wt_ref[...], hb)
    k = _group_rms_t(kvt[:BR_WIDTH], kg_ref[...], bd64_ref[...], HEAD_DIM)
    v = kvt[BR_WIDTH:]
    k_ref[0] = k
    v_ref[0] = v
    k16_ref[0] = _transpose_bf16(k.astype(BF16), eye_ref[...]).astype(BF16)
    v16_ref[0] = v.astype(BF16)


def _mem_kv(mem, g, w, kg, bd64):
    n, m, d = mem.shape
    blk = pl.BlockSpec((1, BR_WIDTH, m), lambda i: (i, 0, 0))
    eye = jnp.asarray(np.eye(m), dtype=BF16)
    return pl.pallas_call(
        _memkv_kernel,
        grid=(n,),
        in_specs=[pl.BlockSpec((1, m, d), lambda i: (i, 0, 0)), _full_spec(g.shape), _full_spec(w.shape),
                  _full_spec(kg.shape), _full_spec(bd64.shape), _full_spec(eye.shape)],
        out_specs=[blk, blk, pl.BlockSpec((1, m, BR_WIDTH), lambda i: (i, 0, 0)), blk],
        out_shape=[jax.ShapeDtypeStruct((n, BR_WIDTH, m), F32)] * 2
        + [jax.ShapeDtypeStruct((n, m, BR_WIDTH), BF16), jax.ShapeDtypeStruct((n, BR_WIDTH, m), BF16)],
        compiler_params=_params("arbitrary"),
        name="mem_kv",
    )(mem, g, w, kg, bd64, eye)


def _layer_norm(x, g, b):
    mu = jnp.mean(x, axis=-1, keepdims=True)
    xc = x - mu
    return xc * lax.rsqrt(jnp.mean(xc * xc, axis=-1, keepdims=True) + EPS) * g + b


def _mem_attention(qm, mk, mvt, eye):
    r = qm.shape[0]
    st = _dot_nt(mk, _stack_heads(qm, 1))
    p = jnp.exp2(st - jnp.max(st, axis=0, keepdims=True))
    inv = 1.0 / jnp.sum(p, axis=0, keepdims=True)
    ot = _dot(mvt, p.astype(BF16))
    yt = jnp.concatenate([ot[HEAD_DIM * h:HEAD_DIM * (h + 1), h * r:(h + 1) * r] * inv[:, h * r:(h + 1) * r]
                          for h in range(N_GROUPS)], axis=0)
    return _transpose_bf16(yt.astype(BF16), eye)


def _inproj_kernel(x_ref, g1_ref, win_ref, wsbt_ref, wdft_ref, cw_ref, lng_ref, lnb_ref, ws_ref, gb_ref, dqg_ref,
                   dkg_ref, dkgr_ref, mqg_ref, bd32_ref, bd64_ref, eye_ref, mk_ref, mv_ref, *rest, tm, n_prev):
    prev_kv = rest[:4] if n_prev else ()
    (yabm_ref, qsb_ref, ksb16_ref, vsb16_ref, qdf_ref, kdf16_ref, vdf16_ref,
     ksb_ref, vsb_ref, kdf_ref, vdf_ref, ca_ref, carry_ref) = rest[len(prev_kv):]
    for prev_ref, out_ref in zip(prev_kv, (ksb_ref, vsb_ref, kdf_ref, vdf_ref)):
        for l in range(n_prev):
            out_ref[l, 0] = prev_ref[l, 0]
    t = pl.program_id(1)
    hb = _rms(x_ref[0], g1_ref[...]).astype(BF16)

    p_all = _dot(hb, win_ref[...])

    def proj(lo, hi):
        return p_all[:, lo:hi]

    pa = proj(0, 3 * BR_WIDTH)
    ach = pa[:, BR_WIDTH:2 * BR_WIDTH] * pa[:, 2 * BR_WIDTH:]

    @pl.when(t == 0)
    def _():
        carry_ref[...] = jnp.zeros_like(carry_ref)

    prev = carry_ref[...]
    p1, p2 = prev[7:8], prev[6:7]
    row = _row_iota(ach.shape)
    s1 = jnp.where(row == 0, p1, pltpu.roll(ach, 1, axis=0))
    s2 = jnp.where(row == 0, p2, jnp.where(row == 1, p1, pltpu.roll(ach, 2, axis=0)))
    cw = cw_ref[...]
    yabm_ref[0, :, 0:BR_WIDTH] = (pa[:, :BR_WIDTH] * (cw[0:1] * s2 + cw[1:2] * s1 + cw[2:3] * ach)).astype(BF16)
    carry_ref[...] = ach[tm - 8:tm]
    ca_ref[0] = ach[tm - 2:tm]

    pg = proj(3 * BR_WIDTH, 5 * BR_WIDTH)
    vn = _layer_norm(pg[:, BR_WIDTH:], lng_ref[...], lnb_ref[...])
    tril = _row_iota((CHUNK, CHUNK)) >= _lane_iota((CHUNK, CHUNK))
    wsm = [jnp.where(tril, ws_ref[g], 0.0).astype(BF16) for g in range(N_GROUPS)]
    gb = gb_ref[...]
    n_chunks = tm // CHUNK
    vc = jnp.concatenate([vn[c * CHUNK:(c + 1) * CHUNK] for c in range(n_chunks)], axis=1).astype(BF16)
    channel = _rem_pow2(_lane_iota(vc.shape), BR_WIDTH)
    mixed = _dot(wsm[0], vc)
    for g in range(1, N_GROUPS):
        mixed = jnp.where(channel >= HEAD_DIM * g, _dot(wsm[g], vc), mixed)
    for c in range(n_chunks):
        yabm_ref[0, c * CHUNK:(c + 1) * CHUNK, BR_WIDTH:2 * BR_WIDTH] = (
            pg[c * CHUNK:(c + 1) * CHUNK, :BR_WIDTH] * (mixed[:, c * BR_WIDTH:(c + 1) * BR_WIDTH] + gb)).astype(BF16)

    pq = proj(5 * BR_WIDTH, 10 * BR_WIDTH)
    qsb_ref[0] = (pq[:, :BR_WIDTH] * (HEAD_DIM ** -0.5 * LOG2E)).astype(BF16)
    kvt = _dot_nt(wsbt_ref[...], hb)
    ksb_ref[n_prev, 0] = kvt[:BR_WIDTH]
    vsb_ref[n_prev, 0] = kvt[BR_WIDTH:]
    ksb16_ref[0] = pq[:, BR_WIDTH:2 * BR_WIDTH].astype(BF16)
    vsb16_ref[0] = kvt[BR_WIDTH:].astype(BF16)

    bd32 = bd32_ref[...]
    qd = _group_rms(pq[:, 2 * BR_WIDTH:3 * BR_WIDTH], dqg_ref[...], bd32, DIFF_SUB)
    qdf_ref[0] = (qd * (DIFF_SUB ** -0.5 * LOG2E)).astype(BF16)
    kvt = _dot_nt(wdft_ref[...], hb)
    kd = _group_rms_t(kvt[:BR_WIDTH], dkg_ref[...], bd32, DIFF_SUB)
    kdf_ref[n_prev, 0] = kd
    vdf_ref[n_prev, 0] = kvt[BR_WIDTH:]
    kdf16_ref[0] = _group_rms(pq[:, 3 * BR_WIDTH:4 * BR_WIDTH], dkgr_ref[...], bd32, DIFF_SUB).astype(BF16)
    vdf16_ref[0] = kvt[BR_WIDTH:].astype(BF16)

    qm = _group_rms(pq[:, 4 * BR_WIDTH:], mqg_ref[...], bd64_ref[...], HEAD_DIM)
    qm = (qm * (HEAD_DIM ** -0.5 * LOG2E)).astype(BF16)
    yabm_ref[0, :, 2 * BR_WIDTH:] = _mem_attention(qm, mk_ref[0], mv_ref[0], eye_ref[...]).astype(BF16)


def _inproj(x, p, mk16, mv16, tm, prev_kv):
    n, t, d = x.shape
    n_prev = prev_kv[0].shape[0] if prev_kv else 0
    prev_kv = list(prev_kv) if prev_kv else []
    stacked = lambda layers: pl.BlockSpec((layers, 1, BR_WIDTH, tm), lambda i, j: (0, i, 0, j))
    consts = [p["norm1_g"], p["w_in_tok"], p["w_sb_kv_t"], p["w_df_kv_t"], p["conv_a_w"], p["gmlp_ln_g"], p["gmlp_ln_b"],
              p["gmlp_ws"], p["gmlp_b_full"], p["diff_qn_g"], p["diff_kn_g_col"], p["diff_kn_g"], p["mem_qn_g"],
              p["bd32"], p["bd64"], jnp.asarray(np.eye(tm), dtype=BF16)]
    seq = lambda w: pl.BlockSpec((1, tm, w), lambda i, j: (i, j, 0))
    seq_t = pl.BlockSpec((1, BR_WIDTH, tm), lambda i, j: (i, 0, j))
    mem_spec = lambda a: pl.BlockSpec((1,) + a.shape[1:], lambda i, j: (i, 0, 0))
    bf = lambda w: jax.ShapeDtypeStruct((n, t, w), BF16)
    bf_t = jax.ShapeDtypeStruct((n, BR_WIDTH, t), BF16)
    fl_t = jax.ShapeDtypeStruct((n_prev + 1, n, BR_WIDTH, t), F32)
    return pl.pallas_call(
        functools.partial(_inproj_kernel, tm=tm, n_prev=n_prev),
        grid=(n, t // tm),
        in_specs=[seq(d)] + [_const_spec(c) for c in consts] + [mem_spec(mk16), mem_spec(mv16)]
        + [stacked(n_prev)] * len(prev_kv),
        out_specs=[seq(3 * BR_WIDTH), seq(BR_WIDTH), seq(BR_WIDTH), seq_t, seq(BR_WIDTH), seq(BR_WIDTH), seq_t]
        + [stacked(n_prev + 1)] * 4 + [pl.BlockSpec((1, 2, BR_WIDTH), lambda i, j: (i, 0, 0))],
        out_shape=[bf(3 * BR_WIDTH), bf(BR_WIDTH), bf(BR_WIDTH), bf_t, bf(BR_WIDTH), bf(BR_WIDTH), bf_t] + [fl_t] * 4
        + [jax.ShapeDtypeStruct((n, 2, BR_WIDTH), F32)],
        scratch_shapes=[pltpu.VMEM((8, BR_WIDTH), F32)],
        compiler_params=_params("arbitrary", "arbitrary"),
        name="in_proj",
    )(x, *map(_operand, consts), mk16, mv16, *prev_kv)


def _sb_kernel(q_ref, k_ref, v_ref, uu_ref, eye_ref, o_ref, qm_ref, carry_ref, acc_ref, *, tb):
    qi = pl.program_id(1)
    q = q_ref[0]
    lane = _lane_iota(q.shape)
    for h in range(N_GROUPS):
        in_head = (lane >= HEAD_DIM * h) & (lane < HEAD_DIM * (h + 1))
        qm_ref[h * tb:(h + 1) * tb] = jnp.where(in_head, q, jnp.zeros_like(q))
    strictly_before = _row_iota((tb, tb)) < _lane_iota((tb, tb))
    heads = range(N_GROUPS)

    def block(kb, first):
        start = pl.multiple_of(kb * tb, tb)
        z_all = _dot_nt(k_ref[0, pl.ds(start, tb), :], qm_ref[...])
        lbs, l1ms = [], []
        for h in heads:
            lb, l1m = _log2_sigmoid_pair(z_all[:, h * tb:(h + 1) * tb])
            lbs.append(lb)
            l1ms.append(jnp.where(strictly_before, l1m, 0.0) if first else l1m)
        hi, lo = _split_bf16(jnp.concatenate(l1ms, axis=1))
        between = _dot(uu_ref[...], jnp.concatenate([hi, lo], axis=0))
        for h in heads:
            total = lbs[h] + between[:, h * tb:(h + 1) * tb]
            if not first:
                total = total + carry_ref[h:h + 1]
            a = jnp.exp2(total)
            if first:
                a = jnp.where(strictly_before, a, 0.0)
            pv = _dot(v_ref[0, HEAD_DIM * h:HEAD_DIM * (h + 1), pl.ds(start, tb)], a.astype(BF16))
            block_sum = jnp.sum(l1ms[h], axis=0, keepdims=True)
            if first:
                acc_ref[h] = pv
                carry_ref[h:h + 1] = block_sum
            else:
                acc_ref[h] += pv
                carry_ref[h:h + 1] += block_sum

    block(qi, True)

    def cond(state):
        i, live = state
        return (i < qi) & (live > SB_UNDERFLOW_LOG2)

    def body(state):
        i, _ = state
        block(qi - 1 - i, False)
        return i + 1, jnp.max(carry_ref[...])

    lax.while_loop(cond, body, (jnp.int32(0), jnp.max(carry_ref[...])))
    yt = jnp.concatenate([acc_ref[h] for h in heads], axis=0)
    o_ref[0] = _dot_nt(eye_ref[...], yt.astype(BF16)).astype(BF16)


def _sb_attention(q, k, v, tb):
    n, t, w = q.shape
    i = np.arange(tb)
    later = (i[None, :] > i[:, None])
    uu = jnp.asarray(np.concatenate([later, later], axis=1), dtype=BF16)
    eye = jnp.asarray(np.eye(tb), dtype=BF16)
    return pl.pallas_call(
        functools.partial(_sb_kernel, tb=tb),
        grid=(n, t // tb),
        in_specs=[pl.BlockSpec((1, tb, w), lambda i, j: (i, j, 0)),
                  pl.BlockSpec((1, t, w), lambda i, j: (i, 0, 0)),
                  pl.BlockSpec((1, w, t), lambda i, j: (i, 0, 0)),
                  _full_spec(uu.shape), _full_spec(eye.shape)],
        out_specs=pl.BlockSpec((1, tb, w), lambda i, j: (i, j, 0)),
        out_shape=jax.ShapeDtypeStruct((n, t, w), BF16),
        scratch_shapes=[pltpu.VMEM((N_GROUPS * tb, w), BF16), pltpu.VMEM((N_GROUPS, tb), F32),
                        pltpu.VMEM((N_GROUPS, HEAD_DIM, tb), F32)],
        compiler_params=_params("arbitrary", "arbitrary"),
        name="sb_attn",
    )(q, k, v, uu, eye)


def _diff_kernel(q_ref, k_ref, v_ref, lp_ref, og_ref, bd64_ref, eye_ref, o_ref,
                 qm_ref, bias_ref, m_ref, l_ref, acc_ref, *, tb, lam_init):
    qi = pl.program_id(1)
    slopes = _alibi_slopes_log2()
    rel = _row_iota((tb, tb)) - _lane_iota((tb, tb))

    @pl.when(qi == 0)
    def _():
        for h in range(N_GROUPS):
            bias_ref[h] = slopes[h] * rel.astype(F32)

    q = q_ref[0]
    lane = _lane_iota(q.shape)
    for c in range(2):
        for h in range(N_GROUPS):
            lo = HEAD_DIM * h + DIFF_SUB * c
            ch = c * N_GROUPS + h
            qm_ref[ch * tb:(ch + 1) * tb] = jnp.where((lane >= lo) & (lane < lo + DIFF_SUB), q, jnp.zeros_like(q))

    def scores(kb):
        start = pl.multiple_of(kb * tb, tb)
        return _dot_nt(k_ref[0, pl.ds(start, tb), :], qm_ref[...])

    def block(kb, score_tile, first):
        start = pl.multiple_of(kb * tb, tb)
        off = ((kb - qi) * tb).astype(F32)
        probs, alphas = [], []
        for c in range(2):
            for h in range(N_GROUPS):
                ch = c * N_GROUPS + h
                s = score_tile(ch) + bias_ref[h]
                shift = slopes[h] * off
                if first:
                    s = jnp.where(rel <= 0, s, NEG_BIG)
                    m_new = jnp.max(s, axis=0, keepdims=True) + shift
                    p = jnp.exp2(s - (m_new - shift))
                    l_ref[ch:ch + 1] = jnp.sum(p, axis=0, keepdims=True)
                    alphas.append(None)
                else:
                    m_old = m_ref[ch:ch + 1]
                    m_new = jnp.maximum(m_old, jnp.max(s, axis=0, keepdims=True) + shift)
                    alpha = jnp.exp2(m_old - m_new)
                    p = jnp.exp2(s - (m_new - shift))
                    l_ref[ch:ch + 1] = alpha * l_ref[ch:ch + 1] + jnp.sum(p, axis=0, keepdims=True)
                    alphas.append(alpha)
                m_ref[ch:ch + 1] = m_new
                probs.append(p.astype(BF16))
        for h in range(N_GROUPS):
            vh = v_ref[0, HEAD_DIM * h:HEAD_DIM * (h + 1), pl.ds(start, tb)]
            pv = _dot(vh, jnp.concatenate([probs[h], probs[N_GROUPS + h]], axis=1))
            for c in range(2):
                ch = c * N_GROUPS + h
                new = pv[:, c * tb:(c + 1) * tb]
                acc_ref[ch] = new if first else alphas[ch] * acc_ref[ch] + new

    def run_block(kb, first):
        s_all = scores(kb)
        block(kb, lambda ch: s_all[:, ch * tb:(ch + 1) * tb], first)

    run_block(qi, True)

    def body(i, _):
        run_block(qi - 1 - i, False)
        return 0

    lax.fori_loop(0, qi, body, 0)
    lam = _lam_value(lp_ref[...], lam_init)
    ys = []
    for h in range(N_GROUPS):
        y0 = acc_ref[h] / l_ref[h:h + 1]
        y1 = acc_ref[N_GROUPS + h] / l_ref[N_GROUPS + h:N_GROUPS + h + 1]
        ys.append(y0 - lam * y1)
    yt = _group_rms_t(jnp.concatenate(ys, axis=0), og_ref[...], bd64_ref[...], HEAD_DIM) * (1.0 - lam_init)
    o_ref[0] = _dot_nt(eye_ref[...], yt.astype(BF16)).astype(BF16)


def _diff_attention(q, k, v, lp, og_col, bd64, lam_init, tb):
    n, t, w = q.shape
    maps = 2 * N_GROUPS
    eye = jnp.asarray(np.eye(tb), dtype=BF16)
    return pl.pallas_call(
        functools.partial(_diff_kernel, tb=tb, lam_init=lam_init),
        grid=(n, t // tb),
        in_specs=[pl.BlockSpec((1, tb, w), lambda i, j: (i, j, 0)),
                  pl.BlockSpec((1, t, w), lambda i, j: (i, 0, 0)),
                  pl.BlockSpec((1, w, t), lambda i, j: (i, 0, 0)),
                  _full_spec(lp.shape), _full_spec(og_col.shape), _full_spec(bd64.shape), _full_spec(eye.shape)],
        out_specs=pl.BlockSpec((1, tb, w), lambda i, j: (i, j, 0)),
        out_shape=jax.ShapeDtypeStruct((n, t, w), BF16),
        scratch_shapes=[pltpu.VMEM((maps * tb, w), BF16), pltpu.VMEM((N_GROUPS, tb, tb), F32),
                        pltpu.VMEM((maps, tb), F32), pltpu.VMEM((maps, tb), F32),
                        pltpu.VMEM((maps, HEAD_DIM, tb), F32)],
        compiler_params=_params("arbitrary", "arbitrary"),
        name="diff_attn",
    )(q, k, v, lp, og_col, bd64, eye)


def _merge_kernel(x_ref, yabm_ref, yc_ref, yd_ref, g1_ref, wg_ref, bg_ref, wb_ref, wo_ref, o_ref):
    x = x_ref[0]
    hb = _rms(x, g1_ref[...]).astype(BF16)
    ys = [yabm_ref[0, :, 0:BR_WIDTH], yabm_ref[0, :, BR_WIDTH:2 * BR_WIDTH], yc_ref[0], yd_ref[0],
          yabm_ref[0, :, 2 * BR_WIDTH:]]
    acc = None
    for b in range(5):
        gate = jax.nn.sigmoid(_dot(hb, wg_ref[:, b * D_MODEL:(b + 1) * D_MODEL]) + bg_ref[:, b * D_MODEL:(b + 1) * D_MODEL])
        term = gate * _dot(ys[b], wb_ref[b])
        acc = term if acc is None else acc + term
    o_ref[0] = x + _dot(acc.astype(BF16), wo_ref[...])


def _merge(x, yabm, yc, yd, p, tm):
    n, t, d = x.shape
    consts = [p["norm1_g"], p["w_gate"], p["b_gate"], p["w_branch"], p["w_o"]]
    seq = lambda w: pl.BlockSpec((1, tm, w), lambda i, j: (i, j, 0))
    return pl.pallas_call(
        _merge_kernel,
        grid=(n, t // tm),
        in_specs=[seq(d), seq(3 * BR_WIDTH), seq(BR_WIDTH), seq(BR_WIDTH)] + [_const_spec(c) for c in consts],
        out_specs=seq(d),
        out_shape=jax.ShapeDtypeStruct((n, t, d), F32),
        compiler_params=_params("arbitrary", "arbitrary"),
        name="merge",
    )(x, yabm, yc, yd, *map(_operand, consts))


MXU_WIDTH = 256
FFN_CHUNKS = ((0, 6 * MXU_WIDTH), (6 * MXU_WIDTH, D_FF - 6 * MXU_WIDTH))


def _ffn_kernel(x_ref, g2_ref, wup_ref, cw_ref, cb_ref, wdn_ref, o_ref, st_ref, carry_ref, *, tm):
    t = pl.program_id(1)
    x = x_ref[0]
    hb = _rms(x, g2_ref[...]).astype(BF16)

    @pl.when(t == 0)
    def _():
        carry_ref[...] = jnp.zeros_like(carry_ref)

    def conv(lo, width):
        cols = slice(lo, lo + width)
        row = _row_iota((tm, width))
        u = _dot(hb, wup_ref[:, cols])
        prev = carry_ref[:, cols]
        p1, p2 = prev[7:8], prev[6:7]
        s1 = jnp.where(row == 0, p1, pltpu.roll(u, 1, axis=0))
        s2 = jnp.where(row == 0, p2, jnp.where(row == 1, p1, pltpu.roll(u, 2, axis=0)))
        carry_ref[:, cols] = u[tm - 8:tm]
        st_ref[0, :, cols] = u[tm - 2:tm]
        return cw_ref[0:1, cols] * s2 + cw_ref[1:2, cols] * s1 + cw_ref[2:3, cols] * u + cb_ref[:, cols]

    acc = x
    for lo, width in FFN_CHUNKS:
        gate = conv(lo, width)
        val = conv(D_FF + lo, width)
        act = (gate * jax.nn.sigmoid(gate) * val).astype(BF16)
        acc = acc + _dot(act, wdn_ref[lo:lo + width, :])
    o_ref[0] = acc


def _ffn(x, p, tm):
    n, t, d = x.shape
    consts = [p["norm2_g"], p["w_up"], p["conv_ffn_w"], p["conv_ffn_b"], p["w_down"]]
    seq = pl.BlockSpec((1, tm, d), lambda i, j: (i, j, 0))
    return pl.pallas_call(
        functools.partial(_ffn_kernel, tm=tm),
        grid=(n, t // tm),
        in_specs=[seq] + [_const_spec(c) for c in consts],
        out_specs=[seq, pl.BlockSpec((1, 2, 2 * D_FF), lambda i, j: (i, 0, 0))],
        out_shape=[jax.ShapeDtypeStruct((n, t, d), F32), jax.ShapeDtypeStruct((n, 2, 2 * D_FF), F32)],
        scratch_shapes=[pltpu.VMEM((8, 2 * D_FF), F32)],
        compiler_params=_params("arbitrary", "arbitrary"),
        name="conv_ffn",
    )(x, *map(_operand, consts))


def _ffn_step_kernel(x_ref, p0_ref, p1_ref, g2_ref, wup_ref, cw_ref, cb_ref, wdn_ref, o_ref, u_ref):
    x = x_ref[...]
    hb = _rms(x, g2_ref[...]).astype(BF16)

    def conv(lo, width):
        cols = slice(lo, lo + width)
        u = _dot(hb, wup_ref[:, cols])
        u_ref[:, cols] = u
        return (cw_ref[0:1, cols] * p0_ref[:, cols] + cw_ref[1:2, cols] * p1_ref[:, cols]
                + cw_ref[2:3, cols] * u + cb_ref[:, cols])

    acc = x
    for lo, width in FFN_CHUNKS:
        gate = conv(lo, width)
        val = conv(D_FF + lo, width)
        act = (gate * jax.nn.sigmoid(gate) * val).astype(BF16)
        acc = acc + _dot(act, wdn_ref[lo:lo + width, :])
    o_ref[...] = acc


def _ffn_step(x, prev0, prev1, p):
    m, d = x.shape
    consts = [p["norm2_g"], p["w_up"], p["conv_ffn_w"], p["conv_ffn_b"], p["w_down"]]
    args = [x, prev0, prev1] + consts
    return pl.pallas_call(
        _ffn_step_kernel,
        grid=(1,),
        in_specs=[_const_spec(a) for a in args],
        out_specs=[_full_spec((m, d)), _full_spec((m, 2 * D_FF))],
        out_shape=[jax.ShapeDtypeStruct((m, d), F32), jax.ShapeDtypeStruct((m, 2 * D_FF), F32)],
        compiler_params=_params("arbitrary"),
        name="conv_ffn_step",
    )(*map(_operand, args))


def _inproj_step_kernel(x_ref, c0_ref, c1_ref, g1_ref, win_ref, cw_ref, lng_ref, lnb_ref, ws_ref, gb_ref,
                        dqg_ref, dkg_ref, mqg_ref, bd32_ref, bd64_ref,
                        yab_ref, q_ref, ksb_ref, vsb_ref, kdf_ref, vdf_ref, vn_ref, ach_ref):
    hb = _rms(x_ref[...], g1_ref[...]).astype(BF16)

    def proj(lo, hi):
        return _dot(hb, win_ref[:, lo:hi])

    pa = proj(0, 3 * BR_WIDTH)
    ach = pa[:, BR_WIDTH:2 * BR_WIDTH] * pa[:, 2 * BR_WIDTH:]
    cw = cw_ref[...]
    yab_ref[:, 0:BR_WIDTH] = (pa[:, :BR_WIDTH] * (cw[0:1] * c0_ref[...] + cw[1:2] * c1_ref[...] + cw[2:3] * ach)).astype(BF16)
    ach_ref[...] = ach

    pg = proj(3 * BR_WIDTH, 5 * BR_WIDTH)
    vn = _layer_norm(pg[:, BR_WIDTH:], lng_ref[...], lnb_ref[...])
    vn_ref[...] = vn
    lane = _lane_iota((1, BR_WIDTH))
    w00 = jnp.zeros((1, BR_WIDTH), F32)
    for g in range(N_GROUPS):
        w00 = jnp.where((lane >= HEAD_DIM * g) & (lane < HEAD_DIM * (g + 1)), ws_ref[g, 0:1, 0:1], w00)
    yab_ref[:, BR_WIDTH:] = (pg[:, :BR_WIDTH] * (w00 * vn + gb_ref[0:1, :])).astype(BF16)

    ps = proj(5 * BR_WIDTH, 8 * BR_WIDTH)
    q_ref[:, 0:BR_WIDTH] = (ps[:, :BR_WIDTH] * (HEAD_DIM ** -0.5)).astype(BF16)
    ksb_ref[...] = ps[:, BR_WIDTH:2 * BR_WIDTH]
    vsb_ref[...] = ps[:, 2 * BR_WIDTH:]

    pd = proj(8 * BR_WIDTH, 11 * BR_WIDTH)
    bd32 = bd32_ref[...]
    qd = _group_rms(pd[:, :BR_WIDTH], dqg_ref[...], bd32, DIFF_SUB)
    q_ref[:, BR_WIDTH:2 * BR_WIDTH] = (qd * (DIFF_SUB ** -0.5 * LOG2E)).astype(BF16)
    kdf_ref[...] = _group_rms(pd[:, BR_WIDTH:2 * BR_WIDTH], dkg_ref[...], bd32, DIFF_SUB)
    vdf_ref[...] = pd[:, 2 * BR_WIDTH:]

    qm = _group_rms(proj(11 * BR_WIDTH, 12 * BR_WIDTH), mqg_ref[...], bd64_ref[...], HEAD_DIM)
    q_ref[:, 2 * BR_WIDTH:] = (qm * (HEAD_DIM ** -0.5 * LOG2E)).astype(BF16)


def _inproj_step(x, c0, c1, p):
    m, d = x.shape
    consts = [p["norm1_g"], p["w_in"], p["conv_a_w"], p["gmlp_ln_g"], p["gmlp_ln_b"], p["gmlp_ws"],
              p["gmlp_b_full"], p["diff_qn_g"], p["diff_kn_g"], p["mem_qn_g"], p["bd32"], p["bd64"]]
    args = [x, c0, c1] + consts
    fl = jax.ShapeDtypeStruct((m, BR_WIDTH), F32)
    out_shape = [jax.ShapeDtypeStruct((m, 2 * BR_WIDTH), BF16), jax.ShapeDtypeStruct((m, 3 * BR_WIDTH), BF16)] + [fl] * 6
    return pl.pallas_call(
        _inproj_step_kernel,
        grid=(1,),
        in_specs=[_const_spec(a) for a in args],
        out_specs=[_full_spec(s.shape) for s in out_shape],
        out_shape=out_shape,
        compiler_params=_params("arbitrary"),
        name="in_proj_step",
    )(*map(_operand, args))


def _decode_kernel(pt_ref, q_ref, kn_ref, vn_ref, mk_ref, mv_ref, lp_ref, og_ref, bd64_ref, ll_ref, ones_ref,
                   ksb_hbm, vsb_hbm, kdf_hbm, vdf_hbm, o_ref, buf_ref, sem_ref, *, layer, n_samples, n_pages, page,
                   lam_init):
    step = pl.program_id(0)
    pools = (ksb_hbm, vsb_hbm, kdf_hbm, vdf_hbm)

    def page_copy(sample, slot, c, j):
        return pltpu.make_async_copy(pools[c].at[layer, pt_ref[sample, j]], buf_ref.at[slot, c, j], sem_ref.at[slot])

    def for_each_page(sample, slot, fn):
        for c in range(len(pools)):
            for j in range(n_pages):
                fn(page_copy(sample, slot, c, j))

    ahead = DECODE_SLOTS - 1

    @pl.when(step == 0)
    def _():
        for s in range(min(ahead, n_samples)):
            for_each_page(s, s, lambda cp: cp.start())

    @pl.when(step + ahead < n_samples)
    def _():
        for_each_page(step + ahead, lax.rem(step + ahead, DECODE_SLOTS), lambda cp: cp.start())

    slot = lax.rem(step, DECODE_SLOTS)
    for_each_page(step, slot, lambda cp: cp.wait())
    ksb, vsb, kdf, vdf = ([buf_ref.at[slot, c, j] for j in range(n_pages)] for c in range(len(pools)))
    rows = 2 * N_GROUPS
    n_sub = HEAD_DIM // DIFF_SUB
    ones = ones_ref[...]

    def q_column(col):
        qrow = jnp.broadcast_to(q_ref[0, :, col * BR_WIDTH:(col + 1) * BR_WIDTH].astype(F32), (BR_WIDTH, BR_WIDTH))
        diag = jnp.where(_row_iota(qrow.shape) == _lane_iota(qrow.shape), qrow, 0.0).astype(BF16)
        return _dot(diag, ones)

    def group_scores(kt, qcol):
        return jnp.sum((kt * qcol).reshape(rows, DIFF_SUB, kt.shape[1]), axis=1)

    def pair_sum(z):
        r = z.shape[0]
        even = (_row_iota(z.shape) & 1) == 0
        return z + jnp.where(even, pltpu.roll(z, r - 1, axis=0), pltpu.roll(z, 1, axis=0))

    def head_rows(w):
        return jnp.concatenate([jnp.broadcast_to(w[n_sub * h:n_sub * h + 1], (HEAD_DIM, w.shape[1]))
                                for h in range(N_GROUPS)], axis=0)

    def reduce_positions(acc):
        hi, lo = _split_bf16(acc)
        ones_row = jnp.ones((rows, acc.shape[1]), BF16)
        return _dot_nt(ones_row, hi) + _dot_nt(ones_row, lo)

    qc = q_column(0)
    z = jnp.concatenate([pair_sum(group_scores(ksb[pg][...], qc)) for pg in range(n_pages)], axis=0)
    lb, l1m = _log_sigmoid_pair(z)
    hi, lo = _split_bf16(l1m)
    hl = jnp.concatenate([hi, lo], axis=1)
    within = _dot(hl, ll_ref[...])
    total = _dot(hl, ones)
    carry = jnp.zeros((rows, page), F32)
    carries = [None] * n_pages
    for pg in reversed(range(n_pages)):
        carries[pg] = carry
        carry = carry + total[rows * pg:rows * (pg + 1)]
    a = jnp.exp(lb + within + jnp.concatenate(carries, axis=0))
    acc = jnp.zeros((BR_WIDTH, page), F32)
    for pg in range(n_pages):
        acc = acc + head_rows(a[rows * pg:rows * (pg + 1)]) * vsb[pg][...]
    y_c = reduce_positions(acc)[0:1]

    qc = q_column(1)
    s = jnp.concatenate([group_scores(kdf[pg][...], qc) for pg in range(n_pages)], axis=0)
    slopes = _alibi_slopes_log2()
    all_rows = _row_iota((rows * n_pages, page))
    head = _div_pow2(_rem_pow2(all_rows, rows), n_sub)
    sl = jnp.full(all_rows.shape, slopes[0], F32)
    for h in range(1, N_GROUPS):
        sl = jnp.where(head == h, slopes[h], sl)
    past = n_pages * page
    k_pos = _div_pow2(all_rows, rows) * page + _lane_iota(all_rows.shape)
    s = s + sl * (k_pos - past).astype(F32)
    row8 = _row_iota((rows, BR_WIDTH))
    group_lanes = _div_pow2(_lane_iota((rows, BR_WIDTH)), DIFF_SUB) == row8
    q8 = jnp.broadcast_to(q_ref[0, :, BR_WIDTH:2 * BR_WIDTH].astype(F32), (rows, BR_WIDTH))
    kn8 = jnp.broadcast_to(kn_ref[0], (rows, BR_WIDTH))
    s_new = jnp.sum(jnp.where(group_lanes, q8 * kn8, 0.0), axis=1, keepdims=True)
    m8 = s[0:rows]
    for pg in range(1, n_pages):
        m8 = jnp.maximum(m8, s[rows * pg:rows * (pg + 1)])
    m = jnp.maximum(jnp.max(m8, axis=1, keepdims=True), s_new)
    p_new = jnp.exp2(s_new - m)
    m_all = jnp.concatenate([jnp.broadcast_to(m, (rows, page))] * n_pages, axis=0)
    p = jnp.exp2(s - m_all)
    l8 = p[0:rows]
    for pg in range(1, n_pages):
        l8 = l8 + p[rows * pg:rows * (pg + 1)]
    l = jnp.sum(l8, axis=1, keepdims=True) + p_new
    lam = _lam_value(lp_ref[...], lam_init)
    first_map = (_row_iota((rows, 1)) & 1) == 0
    coef = jnp.where(first_map, 1.0, -lam) / l
    w = pair_sum(p * jnp.concatenate([jnp.broadcast_to(coef, (rows, page))] * n_pages, axis=0))
    acc = jnp.zeros((BR_WIDTH, page), F32)
    for pg in range(n_pages):
        acc = acc + head_rows(w[rows * pg:rows * (pg + 1)]) * vdf[pg][...]
    w_new = pair_sum(p_new * coef)
    head_lanes = (_div_pow2(_lane_iota((rows, BR_WIDTH)), HEAD_DIM) * n_sub) == row8
    vn8 = jnp.broadcast_to(vn_ref[0], (rows, BR_WIDTH))
    y_new = jnp.sum(jnp.where(head_lanes, w_new * vn8, 0.0), axis=0, keepdims=True)
    y_d = reduce_positions(acc)[0:1] + y_new
    y_d = _group_rms(jnp.broadcast_to(y_d, (rows, BR_WIDTH)), og_ref[...], bd64_ref[...], HEAD_DIM)[0:1]
    y_d = y_d * (1.0 - lam_init)

    qc = q_column(2)
    n_mem = mk_ref.shape[2]
    qc = jnp.concatenate([qc] * (n_mem // page), axis=1)
    s = pair_sum(group_scores(mk_ref[0], qc))
    p = jnp.exp2(s - jnp.max(s, axis=1, keepdims=True))
    p = p / jnp.sum(p, axis=1, keepdims=True)
    y_m = reduce_positions(head_rows(p) * mv_ref[0])[0:1]

    o_ref[0, :, 0:BR_WIDTH] = y_c.astype(BF16)
    o_ref[0, :, BR_WIDTH:2 * BR_WIDTH] = y_d.astype(BF16)
    o_ref[0, :, 2 * BR_WIDTH:] = y_m.astype(BF16)


def _decode_attention(layer, page_table, q, k_new, v_new, caches, mem_k, mem_v, p, lam_init):
    b = q.shape[0]
    n_pages = page_table.shape[1]
    page = caches[0].shape[3]
    ll = _suffix_ones(page)
    row_spec = lambda w: pl.BlockSpec((1, 1, w), lambda i, pt: (i, 0, 0))
    mem_spec = pl.BlockSpec((None, 1) + mem_k.shape[2:], lambda i, pt: (layer, i, 0, 0))
    const = lambda a: pl.BlockSpec(a.shape, lambda i, pt: (0,) * a.ndim)
    consts = [p["diff_lambda"], p["diff_out_g"], p["bd64"], ll, jnp.ones((BR_WIDTH, page), BF16)]
    grid_spec = pltpu.PrefetchScalarGridSpec(
        num_scalar_prefetch=1,
        grid=(b,),
        in_specs=[row_spec(3 * BR_WIDTH), row_spec(BR_WIDTH), row_spec(BR_WIDTH), mem_spec, mem_spec]
        + [const(c) for c in consts] + [pl.BlockSpec(memory_space=pl.ANY)] * len(caches),
        out_specs=row_spec(3 * BR_WIDTH),
        scratch_shapes=[pltpu.VMEM((DECODE_SLOTS, len(caches), n_pages, BR_WIDTH, page), F32),
                        pltpu.SemaphoreType.DMA((DECODE_SLOTS,))],
    )
    return pl.pallas_call(
        functools.partial(_decode_kernel, layer=layer, n_samples=b, n_pages=n_pages, page=page, lam_init=lam_init),
        grid_spec=grid_spec,
        out_shape=jax.ShapeDtypeStruct((b, 1, 3 * BR_WIDTH), BF16),
        compiler_params=_params("arbitrary"),
        name="decode_attn",
    )(page_table, q, k_new, v_new, mem_k, mem_v, *consts, *caches)


PROMPT_TILE = 512
DECODE_SLOTS = 3
ATTN_BLOCK = 256


def _cast_weights(w):
    big = {k: w[k].astype(BF16) for k in ("w_in", "w_gate", "w_branch", "w_o", "w_up", "w_down")}
    w_in = big["w_in"]
    big["w_in_tok"] = jnp.concatenate([w_in[:, :, :7 * BR_WIDTH], w_in[:, :, 8 * BR_WIDTH:10 * BR_WIDTH],
                                       w_in[:, :, 11 * BR_WIDTH:]], axis=2)
    return big


def _layer_params(l, w, big):
    row = lambda a: a[l].reshape(1, -1).astype(F32)
    tile4 = lambda a: jnp.tile(a[l].astype(F32), N_GROUPS).reshape(1, -1)
    col4 = lambda a: jnp.tile(a[l].astype(F32), N_GROUPS).reshape(-1, 1)
    w_in = big["w_in"][l]
    return dict(
        norm1_g=row(w["norm1_g"]), w_in=_Layered(big["w_in"], l), conv_a_w=w["conv_a_w"][l].astype(F32),
        w_sb_kv_t=w_in[:, 6 * BR_WIDTH:8 * BR_WIDTH].T, w_df_kv_t=w_in[:, 9 * BR_WIDTH:11 * BR_WIDTH].T,
        w_in_tok=_Layered(big["w_in_tok"], l),
        diff_kn_g_col=col4(w["diff_kn_g"]), mem_kn_g_col=col4(w["mem_kn_g"]), diff_out_g_col=col4(w["diff_out_g"]),
        w_mem_kv_t=w["w_mem_kv"][l].astype(BF16).T,
        gmlp_ln_g=row(w["gmlp_ln_g"]), gmlp_ln_b=row(w["gmlp_ln_b"]), gmlp_ws=w["gmlp_ws"][l].astype(F32),
        gmlp_b_full=jnp.repeat(w["gmlp_b"][l].astype(F32).T, HEAD_DIM, axis=1),
        diff_qn_g=tile4(w["diff_qn_g"]), diff_kn_g=tile4(w["diff_kn_g"]), diff_out_g=tile4(w["diff_out_g"]),
        diff_lambda=w["diff_lambda"][l].astype(F32),
        mem_norm_g=row(w["mem_norm_g"]), mem_qn_g=tile4(w["mem_qn_g"]),
        w_branch=_Layered(big["w_branch"], l), w_gate=_Layered(big["w_gate"], l), b_gate=row(w["b_gate"]),
        w_o=_Layered(big["w_o"], l), norm2_g=row(w["norm2_g"]), w_up=_Layered(big["w_up"], l),
        conv_ffn_w=w["conv_ffn_w"][l].astype(F32), conv_ffn_b=row(w["conv_ffn_b"]),
        w_down=_Layered(big["w_down"], l),
        bd32=_block_ones(DIFF_SUB), bd64=_block_ones(HEAD_DIM),
    )


def _prompt_layer(x, mem, p, lam_init, prev_kv):
    n, t, _ = x.shape
    tm = min(PROMPT_TILE, t)
    tb = min(ATTN_BLOCK, t)
    mk, mv, mk16, mv16 = _mem_kv(mem, p["mem_norm_g"], p["w_mem_kv_t"], p["mem_kn_g_col"], p["bd64"])
    (yabm, qsb, ksb16, vsb16, qdf, kdf16, vdf16, *kv, ca) = _inproj(x, p, mk16, mv16, tm, prev_kv)
    yc = _sb_attention(qsb, ksb16, vsb16, tb)
    yd = _diff_attention(qdf, kdf16, vdf16, p["diff_lambda"], p["diff_out_g_col"], p["bd64"], lam_init, tb)
    x1 = _merge(x, yabm, yc, yd, p, tm)
    x2, cf = _ffn(x1, p, tm)
    return x2, kv, (ca, mk, mv, cf)


def _sample_layer(l, x, conv_a, conv_ffn, page_table, caches, mem_k, mem_v, p, lam_init):
    b = x.shape[0]
    yab, q, ksb, vsb, kdf, vdf, vn, ach = _inproj_step(x, conv_a[:, 0], conv_a[:, 1], p)
    row3 = lambda a: a.reshape(b, 1, -1)
    ycdm = _decode_attention(l, page_table, row3(q), row3(kdf), row3(vdf), caches, mem_k, mem_v, p, lam_init)
    yabm = jnp.concatenate([yab, ycdm[:, 0, 2 * BR_WIDTH:]], axis=1)
    x1 = _merge(x[None], yabm[None], ycdm[None, :, 0, 0:BR_WIDTH], ycdm[None, :, 0, BR_WIDTH:2 * BR_WIDTH], p, b)[0]
    x2, u = _ffn_step(x1, conv_ffn[:, 0], conv_ffn[:, 1], p)
    ca_new = jnp.stack([conv_a[:, 1], ach], axis=1)
    cf_new = jnp.stack([conv_ffn[:, 1], u], axis=1)
    return x2, ca_new, vn, ksb, vsb, kdf, vdf, cf_new


def kernel(x_prompt, x_sample, state_conv_a, cache_k_sb, cache_v_sb, cache_k_diff, cache_v_diff, cache_mem_k,
           cache_mem_v, state_conv_ffn, page_table, mem_prompt, norm1_g, w_in, conv_a_w, gmlp_ln_g, gmlp_ln_b,
           gmlp_ws, gmlp_b, diff_qn_g, diff_kn_g, diff_lambda, diff_out_g, mem_norm_g, w_mem_kv, mem_qn_g,
           mem_kn_g, w_branch, w_gate, b_gate, w_o, norm2_g, w_up, conv_ffn_w, conv_ffn_b, w_down):
    w = dict(norm1_g=norm1_g, w_in=w_in, conv_a_w=conv_a_w, gmlp_ln_g=gmlp_ln_g, gmlp_ln_b=gmlp_ln_b,
             gmlp_ws=gmlp_ws, gmlp_b=gmlp_b, diff_qn_g=diff_qn_g, diff_kn_g=diff_kn_g, diff_lambda=diff_lambda,
             diff_out_g=diff_out_g, mem_norm_g=mem_norm_g, w_mem_kv=w_mem_kv, mem_qn_g=mem_qn_g,
             mem_kn_g=mem_kn_g, w_branch=w_branch, w_gate=w_gate, b_gate=b_gate, w_o=w_o, norm2_g=norm2_g,
             w_up=w_up, conv_ffn_w=conv_ffn_w, conv_ffn_b=conv_ffn_b, w_down=w_down)
    depth = w_in.shape[0]
    n_p, t_p, _ = x_prompt.shape
    n_s = x_sample.shape[0]
    flat = lambda c: jnp.transpose(c, (0, 1, 3, 4, 2)).reshape(c.shape[:2] + (BR_WIDTH, c.shape[2]))
    caches = [flat(cache_k_sb), flat(cache_v_sb), flat(cache_k_diff), flat(cache_v_diff)]
    mem_k, mem_v = flat(cache_mem_k), flat(cache_mem_v)
    xp, xs = x_prompt, x_sample[:, 0]
    outs_p, outs_s = [], []
    big = _cast_weights(w)
    kv_p = None
    for l in range(depth):
        p = _layer_params(l, w, big)
        lam_init = 0.8 - 0.6 * math.exp(-0.3 * l)
        xp, kv_p, rest_p = _prompt_layer(xp, mem_prompt, p, lam_init, kv_p)
        outs_p.append(rest_p)
        xs, *rest_s = _sample_layer(l, xs, state_conv_a[l], state_conv_ffn[l], page_table, caches, mem_k, mem_v,
                                    p, lam_init)
        outs_s.append(rest_s)
    heads = lambda a: a.reshape(a.shape[:-1] + (N_GROUPS, HEAD_DIM))
    stack_p = lambda i: jnp.stack([o[i] for o in outs_p], axis=0)
    stack_s = lambda i: jnp.stack([o[i] for o in outs_s], axis=0)
    step = lambda a: a.reshape(depth, n_s, 1, -1)

    def heads_t(a):
        d0, n, _, t = a.shape
        return jnp.transpose(a.reshape(d0, n, N_GROUPS, HEAD_DIM, t), (0, 1, 4, 2, 3))

    ksb_p, vsb_p, kdf_p, vdf_p = kv_p
    return (xp, xs[:, None, :],
            stack_p(0), stack_s(0), step(stack_s(1)),
            heads_t(ksb_p), heads_t(vsb_p), heads(step(stack_s(2))), heads(step(stack_s(3))),
            heads_t(kdf_p), heads_t(vdf_p), heads(step(stack_s(4))), heads(step(stack_s(5))),
            heads_t(stack_p(1)), heads_t(stack_p(2)),
            stack_p(3), stack_s(6))
```

```python
import functools
import math
from typing import NamedTuple

import numpy as np
import jax
import jax.numpy as jnp
from jax import lax
from jax.experimental import pallas as pl
from jax.experimental.pallas import tpu as pltpu

D_MODEL = 1024
HEAD_DIM = 64
BR_WIDTH = 256
N_GROUPS = 4
DIFF_SUB = 32
CHUNK = 128
D_FF = 2816
EPS = 1e-6
NEG_BIG = -1e30
SB_UNDERFLOW_LOG2 = -150.0
LOG2E = 1.4426950408889634
VMEM_LIMIT_BYTES = 56 * 1024 * 1024

F32 = jnp.float32
BF16 = jnp.bfloat16


def _dot(a, b):
    return jnp.dot(a, b, preferred_element_type=F32)


def _dot_nt(a, b):
    return lax.dot_general(a, b, (((1,), (1,)), ((), ())), preferred_element_type=F32)


def _split_bf16(x):
    hi = x.astype(BF16)
    lo = (x - hi.astype(F32)).astype(BF16)
    return hi, lo


def _rms(x, g):
    ms = jnp.mean(x * x, axis=-1, keepdims=True)
    return x * lax.rsqrt(ms + EPS) * g


def _group_rms(z, g, ones_bd, group):
    ms = _dot((z * z).astype(BF16), ones_bd) * (1.0 / group)
    return z * lax.rsqrt(ms + EPS) * g


def _lane_iota(shape):
    return lax.broadcasted_iota(jnp.int32, shape, len(shape) - 1)


def _row_iota(shape):
    return lax.broadcasted_iota(jnp.int32, shape, len(shape) - 2)


def _div_pow2(x, d):
    assert d & (d - 1) == 0
    return lax.shift_right_logical(x, d.bit_length() - 1)


def _rem_pow2(x, d):
    assert d & (d - 1) == 0
    return x & (d - 1)


def _head_select(parts):
    lane = _lane_iota(parts[0].shape)
    out = parts[0]
    for h in range(1, N_GROUPS):
        out = jnp.where(lane >= HEAD_DIM * h, parts[h], out)
    return out


def _stack_heads(q, n_sub):
    lane = _lane_iota(q.shape)
    width = HEAD_DIM // n_sub
    zero = jnp.zeros_like(q)
    parts = []
    for c in range(n_sub):
        for h in range(N_GROUPS):
            lo = HEAD_DIM * h + width * c
            parts.append(jnp.where((lane >= lo) & (lane < lo + width), q, zero))
    return jnp.concatenate(parts, axis=0)


def _log_sigmoid_pair(z):
    lb = jnp.minimum(z, 0.0) - jnp.log(1.0 + jnp.exp(-jnp.abs(z)))
    return lb, lb - z


def _log2_sigmoid_pair(z2):
    lb = jnp.minimum(z2, 0.0) - jnp.log2(1.0 + jnp.exp2(-jnp.abs(z2)))
    return lb, lb - z2


def _lam_value(lp, lam_init):
    a = jnp.sum(lp[0:1] * lp[1:2], axis=1, keepdims=True)
    b = jnp.sum(lp[2:3] * lp[3:4], axis=1, keepdims=True)
    return jnp.exp(a) - jnp.exp(b) + lam_init


def _block_ones(group):
    i = np.arange(BR_WIDTH)
    return jnp.asarray((i[:, None] // group) == (i[None, :] // group), dtype=BF16)


def _suffix_ones(n):
    i = np.arange(n)
    l = (i[:, None] > i[None, :])
    return jnp.asarray(np.concatenate([l, l], axis=0), dtype=BF16)


def _alibi_slopes_log2():
    return [LOG2E * 2.0 ** (-8.0 * (h + 1) / N_GROUPS) for h in range(N_GROUPS)]


def _full_spec(shape):
    nd = len(shape)
    return pl.BlockSpec(shape, lambda *_: (0,) * nd, pipeline_mode=pl.Buffered(1))


class _Layered(NamedTuple):
    array: jax.Array
    layer: int

    @property
    def shape(self):
        return self.array.shape[1:]


def _const_spec(c):
    if isinstance(c, _Layered):
        rest = (0,) * len(c.shape)
        return pl.BlockSpec((None,) + c.shape, lambda *_: (c.layer,) + rest, pipeline_mode=pl.Buffered(1))
    return _full_spec(c.shape)


def _operand(c):
    return c.array if isinstance(c, _Layered) else c


def _params(*sem):
    return pltpu.CompilerParams(dimension_semantics=sem, vmem_limit_bytes=VMEM_LIMIT_BYTES)


def _group_rms_t(zt, g_col, ones_bd, group):
    hi, lo = _split_bf16(zt * zt)
    ms = (_dot(ones_bd, hi) + _dot(ones_bd, lo)) * (1.0 / group)
    return zt * lax.rsqrt(ms + EPS) * g_col


def _transpose_bf16(xt, eye):
    return _dot_nt(eye, xt)


def _memkv_kernel(mem_ref, g_ref, wt_ref, kg_ref, bd64_ref, eye_ref, k_ref, v_ref, k16_ref, v16_ref):
    hb = _rms(mem_ref[0], g_ref[...]).astype(BF16)
    kvt = _dot_nt(wt_ref[...], hb)
    k = _group_rms_t(kvt[:BR_WIDTH], kg_ref[...], bd64_ref[...], HEAD_DIM)
    v = kvt[BR_WIDTH:]
    k_ref[0] = k
    v_ref[0] = v
    k16_ref[0] = _transpose_bf16(k.astype(BF16), eye_ref[...]).astype(BF16)
    v16_ref[0] = v.astype(BF16)


def _mem_kv(mem, g, w, kg, bd64):
    n, m, d = mem.shape
    blk = pl.BlockSpec((1, BR_WIDTH, m), lambda i: (i, 0, 0))
    eye = jnp.asarray(np.eye(m), dtype=BF16)
    return pl.pallas_call(
        _memkv_kernel,
        grid=(n,),
        in_specs=[pl.BlockSpec((1, m, d), lambda i: (i, 0, 0)), _full_spec(g.shape), _full_spec(w.shape),
                  _full_spec(kg.shape), _full_spec(bd64.shape), _full_spec(eye.shape)],
        out_specs=[blk, blk, pl.BlockSpec((1, m, BR_WIDTH), lambda i: (i, 0, 0)), blk],
        out_shape=[jax.ShapeDtypeStruct((n, BR_WIDTH, m), F32)] * 2
        + [jax.ShapeDtypeStruct((n, m, BR_WIDTH), BF16), jax.ShapeDtypeStruct((n, BR_WIDTH, m), BF16)],
        compiler_params=_params("arbitrary"),
        name="mem_kv",
    )(mem, g, w, kg, bd64, eye)


def _layer_norm(x, g, b):
    mu = jnp.mean(x, axis=-1, keepdims=True)
    xc = x - mu
    return xc * lax.rsqrt(jnp.mean(xc * xc, axis=-1, keepdims=True) + EPS) * g + b


def _mem_attention(qm, mk, mvt, eye):
    r = qm.shape[0]
    st = _dot_nt(mk, _stack_heads(qm, 1))
    p = jnp.exp2(st - jnp.max(st, axis=0, keepdims=True))
    inv = 1.0 / jnp.sum(p, axis=0, keepdims=True)
    ot = _dot(mvt, p.astype(BF16))
    yt = jnp.concatenate([ot[HEAD_DIM * h:HEAD_DIM * (h + 1), h * r:(h + 1) * r] * inv[:, h * r:(h + 1) * r]
                          for h in range(N_GROUPS)], axis=0)
    return _transpose_bf16(yt.astype(BF16), eye)


def _inproj_kernel(x_ref, g1_ref, win_ref, wsbt_ref, wdft_ref, cw_ref, lng_ref, lnb_ref, ws_ref, gb_ref, dqg_ref,
                   dkg_ref, dkgr_ref, mqg_ref, bd32_ref, bd64_ref, eye_ref, mk_ref, mv_ref, *rest, tm, n_prev):
    prev_kv = rest[:4] if n_prev else ()
    (yabm_ref, qsb_ref, ksb16_ref, vsb16_ref, qdf_ref, kdf16_ref, vdf16_ref,
     ksb_ref, vsb_ref, kdf_ref, vdf_ref, ca_ref, carry_ref) = rest[len(prev_kv):]
    for prev_ref, out_ref in zip(prev_kv, (ksb_ref, vsb_ref, kdf_ref, vdf_ref)):
        for l in range(n_prev):
            out_ref[l, 0] = prev_ref[l, 0]
    t = pl.program_id(1)
    hb = _rms(x_ref[0], g1_ref[...]).astype(BF16)

    p_all = _dot(hb, win_ref[...])

    def proj(lo, hi):
        return p_all[:, lo:hi]

    pa = proj(0, 3 * BR_WIDTH)
    ach = pa[:, BR_WIDTH:2 * BR_WIDTH] * pa[:, 2 * BR_WIDTH:]

    @pl.when(t == 0)
    def _():
        carry_ref[...] = jnp.zeros_like(carry_ref)

    prev = carry_ref[...]
    p1, p2 = prev[7:8], prev[6:7]
    row = _row_iota(ach.shape)
    s1 = jnp.where(row == 0, p1, pltpu.roll(ach, 1, axis=0))
    s2 = jnp.where(row == 0, p2, jnp.where(row == 1, p1, pltpu.roll(ach, 2, axis=0)))
    cw = cw_ref[...]
    yabm_ref[0, :, 0:BR_WIDTH] = (pa[:, :BR_WIDTH] * (cw[0:1] * s2 + cw[1:2] * s1 + cw[2:3] * ach)).astype(BF16)
    carry_ref[...] = ach[tm - 8:tm]
    ca_ref[0] = ach[tm - 2:tm]

    pg = proj(3 * BR_WIDTH, 5 * BR_WIDTH)
    vn = _layer_norm(pg[:, BR_WIDTH:], lng_ref[...], lnb_ref[...])
    tril = _row_iota((CHUNK, CHUNK)) >= _lane_iota((CHUNK, CHUNK))
    wsm = [jnp.where(tril, ws_ref[g], 0.0).astype(BF16) for g in range(N_GROUPS)]
    gb = gb_ref[...]
    n_chunks = tm // CHUNK
    vc = jnp.concatenate([vn[c * CHUNK:(c + 1) * CHUNK] for c in range(n_chunks)], axis=1).astype(BF16)
    channel = _rem_pow2(_lane_iota(vc.shape), BR_WIDTH)
    mixed = _dot(wsm[0], vc)
    for g in range(1, N_GROUPS):
        mixed = jnp.where(channel >= HEAD_DIM * g, _dot(wsm[g], vc), mixed)
    for c in range(n_chunks):
        yabm_ref[0, c * CHUNK:(c + 1) * CHUNK, BR_WIDTH:2 * BR_WIDTH] = (
            pg[c * CHUNK:(c + 1) * CHUNK, :BR_WIDTH] * (mixed[:, c * BR_WIDTH:(c + 1) * BR_WIDTH] + gb)).astype(BF16)

    pq = proj(5 * BR_WIDTH, 10 * BR_WIDTH)
    qsb_ref[0] = (pq[:, :BR_WIDTH] * (HEAD_DIM ** -0.5 * LOG2E)).astype(BF16)
    kvt = _dot_nt(wsbt_ref[...], hb)
    ksb_ref[n_prev, 0] = kvt[:BR_WIDTH]
    vsb_ref[n_prev, 0] = kvt[BR_WIDTH:]
    ksb16_ref[0] = pq[:, BR_WIDTH:2 * BR_WIDTH].astype(BF16)
    vsb16_ref[0] = kvt[BR_WIDTH:].astype(BF16)

    bd32 = bd32_ref[...]
    qd = _group_rms(pq[:, 2 * BR_WIDTH:3 * BR_WIDTH], dqg_ref[...], bd32, DIFF_SUB)
    qdf_ref[0] = (qd * (DIFF_SUB ** -0.5 * LOG2E)).astype(BF16)
    kvt = _dot_nt(wdft_ref[...], hb)
    kd = _group_rms_t(kvt[:BR_WIDTH], dkg_ref[...], bd32, DIFF_SUB)
    kdf_ref[n_prev, 0] = kd
    vdf_ref[n_prev, 0] = kvt[BR_WIDTH:]
    kdf16_ref[0] = _group_rms(pq[:, 3 * BR_WIDTH:4 * BR_WIDTH], dkgr_ref[...], bd32, DIFF_SUB).astype(BF16)
    vdf16_ref[0] = kvt[BR_WIDTH:].astype(BF16)

    qm = _group_rms(pq[:, 4 * BR_WIDTH:], mqg_ref[...], bd64_ref[...], HEAD_DIM)
    qm = (qm * (HEAD_DIM ** -0.5 * LOG2E)).astype(BF16)
    yabm_ref[0, :, 2 * BR_WIDTH:] = _mem_attention(qm, mk_ref[0], mv_ref[0], eye_ref[...]).astype(BF16)


def _inproj(x, p, mk16, mv16, tm, prev_kv):
    n, t, d = x.shape
    n_prev = prev_kv[0].shape[0] if prev_kv else 0
    prev_kv = list(prev_kv) if prev_kv else []
    stacked = lambda layers: pl.BlockSpec((layers, 1, BR_WIDTH, tm), lambda i, j: (0, i, 0, j))
    consts = [p["norm1_g"], p["w_in_tok"], p["w_sb_kv_t"], p["w_df_kv_t"], p["conv_a_w"], p["gmlp_ln_g"], p["gmlp_ln_b"],
              p["gmlp_ws"], p["gmlp_b_full"], p["diff_qn_g"], p["diff_kn_g_col"], p["diff_kn_g"], p["mem_qn_g"],
              p["bd32"], p["bd64"], jnp.asarray(np.eye(tm), dtype=BF16)]
    seq = lambda w: pl.BlockSpec((1, tm, w), lambda i, j: (i, j, 0))
    seq_t = pl.BlockSpec((1, BR_WIDTH, tm), lambda i, j: (i, 0, j))
    mem_spec = lambda a: pl.BlockSpec((1,) + a.shape[1:], lambda i, j: (i, 0, 0))
    bf = lambda w: jax.ShapeDtypeStruct((n, t, w), BF16)
    bf_t = jax.ShapeDtypeStruct((n, BR_WIDTH, t), BF16)
    fl_t = jax.ShapeDtypeStruct((n_prev + 1, n, BR_WIDTH, t), F32)
    return pl.pallas_call(
        functools.partial(_inproj_kernel, tm=tm, n_prev=n_prev),
        grid=(n, t // tm),
        in_specs=[seq(d)] + [_const_spec(c) for c in consts] + [mem_spec(mk16), mem_spec(mv16)]
        + [stacked(n_prev)] * len(prev_kv),
        out_specs=[seq(3 * BR_WIDTH), seq(BR_WIDTH), seq(BR_WIDTH), seq_t, seq(BR_WIDTH), seq(BR_WIDTH), seq_t]
        + [stacked(n_prev + 1)] * 4 + [pl.BlockSpec((1, 2, BR_WIDTH), lambda i, j: (i, 0, 0))],
        out_shape=[bf(3 * BR_WIDTH), bf(BR_WIDTH), bf(BR_WIDTH), bf_t, bf(BR_WIDTH), bf(BR_WIDTH), bf_t] + [fl_t] * 4
        + [jax.ShapeDtypeStruct((n, 2, BR_WIDTH), F32)],
        scratch_shapes=[pltpu.VMEM((8, BR_WIDTH), F32)],
        compiler_params=_params("arbitrary", "arbitrary"),
        name="in_proj",
    )(x, *map(_operand, consts), mk16, mv16, *prev_kv)


def _sb_kernel(q_ref, k_ref, v_ref, uu_ref, eye_ref, o_ref, qm_ref, carry_ref, acc_ref, *, tb):
    qi = pl.program_id(1)
    q = q_ref[0]
    lane = _lane_iota(q.shape)
    for h in range(N_GROUPS):
        in_head = (lane >= HEAD_DIM * h) & (lane < HEAD_DIM * (h + 1))
        qm_ref[h * tb:(h + 1) * tb] = jnp.where(in_head, q, jnp.zeros_like(q))
    strictly_before = _row_iota((tb, tb)) < _lane_iota((tb, tb))
    heads = range(N_GROUPS)

    def block(kb, first):
        start = pl.multiple_of(kb * tb, tb)
        z_all = _dot_nt(k_ref[0, pl.ds(start, tb), :], qm_ref[...])
        lbs, l1ms = [], []
        for h in heads:
            lb, l1m = _log2_sigmoid_pair(z_all[:, h * tb:(h + 1) * tb])
            lbs.append(lb)
            l1ms.append(jnp.where(strictly_before, l1m, 0.0) if first else l1m)
        between = _dot(uu_ref[...], jnp.concatenate(l1ms, axis=1).astype(BF16))
        for h in heads:
            total = lbs[h] + between[:, h * tb:(h + 1) * tb]
            if not first:
                total = total + carry_ref[h:h + 1]
            a = jnp.exp2(total)
            if first:
                a = jnp.where(strictly_before, a, 0.0)
            pv = _dot(v_ref[0, HEAD_DIM * h:HEAD_DIM * (h + 1), pl.ds(start, tb)], a.astype(BF16))
            block_sum = jnp.sum(l1ms[h], axis=0, keepdims=True)
            if first:
                acc_ref[h] = pv
                carry_ref[h:h + 1] = block_sum
            else:
                acc_ref[h] += pv
                carry_ref[h:h + 1] += block_sum

    block(qi, True)

    def cond(state):
        i, live = state
        return (i < qi) & (live > SB_UNDERFLOW_LOG2)

    def body(state):
        i, _ = state
        block(qi - 1 - i, False)
        return i + 1, jnp.max(carry_ref[...])

    lax.while_loop(cond, body, (jnp.int32(0), jnp.max(carry_ref[...])))
    yt = jnp.concatenate([acc_ref[h] for h in heads], axis=0)
    o_ref[0] = _dot_nt(eye_ref[...], yt.astype(BF16)).astype(BF16)


def _sb_attention(q, k, v, tb):
    n, t, w = q.shape
    i = np.arange(tb)
    uu = jnp.asarray(i[None, :] > i[:, None], dtype=BF16)
    eye = jnp.asarray(np.eye(tb), dtype=BF16)
    return pl.pallas_call(
        functools.partial(_sb_kernel, tb=tb),
        grid=(n, t // tb),
        in_specs=[pl.BlockSpec((1, tb, w), lambda i, j: (i, j, 0)),
                  pl.BlockSpec((1, t, w), lambda i, j: (i, 0, 0)),
                  pl.BlockSpec((1, w, t), lambda i, j: (i, 0, 0)),
                  _full_spec(uu.shape), _full_spec(eye.shape)],
        out_specs=pl.BlockSpec((1, tb, w), lambda i, j: (i, j, 0)),
        out_shape=jax.ShapeDtypeStruct((n, t, w), BF16),
        scratch_shapes=[pltpu.VMEM((N_GROUPS * tb, w), BF16), pltpu.VMEM((N_GROUPS, tb), F32),
                        pltpu.VMEM((N_GROUPS, HEAD_DIM, tb), F32)],
        compiler_params=_params("arbitrary", "arbitrary"),
        name="sb_attn",
    )(q, k, v, uu, eye)


def _diff_kernel(q_ref, k_ref, v_ref, lp_ref, og_ref, bd64_ref, eye_ref, o_ref,
                 qm_ref, bias_ref, m_ref, l_ref, acc_ref, *, tb, lam_init):
    qi = pl.program_id(1)
    slopes = _alibi_slopes_log2()
    rel = _row_iota((tb, tb)) - _lane_iota((tb, tb))

    @pl.when(qi == 0)
    def _():
        for h in range(N_GROUPS):
            bias_ref[h] = slopes[h] * rel.astype(F32)

    q = q_ref[0]
    lane = _lane_iota(q.shape)
    for c in range(2):
        for h in range(N_GROUPS):
            lo = HEAD_DIM * h + DIFF_SUB * c
            ch = c * N_GROUPS + h
            qm_ref[ch * tb:(ch + 1) * tb] = jnp.where((lane >= lo) & (lane < lo + DIFF_SUB), q, jnp.zeros_like(q))

    def scores(kb):
        start = pl.multiple_of(kb * tb, tb)
        return _dot_nt(k_ref[0, pl.ds(start, tb), :], qm_ref[...])

    def block(kb, score_tile, first):
        start = pl.multiple_of(kb * tb, tb)
        off = ((kb - qi) * tb).astype(F32)
        probs, alphas = [], []
        for c in range(2):
            for h in range(N_GROUPS):
                ch = c * N_GROUPS + h
                s = score_tile(ch) + bias_ref[h]
                shift = slopes[h] * off
                if first:
                    s = jnp.where(rel <= 0, s, NEG_BIG)
                    m_new = jnp.max(s, axis=0, keepdims=True) + shift
                    p = jnp.exp2(s - (m_new - shift))
                    l_ref[ch:ch + 1] = jnp.sum(p, axis=0, keepdims=True)
                    alphas.append(None)
                else:
                    m_old = m_ref[ch:ch + 1]
                    m_new = jnp.maximum(m_old, jnp.max(s, axis=0, keepdims=True) + shift)
                    alpha = jnp.exp2(m_old - m_new)
                    p = jnp.exp2(s - (m_new - shift))
                    l_ref[ch:ch + 1] = alpha * l_ref[ch:ch + 1] + jnp.sum(p, axis=0, keepdims=True)
                    alphas.append(alpha)
                m_ref[ch:ch + 1] = m_new
                probs.append(p.astype(BF16))
        for h in range(N_GROUPS):
            vh = v_ref[0, HEAD_DIM * h:HEAD_DIM * (h + 1), pl.ds(start, tb)]
            pv = _dot(vh, jnp.concatenate([probs[h], probs[N_GROUPS + h]], axis=1))
            for c in range(2):
                ch = c * N_GROUPS + h
                new = pv[:, c * tb:(c + 1) * tb]
                acc_ref[ch] = new if first else alphas[ch] * acc_ref[ch] + new

    def run_block(kb, first):
        s_all = scores(kb)
        block(kb, lambda ch: s_all[:, ch * tb:(ch + 1) * tb], first)

    run_block(qi, True)

    def body(i, _):
        run_block(qi - 1 - i, False)
        return 0

    lax.fori_loop(0, qi, body, 0)
    lam = _lam_value(lp_ref[...], lam_init)
    ys = []
    for h in range(N_GROUPS):
        y0 = acc_ref[h] * (1.0 / l_ref[h:h + 1])
        y1 = acc_ref[N_GROUPS + h] * (lam / l_ref[N_GROUPS + h:N_GROUPS + h + 1])
        ys.append(y0 - y1)
    yt = _group_rms_t(jnp.concatenate(ys, axis=0), og_ref[...], bd64_ref[...], HEAD_DIM) * (1.0 - lam_init)
    o_ref[0] = _dot_nt(eye_ref[...], yt.astype(BF16)).astype(BF16)


def _diff_attention(q, k, v, lp, og_col, bd64, lam_init, tb):
    n, t, w = q.shape
    maps = 2 * N_GROUPS
    eye = jnp.asarray(np.eye(tb), dtype=BF16)
    return pl.pallas_call(
        functools.partial(_diff_kernel, tb=tb, lam_init=lam_init),
        grid=(n, t // tb),
        in_specs=[pl.BlockSpec((1, tb, w), lambda i, j: (i, j, 0)),
                  pl.BlockSpec((1, t, w), lambda i, j: (i, 0, 0)),
                  pl.BlockSpec((1, w, t), lambda i, j: (i, 0, 0)),
                  _full_spec(lp.shape), _full_spec(og_col.shape), _full_spec(bd64.shape), _full_spec(eye.shape)],
        out_specs=pl.BlockSpec((1, tb, w), lambda i, j: (i, j, 0)),
        out_shape=jax.ShapeDtypeStruct((n, t, w), BF16),
        scratch_shapes=[pltpu.VMEM((maps * tb, w), BF16), pltpu.VMEM((N_GROUPS, tb, tb), F32),
                        pltpu.VMEM((maps, tb), F32), pltpu.VMEM((maps, tb), F32),
                        pltpu.VMEM((maps, HEAD_DIM, tb), F32)],
        compiler_params=_params("arbitrary", "arbitrary"),
        name="diff_attn",
    )(q, k, v, lp, og_col, bd64, eye)


def _merge_kernel(x_ref, yabm_ref, yc_ref, yd_ref, g1_ref, wg_ref, bg_ref, wb_ref, wo_ref, o_ref):
    x = x_ref[0]
    hb = _rms(x, g1_ref[...]).astype(BF16)
    ys = [yabm_ref[0, :, 0:BR_WIDTH], yabm_ref[0, :, BR_WIDTH:2 * BR_WIDTH], yc_ref[0], yd_ref[0],
          yabm_ref[0, :, 2 * BR_WIDTH:]]
    acc = None
    for b in range(5):
        gate = jax.nn.sigmoid(_dot(hb, wg_ref[:, b * D_MODEL:(b + 1) * D_MODEL]) + bg_ref[:, b * D_MODEL:(b + 1) * D_MODEL])
        term = gate * _dot(ys[b], wb_ref[b])
        acc = term if acc is None else acc + term
    o_ref[0] = x + _dot(acc.astype(BF16), wo_ref[...])


def _merge(x, yabm, yc, yd, p, tm):
    n, t, d = x.shape
    consts = [p["norm1_g"], p["w_gate"], p["b_gate"], p["w_branch"], p["w_o"]]
    seq = lambda w: pl.BlockSpec((1, tm, w), lambda i, j: (i, j, 0))
    return pl.pallas_call(
        _merge_kernel,
        grid=(n, t // tm),
        in_specs=[seq(d), seq(3 * BR_WIDTH), seq(BR_WIDTH), seq(BR_WIDTH)] + [_const_spec(c) for c in consts],
        out_specs=seq(d),
        out_shape=jax.ShapeDtypeStruct((n, t, d), F32),
        compiler_params=_params("arbitrary", "arbitrary"),
        name="merge",
    )(x, yabm, yc, yd, *map(_operand, consts))


SUBLANES = 8
MXU_WIDTH = 256
FFN_CHUNKS = ((0, 6 * MXU_WIDTH), (6 * MXU_WIDTH, D_FF - 6 * MXU_WIDTH))


def _ffn_kernel(x_ref, g2_ref, wup_ref, cw_ref, cb_ref, wdn_ref, o_ref, st_ref, carry_ref, *, tm):
    t = pl.program_id(1)
    x = x_ref[0]
    hb = _rms(x, g2_ref[...]).astype(BF16)

    @pl.when(t == 0)
    def _():
        carry_ref[...] = jnp.zeros_like(carry_ref)

    def conv(lo, width):
        cols = slice(lo, lo + width)
        row = _row_iota((tm, width))
        u = _dot(hb, wup_ref[:, cols])
        prev = carry_ref[:, cols]
        p1, p2 = prev[7:8], prev[6:7]
        s1 = jnp.where(row == 0, p1, pltpu.roll(u, 1, axis=0))
        s2 = jnp.where(row == 0, p2, jnp.where(row == 1, p1, pltpu.roll(u, 2, axis=0)))
        carry_ref[:, cols] = u[tm - 8:tm]
        st_ref[0, :, cols] = u[tm - 2:tm]
        return cw_ref[0:1, cols] * s2 + cw_ref[1:2, cols] * s1 + cw_ref[2:3, cols] * u + cb_ref[:, cols]

    acc = x
    for lo, width in FFN_CHUNKS:
        gate = conv(lo, width)
        val = conv(D_FF + lo, width)
        act = (gate * jax.nn.sigmoid(gate) * val).astype(BF16)
        acc = acc + _dot(act, wdn_ref[lo:lo + width, :])
    o_ref[0] = acc


def _ffn(x, p, tm):
    n, t, d = x.shape
    consts = [p["norm2_g"], p["w_up"], p["conv_ffn_w"], p["conv_ffn_b"], p["w_down"]]
    seq = pl.BlockSpec((1, tm, d), lambda i, j: (i, j, 0))
    return pl.pallas_call(
        functools.partial(_ffn_kernel, tm=tm),
        grid=(n, t // tm),
        in_specs=[seq] + [_const_spec(c) for c in consts],
        out_specs=[seq, pl.BlockSpec((1, 2, 2 * D_FF), lambda i, j: (i, 0, 0))],
        out_shape=[jax.ShapeDtypeStruct((n, t, d), F32), jax.ShapeDtypeStruct((n, 2, 2 * D_FF), F32)],
        scratch_shapes=[pltpu.VMEM((8, 2 * D_FF), F32)],
        compiler_params=_params("arbitrary", "arbitrary"),
        name="conv_ffn",
    )(x, *map(_operand, consts))


def _ffn_step_kernel(x_ref, p0_ref, p1_ref, g2_ref, wup_ref, cw_ref, cb_ref, wdn_ref, o_ref, u_ref):
    x = x_ref[...]
    hb = _rms(x, g2_ref[...]).astype(BF16)

    def conv(lo, width):
        cols = slice(lo, lo + width)
        u = _dot(hb, wup_ref[:, cols])
        u_ref[:, cols] = u
        return (cw_ref[0:1, cols] * p0_ref[:, cols] + cw_ref[1:2, cols] * p1_ref[:, cols]
                + cw_ref[2:3, cols] * u + cb_ref[:, cols])

    acc = x
    for lo, width in FFN_CHUNKS:
        gate = conv(lo, width)
        val = conv(D_FF + lo, width)
        act = (gate * jax.nn.sigmoid(gate) * val).astype(BF16)
        acc = acc + _dot(act, wdn_ref[lo:lo + width, :])
    o_ref[...] = acc


def _ffn_step(x, prev0, prev1, p):
    m, d = x.shape
    consts = [p["norm2_g"], p["w_up"], p["conv_ffn_w"], p["conv_ffn_b"], p["w_down"]]
    args = [x, prev0, prev1] + consts
    return pl.pallas_call(
        _ffn_step_kernel,
        grid=(1,),
        in_specs=[_const_spec(a) for a in args],
        out_specs=[_full_spec((m, d)), _full_spec((m, 2 * D_FF))],
        out_shape=[jax.ShapeDtypeStruct((m, d), F32), jax.ShapeDtypeStruct((m, 2 * D_FF), F32)],
        compiler_params=_params("arbitrary"),
        name="conv_ffn_step",
    )(*map(_operand, args))


def _inproj_step_kernel(x_ref, c0_ref, c1_ref, g1_ref, win_ref, cw_ref, lng_ref, lnb_ref, ws_ref, gb_ref,
                        dqg_ref, dkg_ref, mqg_ref, bd32_ref, bd64_ref,
                        yab_ref, q_ref, ksb_ref, vsb_ref, kdf_ref, vdf_ref, vn_ref, ach_ref):
    hb = _rms(x_ref[...], g1_ref[...]).astype(BF16)

    def proj(lo, hi):
        return _dot(hb, win_ref[:, lo:hi])

    pa = proj(0, 3 * BR_WIDTH)
    ach = pa[:, BR_WIDTH:2 * BR_WIDTH] * pa[:, 2 * BR_WIDTH:]
    cw = cw_ref[...]
    yab_ref[:, 0:BR_WIDTH] = (pa[:, :BR_WIDTH] * (cw[0:1] * c0_ref[...] + cw[1:2] * c1_ref[...] + cw[2:3] * ach)).astype(BF16)
    ach_ref[...] = ach

    pg = proj(3 * BR_WIDTH, 5 * BR_WIDTH)
    vn = _layer_norm(pg[:, BR_WIDTH:], lng_ref[...], lnb_ref[...])
    vn_ref[...] = vn
    lane = _lane_iota((1, BR_WIDTH))
    w00 = jnp.zeros((1, BR_WIDTH), F32)
    for g in range(N_GROUPS):
        w00 = jnp.where((lane >= HEAD_DIM * g) & (lane < HEAD_DIM * (g + 1)), ws_ref[g, 0:1, 0:1], w00)
    yab_ref[:, BR_WIDTH:] = (pg[:, :BR_WIDTH] * (w00 * vn + gb_ref[0:1, :])).astype(BF16)

    ps = proj(5 * BR_WIDTH, 8 * BR_WIDTH)
    q_ref[:, 0:BR_WIDTH] = (ps[:, :BR_WIDTH] * (HEAD_DIM ** -0.5)).astype(BF16)
    ksb_ref[...] = ps[:, BR_WIDTH:2 * BR_WIDTH]
    vsb_ref[...] = ps[:, 2 * BR_WIDTH:]

    pd = proj(8 * BR_WIDTH, 11 * BR_WIDTH)
    bd32 = bd32_ref[...]
    qd = _group_rms(pd[:, :BR_WIDTH], dqg_ref[...], bd32, DIFF_SUB)
    q_ref[:, BR_WIDTH:2 * BR_WIDTH] = (qd * (DIFF_SUB ** -0.5 * LOG2E)).astype(BF16)
    kdf_ref[...] = _group_rms(pd[:, BR_WIDTH:2 * BR_WIDTH], dkg_ref[...], bd32, DIFF_SUB)
    vdf_ref[...] = pd[:, 2 * BR_WIDTH:]

    qm = _group_rms(proj(11 * BR_WIDTH, 12 * BR_WIDTH), mqg_ref[...], bd64_ref[...], HEAD_DIM)
    q_ref[:, 2 * BR_WIDTH:] = (qm * (HEAD_DIM ** -0.5 * LOG2E)).astype(BF16)


def _inproj_step(x, c0, c1, p):
    m, d = x.shape
    consts = [p["norm1_g"], p["w_in"], p["conv_a_w"], p["gmlp_ln_g"], p["gmlp_ln_b"], p["gmlp_ws"],
              p["gmlp_b_full"], p["diff_qn_g"], p["diff_kn_g"], p["mem_qn_g"], p["bd32"], p["bd64"]]
    args = [x, c0, c1] + consts
    fl = jax.ShapeDtypeStruct((m, BR_WIDTH), F32)
    out_shape = [jax.ShapeDtypeStruct((m, 2 * BR_WIDTH), BF16), jax.ShapeDtypeStruct((m, 3 * BR_WIDTH), BF16)] + [fl] * 6
    return pl.pallas_call(
        _inproj_step_kernel,
        grid=(1,),
        in_specs=[_const_spec(a) for a in args],
        out_specs=[_full_spec(s.shape) for s in out_shape],
        out_shape=out_shape,
        compiler_params=_params("arbitrary"),
        name="in_proj_step",
    )(*map(_operand, args))


def _decode_kernel(pt_ref, q_ref, kn_ref, vn_ref, mk_ref, mv_ref, lp_ref, og_ref, bd64_ref, ll_ref, ones_ref,
                   ksb_hbm, vsb_hbm, kdf_hbm, vdf_hbm, o_ref, buf_ref, sem_ref, *, layer, n_samples, n_pages, page,
                   lam_init):
    step = pl.program_id(0)
    pools = (ksb_hbm, vsb_hbm, kdf_hbm, vdf_hbm)

    def page_copy(sample, slot, c, j):
        return pltpu.make_async_copy(pools[c].at[layer, pt_ref[sample, j]], buf_ref.at[slot, c, j], sem_ref.at[slot])

    def for_each_page(sample, slot, fn):
        for c in range(len(pools)):
            for j in range(n_pages):
                fn(page_copy(sample, slot, c, j))

    ahead = DECODE_SLOTS - 1

    @pl.when(step == 0)
    def _():
        for s in range(min(ahead, n_samples)):
            for_each_page(s, s, lambda cp: cp.start())

    @pl.when(step + ahead < n_samples)
    def _():
        for_each_page(step + ahead, lax.rem(step + ahead, DECODE_SLOTS), lambda cp: cp.start())

    slot = lax.rem(step, DECODE_SLOTS)
    for_each_page(step, slot, lambda cp: cp.wait())
    ksb, vsb, kdf, vdf = ([buf_ref.at[slot, c, j] for j in range(n_pages)] for c in range(len(pools)))
    rows = 2 * N_GROUPS
    n_sub = HEAD_DIM // DIFF_SUB
    ones = ones_ref[...]

    def q_column(col):
        qrow = jnp.broadcast_to(q_ref[0, :, col * BR_WIDTH:(col + 1) * BR_WIDTH].astype(F32), (BR_WIDTH, BR_WIDTH))
        diag = jnp.where(_row_iota(qrow.shape) == _lane_iota(qrow.shape), qrow, 0.0).astype(BF16)
        return _dot(diag, ones)

    def group_scores(kt, qcol):
        w = kt.shape[1]
        part = jnp.sum((kt * qcol).reshape(rows, DIFF_SUB // SUBLANES, SUBLANES, w), axis=1)
        tiles = [part[g] for g in range(rows)]
        sub = _row_iota((SUBLANES, w))
        for k in (4, 2, 1):
            upper = (sub & k) != 0
            half = len(tiles) // 2
            nxt = []
            for a in range(half):
                x, y = tiles[a], tiles[a + half]
                keep = jnp.where(upper, y, x)
                send = jnp.where(upper, x, y)
                down = pltpu.roll(send, k, axis=0)
                partner = down if 2 * k == SUBLANES else jnp.where(upper, down,
                                                                   pltpu.roll(send, SUBLANES - k, axis=0))
                nxt.append(keep + partner)
            tiles = nxt
        return tiles[0]

    def pair_sum(z):
        r = z.shape[0]
        even = (_row_iota(z.shape) & 1) == 0
        return z + jnp.where(even, pltpu.roll(z, r - 1, axis=0), pltpu.roll(z, 1, axis=0))

    def head_rows(w):
        return jnp.concatenate([jnp.broadcast_to(w[n_sub * h:n_sub * h + 1], (HEAD_DIM, w.shape[1]))
                                for h in range(N_GROUPS)], axis=0)

    def reduce_positions(acc):
        hi, lo = _split_bf16(acc)
        ones_row = jnp.ones((rows, acc.shape[1]), BF16)
        return _dot_nt(ones_row, hi) + _dot_nt(ones_row, lo)

    qc = q_column(0)
    z = jnp.concatenate([pair_sum(group_scores(ksb[pg][...], qc)) for pg in range(n_pages)], axis=0)
    lb, l1m = _log_sigmoid_pair(z)
    hi, lo = _split_bf16(l1m)
    hl = jnp.concatenate([hi, lo], axis=1)
    within = _dot(hl, ll_ref[...])
    total = _dot(hl, ones)
    carry = jnp.zeros((rows, page), F32)
    carries = [None] * n_pages
    for pg in reversed(range(n_pages)):
        carries[pg] = carry
        carry = carry + total[rows * pg:rows * (pg + 1)]
    a = jnp.exp(lb + within + jnp.concatenate(carries, axis=0))
    acc = jnp.zeros((BR_WIDTH, page), F32)
    for pg in range(n_pages):
        acc = acc + head_rows(a[rows * pg:rows * (pg + 1)]) * vsb[pg][...]
    y_c = reduce_positions(acc)[0:1]

    qc = q_column(1)
    s = jnp.concatenate([group_scores(kdf[pg][...], qc) for pg in range(n_pages)], axis=0)
    slopes = _alibi_slopes_log2()
    all_rows = _row_iota((rows * n_pages, page))
    head = _div_pow2(_rem_pow2(all_rows, rows), n_sub)
    sl = jnp.full(all_rows.shape, slopes[0], F32)
    for h in range(1, N_GROUPS):
        sl = jnp.where(head == h, slopes[h], sl)
    past = n_pages * page
    k_pos = _div_pow2(all_rows, rows) * page + _lane_iota(all_rows.shape)
    s = s + sl * (k_pos - past).astype(F32)
    row8 = _row_iota((rows, BR_WIDTH))
    group_lanes = _div_pow2(_lane_iota((rows, BR_WIDTH)), DIFF_SUB) == row8
    q8 = jnp.broadcast_to(q_ref[0, :, BR_WIDTH:2 * BR_WIDTH].astype(F32), (rows, BR_WIDTH))
    kn8 = jnp.broadcast_to(kn_ref[0], (rows, BR_WIDTH))
    s_new = jnp.sum(jnp.where(group_lanes, q8 * kn8, 0.0), axis=1, keepdims=True)
    m8 = s[0:rows]
    for pg in range(1, n_pages):
        m8 = jnp.maximum(m8, s[rows * pg:rows * (pg + 1)])
    m = jnp.maximum(jnp.max(m8, axis=1, keepdims=True), s_new)
    p_new = jnp.exp2(s_new - m)
    m_all = jnp.concatenate([jnp.broadcast_to(m, (rows, page))] * n_pages, axis=0)
    p = jnp.exp2(s - m_all)
    l8 = p[0:rows]
    for pg in range(1, n_pages):
        l8 = l8 + p[rows * pg:rows * (pg + 1)]
    l = jnp.sum(l8, axis=1, keepdims=True) + p_new
    lam = _lam_value(lp_ref[...], lam_init)
    first_map = (_row_iota((rows, 1)) & 1) == 0
    coef = jnp.where(first_map, 1.0, -lam) / l
    w = pair_sum(p * jnp.concatenate([jnp.broadcast_to(coef, (rows, page))] * n_pages, axis=0))
    acc = jnp.zeros((BR_WIDTH, page), F32)
    for pg in range(n_pages):
        acc = acc + head_rows(w[rows * pg:rows * (pg + 1)]) * vdf[pg][...]
    w_new = pair_sum(p_new * coef)
    head_lanes = (_div_pow2(_lane_iota((rows, BR_WIDTH)), HEAD_DIM) * n_sub) == row8
    vn8 = jnp.broadcast_to(vn_ref[0], (rows, BR_WIDTH))
    y_new = jnp.sum(jnp.where(head_lanes, w_new * vn8, 0.0), axis=0, keepdims=True)
    y_d = reduce_positions(acc)[0:1] + y_new
    y_d = _group_rms(jnp.broadcast_to(y_d, (rows, BR_WIDTH)), og_ref[...], bd64_ref[...], HEAD_DIM)[0:1]
    y_d = y_d * (1.0 - lam_init)

    qc = q_column(2)
    n_mem = mk_ref.shape[2]
    qc = jnp.concatenate([qc] * (n_mem // page), axis=1)
    s = pair_sum(group_scores(mk_ref[0], qc))
    p = jnp.exp2(s - jnp.max(s, axis=1, keepdims=True))
    p = p / jnp.sum(p, axis=1, keepdims=True)
    y_m = reduce_positions(head_rows(p) * mv_ref[0])[0:1]

    o_ref[0, :, 0:BR_WIDTH] = y_c.astype(BF16)
    o_ref[0, :, BR_WIDTH:2 * BR_WIDTH] = y_d.astype(BF16)
    o_ref[0, :, 2 * BR_WIDTH:] = y_m.astype(BF16)


def _decode_attention(layer, page_table, q, k_new, v_new, caches, mem_k, mem_v, p, lam_init):
    b = q.shape[0]
    n_pages = page_table.shape[1]
    page = caches[0].shape[3]
    ll = _suffix_ones(page)
    row_spec = lambda w: pl.BlockSpec((1, 1, w), lambda i, pt: (i, 0, 0))
    mem_spec = pl.BlockSpec((None, 1) + mem_k.shape[2:], lambda i, pt: (layer, i, 0, 0))
    const = lambda a: pl.BlockSpec(a.shape, lambda i, pt: (0,) * a.ndim)
    consts = [p["diff_lambda"], p["diff_out_g"], p["bd64"], ll, jnp.ones((BR_WIDTH, page), BF16)]
    grid_spec = pltpu.PrefetchScalarGridSpec(
        num_scalar_prefetch=1,
        grid=(b,),
        in_specs=[row_spec(3 * BR_WIDTH), row_spec(BR_WIDTH), row_spec(BR_WIDTH), mem_spec, mem_spec]
        + [const(c) for c in consts] + [pl.BlockSpec(memory_space=pl.ANY)] * len(caches),
        out_specs=row_spec(3 * BR_WIDTH),
        scratch_shapes=[pltpu.VMEM((DECODE_SLOTS, len(caches), n_pages, BR_WIDTH, page), F32),
                        pltpu.SemaphoreType.DMA((DECODE_SLOTS,))],
    )
    return pl.pallas_call(
        functools.partial(_decode_kernel, layer=layer, n_samples=b, n_pages=n_pages, page=page, lam_init=lam_init),
        grid_spec=grid_spec,
        out_shape=jax.ShapeDtypeStruct((b, 1, 3 * BR_WIDTH), BF16),
        compiler_params=_params("arbitrary"),
        name="decode_attn",
    )(page_table, q, k_new, v_new, mem_k, mem_v, *consts, *caches)


PROMPT_TILE = 512
DECODE_SLOTS = 3
ATTN_BLOCK = 256


def _cast_weights(w):
    big = {k: w[k].astype(BF16) for k in ("w_in", "w_gate", "w_branch", "w_o", "w_up", "w_down")}
    w_in = big["w_in"]
    big["w_in_tok"] = jnp.concatenate([w_in[:, :, :7 * BR_WIDTH], w_in[:, :, 8 * BR_WIDTH:10 * BR_WIDTH],
                                       w_in[:, :, 11 * BR_WIDTH:]], axis=2)
    return big


def _layer_params(l, w, big):
    row = lambda a: a[l].reshape(1, -1).astype(F32)
    tile4 = lambda a: jnp.tile(a[l].astype(F32), N_GROUPS).reshape(1, -1)
    col4 = lambda a: jnp.tile(a[l].astype(F32), N_GROUPS).reshape(-1, 1)
    w_in = big["w_in"][l]
    return dict(
        norm1_g=row(w["norm1_g"]), w_in=_Layered(big["w_in"], l), conv_a_w=w["conv_a_w"][l].astype(F32),
        w_sb_kv_t=w_in[:, 6 * BR_WIDTH:8 * BR_WIDTH].T, w_df_kv_t=w_in[:, 9 * BR_WIDTH:11 * BR_WIDTH].T,
        w_in_tok=_Layered(big["w_in_tok"], l),
        diff_kn_g_col=col4(w["diff_kn_g"]), mem_kn_g_col=col4(w["mem_kn_g"]), diff_out_g_col=col4(w["diff_out_g"]),
        w_mem_kv_t=w["w_mem_kv"][l].astype(BF16).T,
        gmlp_ln_g=row(w["gmlp_ln_g"]), gmlp_ln_b=row(w["gmlp_ln_b"]), gmlp_ws=w["gmlp_ws"][l].astype(F32),
        gmlp_b_full=jnp.repeat(w["gmlp_b"][l].astype(F32).T, HEAD_DIM, axis=1),
        diff_qn_g=tile4(w["diff_qn_g"]), diff_kn_g=tile4(w["diff_kn_g"]), diff_out_g=tile4(w["diff_out_g"]),
        diff_lambda=w["diff_lambda"][l].astype(F32),
        mem_norm_g=row(w["mem_norm_g"]), mem_qn_g=tile4(w["mem_qn_g"]),
        w_branch=_Layered(big["w_branch"], l), w_gate=_Layered(big["w_gate"], l), b_gate=row(w["b_gate"]),
        w_o=_Layered(big["w_o"], l), norm2_g=row(w["norm2_g"]), w_up=_Layered(big["w_up"], l),
        conv_ffn_w=w["conv_ffn_w"][l].astype(F32), conv_ffn_b=row(w["conv_ffn_b"]),
        w_down=_Layered(big["w_down"], l),
        bd32=_block_ones(DIFF_SUB), bd64=_block_ones(HEAD_DIM),
    )


def _prompt_layer(x, mem, p, lam_init, prev_kv):
    n, t, _ = x.shape
    tm = min(PROMPT_TILE, t)
    tb = min(ATTN_BLOCK, t)
    mk, mv, mk16, mv16 = _mem_kv(mem, p["mem_norm_g"], p["w_mem_kv_t"], p["mem_kn_g_col"], p["bd64"])
    (yabm, qsb, ksb16, vsb16, qdf, kdf16, vdf16, *kv, ca) = _inproj(x, p, mk16, mv16, tm, prev_kv)
    yc = _sb_attention(qsb, ksb16, vsb16, tb)
    yd = _diff_attention(qdf, kdf16, vdf16, p["diff_lambda"], p["diff_out_g_col"], p["bd64"], lam_init, tb)
    x1 = _merge(x, yabm, yc, yd, p, tm)
    x2, cf = _ffn(x1, p, tm)
    return x2, kv, (ca, mk, mv, cf)


def _sample_layer(l, x, conv_a, conv_ffn, page_table, caches, mem_k, mem_v, p, lam_init):
    b = x.shape[0]
    yab, q, ksb, vsb, kdf, vdf, vn, ach = _inproj_step(x, conv_a[:, 0], conv_a[:, 1], p)
    row3 = lambda a: a.reshape(b, 1, -1)
    ycdm = _decode_attention(l, page_table, row3(q), row3(kdf), row3(vdf), caches, mem_k, mem_v, p, lam_init)
    yabm = jnp.concatenate([yab, ycdm[:, 0, 2 * BR_WIDTH:]], axis=1)
    x1 = _merge(x[None], yabm[None], ycdm[None, :, 0, 0:BR_WIDTH], ycdm[None, :, 0, BR_WIDTH:2 * BR_WIDTH], p, b)[0]
    x2, u = _ffn_step(x1, conv_ffn[:, 0], conv_ffn[:, 1], p)
    ca_new = jnp.stack([conv_a[:, 1], ach], axis=1)
    cf_new = jnp.stack([conv_ffn[:, 1], u], axis=1)
    return x2, ca_new, vn, ksb, vsb, kdf, vdf, cf_new


def kernel(x_prompt, x_sample, state_conv_a, cache_k_sb, cache_v_sb, cache_k_diff, cache_v_diff, cache_mem_k,
           cache_mem_v, state_conv_ffn, page_table, mem_prompt, norm1_g, w_in, conv_a_w, gmlp_ln_g, gmlp_ln_b,
           gmlp_ws, gmlp_b, diff_qn_g, diff_kn_g, diff_lambda, diff_out_g, mem_norm_g, w_mem_kv, mem_qn_g,
           mem_kn_g, w_branch, w_gate, b_gate, w_o, norm2_g, w_up, conv_ffn_w, conv_ffn_b, w_down):
    w = dict(norm1_g=norm1_g, w_in=w_in, conv_a_w=conv_a_w, gmlp_ln_g=gmlp_ln_g, gmlp_ln_b=gmlp_ln_b,
             gmlp_ws=gmlp_ws, gmlp_b=gmlp_b, diff_qn_g=diff_qn_g, diff_kn_g=diff_kn_g, diff_lambda=diff_lambda,
             diff_out_g=diff_out_g, mem_norm_g=mem_norm_g, w_mem_kv=w_mem_kv, mem_qn_g=mem_qn_g,
             mem_kn_g=mem_kn_g, w_branch=w_branch, w_gate=w_gate, b_gate=b_gate, w_o=w_o, norm2_g=norm2_g,
             w_up=w_up, conv_ffn_w=conv_ffn_w, conv_ffn_b=conv_ffn_b, w_down=w_down)
    depth = w_in.shape[0]
    n_p, t_p, _ = x_prompt.shape
    n_s = x_sample.shape[0]
    flat = lambda c: jnp.transpose(c, (0, 1, 3, 4, 2)).reshape(c.shape[:2] + (BR_WIDTH, c.shape[2]))
    caches = [flat(cache_k_sb), flat(cache_v_sb), flat(cache_k_diff), flat(cache_v_diff)]
    mem_k, mem_v = flat(cache_mem_k), flat(cache_mem_v)
    xp, xs = x_prompt, x_sample[:, 0]
    outs_p, outs_s = [], []
    big = _cast_weights(w)
    kv_p = None
    for l in range(depth):
        p = _layer_params(l, w, big)
        lam_init = 0.8 - 0.6 * math.exp(-0.3 * l)
        xp, kv_p, rest_p = _prompt_layer(xp, mem_prompt, p, lam_init, kv_p)
        outs_p.append(rest_p)
        xs, *rest_s = _sample_layer(l, xs, state_conv_a[l], state_conv_ffn[l], page_table, caches, mem_k, mem_v,
                                    p, lam_init)
        outs_s.append(rest_s)
    heads = lambda a: a.reshape(a.shape[:-1] + (N_GROUPS, HEAD_DIM))
    stack_p = lambda i: jnp.stack([o[i] for o in outs_p], axis=0)
    stack_s = lambda i: jnp.stack([o[i] for o in outs_s], axis=0)
    step = lambda a: a.reshape(depth, n_s, 1, -1)

    def heads_t(a):
        d0, n, _, t = a.shape
        return jnp.transpose(a.reshape(d0, n, N_GROUPS, HEAD_DIM, t), (0, 1, 4, 2, 3))

    ksb_p, vsb_p, kdf_p, vdf_p = kv_p
    return (xp, xs[:, None, :],
            stack_p(0), stack_s(0), step(stack_s(1)),
            heads_t(ksb_p), heads_t(vsb_p), heads(step(stack_s(2))), heads(step(stack_s(3))),
            heads_t(kdf_p), heads_t(vdf_p), heads(step(stack_s(4))), heads(step(stack_s(5))),
            heads_t(stack_p(1)), heads_t(stack_p(2)),
            stack_p(3), stack_s(6))
```

```python
import functools
import math
from typing import NamedTuple

import numpy as np
import jax
import jax.numpy as jnp
from jax import lax
from jax.experimental import pallas as pl
from jax.experimental.pallas import tpu as pltpu

D_MODEL = 1024
HEAD_DIM = 64
BR_WIDTH = 256
N_GROUPS = 4
DIFF_SUB = 32
CHUNK = 128
D_FF = 2816
EPS = 1e-6
NEG_BIG = -1e30
SB_UNDERFLOW_LOG2 = -150.0
LOG2E = 1.4426950408889634
VMEM_LIMIT_BYTES = 56 * 1024 * 1024

F32 = jnp.float32
BF16 = jnp.bfloat16


def _dot(a, b):
    return jnp.dot(a, b, preferred_element_type=F32)


def _dot_nt(a, b):
    return lax.dot_general(a, b, (((1,), (1,)), ((), ())), preferred_element_type=F32)


def _split_bf16(x):
    hi = x.astype(BF16)
    lo = (x - hi.astype(F32)).astype(BF16)
    return hi, lo


def _rms(x, g):
    ms = jnp.mean(x * x, axis=-1, keepdims=True)
    return x * lax.rsqrt(ms + EPS) * g


def _group_rms(z, g, ones_bd, group):
    ms = _dot((z * z).astype(BF16), ones_bd) * (1.0 / group)
    return z * lax.rsqrt(ms + EPS) * g


def _lane_iota(shape):
    return lax.broadcasted_iota(jnp.int32, shape, len(shape) - 1)


def _row_iota(shape):
    return lax.broadcasted_iota(jnp.int32, shape, len(shape) - 2)


def _div_pow2(x, d):
    assert d & (d - 1) == 0
    return lax.shift_right_logical(x, d.bit_length() - 1)


def _rem_pow2(x, d):
    assert d & (d - 1) == 0
    return x & (d - 1)


def _head_select(parts):
    lane = _lane_iota(parts[0].shape)
    out = parts[0]
    for h in range(1, N_GROUPS):
        out = jnp.where(lane >= HEAD_DIM * h, parts[h], out)
    return out


def _stack_heads(q, n_sub):
    lane = _lane_iota(q.shape)
    width = HEAD_DIM // n_sub
    zero = jnp.zeros_like(q)
    parts = []
    for c in range(n_sub):
        for h in range(N_GROUPS):
            lo = HEAD_DIM * h + width * c
            parts.append(jnp.where((lane >= lo) & (lane < lo + width), q, zero))
    return jnp.concatenate(parts, axis=0)


def _log_sigmoid_pair(z):
    lb = jnp.minimum(z, 0.0) - jnp.log(1.0 + jnp.exp(-jnp.abs(z)))
    return lb, lb - z


def _log2_sigmoid_pair(z2):
    lb = jnp.minimum(z2, 0.0) - jnp.log2(1.0 + jnp.exp2(-jnp.abs(z2)))
    return lb, lb - z2


def _lam_value(lp, lam_init):
    a = jnp.sum(lp[0:1] * lp[1:2], axis=1, keepdims=True)
    b = jnp.sum(lp[2:3] * lp[3:4], axis=1, keepdims=True)
    return jnp.exp(a) - jnp.exp(b) + lam_init


def _block_ones(group):
    i = np.arange(BR_WIDTH)
    return jnp.asarray((i[:, None] // group) == (i[None, :] // group), dtype=BF16)


def _suffix_ones(n):
    i = np.arange(n)
    l = (i[:, None] > i[None, :])
    return jnp.asarray(np.concatenate([l, l], axis=0), dtype=BF16)


def _alibi_slopes_log2():
    return [LOG2E * 2.0 ** (-8.0 * (h + 1) / N_GROUPS) for h in range(N_GROUPS)]


def _full_spec(shape):
    nd = len(shape)
    return pl.BlockSpec(shape, lambda *_: (0,) * nd, pipeline_mode=pl.Buffered(1))


class _Layered(NamedTuple):
    array: jax.Array
    layer: int

    @property
    def shape(self):
        return self.array.shape[1:]


def _const_spec(c):
    if isinstance(c, _Layered):
        rest = (0,) * len(c.shape)
        return pl.BlockSpec((None,) + c.shape, lambda *_: (c.layer,) + rest, pipeline_mode=pl.Buffered(1))
    return _full_spec(c.shape)


def _operand(c):
    return c.array if isinstance(c, _Layered) else c


def _params(*sem):
    return pltpu.CompilerParams(dimension_semantics=sem, vmem_limit_bytes=VMEM_LIMIT_BYTES)


def _group_rms_t(zt, g_col, ones_bd, group):
    hi, lo = _split_bf16(zt * zt)
    ms = (_dot(ones_bd, hi) + _dot(ones_bd, lo)) * (1.0 / group)
    return zt * lax.rsqrt(ms + EPS) * g_col


def _transpose_bf16(xt, eye):
    return _dot_nt(eye, xt)


def _memkv_kernel(mem_ref, g_ref, wt_ref, kg_ref, bd64_ref, eye_ref, k_ref, v_ref, k16_ref, v16_ref):
    hb = _rms(mem_ref[0], g_ref[...]).astype(BF16)
    kvt = _dot_nt(wt_ref[...], hb)
    k = _group_rms_t(kvt[:BR_WIDTH], kg_ref[...], bd64_ref[...], HEAD_DIM)
    v = kvt[BR_WIDTH:]
    k_ref[0] = k
    v_ref[0] = v
    k16_ref[0] = _transpose_bf16(k.astype(BF16), eye_ref[...]).astype(BF16)
    v16_ref[0] = v.astype(BF16)


def _mem_kv(mem, g, w, kg, bd64):
    n, m, d = mem.shape
    blk = pl.BlockSpec((1, BR_WIDTH, m), lambda i: (i, 0, 0))
    eye = jnp.asarray(np.eye(m), dtype=BF16)
    return pl.pallas_call(
        _memkv_kernel,
        grid=(n,),
        in_specs=[pl.BlockSpec((1, m, d), lambda i: (i, 0, 0))] + [_const_spec(c) for c in (g, w, kg, bd64, eye)],
        out_specs=[blk, blk, pl.BlockSpec((1, m, BR_WIDTH), lambda i: (i, 0, 0)), blk],
        out_shape=[jax.ShapeDtypeStruct((n, BR_WIDTH, m), F32)] * 2
        + [jax.ShapeDtypeStruct((n, m, BR_WIDTH), BF16), jax.ShapeDtypeStruct((n, BR_WIDTH, m), BF16)],
        compiler_params=_params("arbitrary"),
        name="mem_kv",
    )(mem, *map(_operand, (g, w, kg, bd64, eye)))


def _layer_norm(x, g, b):
    mu = jnp.mean(x, axis=-1, keepdims=True)
    xc = x - mu
    return xc * lax.rsqrt(jnp.mean(xc * xc, axis=-1, keepdims=True) + EPS) * g + b


def _mem_attention(qm, mk, mvt, eye):
    r = qm.shape[0]
    st = _dot_nt(mk, _stack_heads(qm, 1))
    p = jnp.exp2(st - jnp.max(st, axis=0, keepdims=True))
    inv = 1.0 / jnp.sum(p, axis=0, keepdims=True)
    ot = _dot(mvt, p.astype(BF16))
    yt = jnp.concatenate([ot[HEAD_DIM * h:HEAD_DIM * (h + 1), h * r:(h + 1) * r] * inv[:, h * r:(h + 1) * r]
                          for h in range(N_GROUPS)], axis=0)
    return _transpose_bf16(yt.astype(BF16), eye)


def _inproj_kernel(x_ref, g1_ref, win_ref, wsbt_ref, wdft_ref, cw_ref, lng_ref, lnb_ref, ws_ref, gb_ref, dqg_ref,
                   dkg_ref, dkgr_ref, mqg_ref, bd32_ref, bd64_ref, eye_ref, mk_ref, mv_ref, *rest, tm, n_prev):
    prev_kv = rest[:4] if n_prev else ()
    (yabm_ref, qsb_ref, ksb16_ref, vsb16_ref, qdf_ref, kdf16_ref, vdf16_ref,
     ksb_ref, vsb_ref, kdf_ref, vdf_ref, ca_ref, carry_ref) = rest[len(prev_kv):]
    for prev_ref, out_ref in zip(prev_kv, (ksb_ref, vsb_ref, kdf_ref, vdf_ref)):
        for l in range(n_prev):
            out_ref[l, 0] = prev_ref[l, 0]
    t = pl.program_id(1)
    hb = _rms(x_ref[0], g1_ref[...]).astype(BF16)

    p_all = _dot(hb, win_ref[...])

    def proj(lo, hi):
        return p_all[:, lo:hi]

    pa = proj(0, 3 * BR_WIDTH)
    ach = pa[:, BR_WIDTH:2 * BR_WIDTH] * pa[:, 2 * BR_WIDTH:]

    @pl.when(t == 0)
    def _():
        carry_ref[...] = jnp.zeros_like(carry_ref)

    prev = carry_ref[...]
    p1, p2 = prev[7:8], prev[6:7]
    row = _row_iota(ach.shape)
    s1 = jnp.where(row == 0, p1, pltpu.roll(ach, 1, axis=0))
    s2 = jnp.where(row == 0, p2, jnp.where(row == 1, p1, pltpu.roll(ach, 2, axis=0)))
    cw = cw_ref[...]
    yabm_ref[0, :, 0:BR_WIDTH] = (pa[:, :BR_WIDTH] * (cw[0:1] * s2 + cw[1:2] * s1 + cw[2:3] * ach)).astype(BF16)
    carry_ref[...] = ach[tm - 8:tm]
    ca_ref[0] = ach[tm - 2:tm]

    pg = proj(3 * BR_WIDTH, 5 * BR_WIDTH)
    vn = _layer_norm(pg[:, BR_WIDTH:], lng_ref[...], lnb_ref[...])
    tril = _row_iota((CHUNK, CHUNK)) >= _lane_iota((CHUNK, CHUNK))
    wsm = [jnp.where(tril, ws_ref[g], 0.0).astype(BF16) for g in range(N_GROUPS)]
    gb = gb_ref[...]
    n_chunks = tm // CHUNK
    vc = jnp.concatenate([vn[c * CHUNK:(c + 1) * CHUNK] for c in range(n_chunks)], axis=1).astype(BF16)
    channel = _rem_pow2(_lane_iota(vc.shape), BR_WIDTH)
    mixed = _dot(wsm[0], vc)
    for g in range(1, N_GROUPS):
        mixed = jnp.where(channel >= HEAD_DIM * g, _dot(wsm[g], vc), mixed)
    for c in range(n_chunks):
        yabm_ref[0, c * CHUNK:(c + 1) * CHUNK, BR_WIDTH:2 * BR_WIDTH] = (
            pg[c * CHUNK:(c + 1) * CHUNK, :BR_WIDTH] * (mixed[:, c * BR_WIDTH:(c + 1) * BR_WIDTH] + gb)).astype(BF16)

    pq = proj(5 * BR_WIDTH, 10 * BR_WIDTH)
    qsb_ref[0] = (pq[:, :BR_WIDTH] * (HEAD_DIM ** -0.5 * LOG2E)).astype(BF16)
    kvt = _dot_nt(wsbt_ref[...], hb)
    ksb_ref[n_prev, 0] = kvt[:BR_WIDTH]
    vsb_ref[n_prev, 0] = kvt[BR_WIDTH:]
    ksb16_ref[0] = pq[:, BR_WIDTH:2 * BR_WIDTH].astype(BF16)
    vsb16_ref[0] = kvt[BR_WIDTH:].astype(BF16)

    bd32 = bd32_ref[...]
    qd = _group_rms(pq[:, 2 * BR_WIDTH:3 * BR_WIDTH], dqg_ref[...], bd32, DIFF_SUB)
    qdf_ref[0] = (qd * (DIFF_SUB ** -0.5 * LOG2E)).astype(BF16)
    kvt = _dot_nt(wdft_ref[...], hb)
    kd = _group_rms_t(kvt[:BR_WIDTH], dkg_ref[...], bd32, DIFF_SUB)
    kdf_ref[n_prev, 0] = kd
    vdf_ref[n_prev, 0] = kvt[BR_WIDTH:]
    kdf16_ref[0] = _group_rms(pq[:, 3 * BR_WIDTH:4 * BR_WIDTH], dkgr_ref[...], bd32, DIFF_SUB).astype(BF16)
    vdf16_ref[0] = kvt[BR_WIDTH:].astype(BF16)

    qm = _group_rms(pq[:, 4 * BR_WIDTH:], mqg_ref[...], bd64_ref[...], HEAD_DIM)
    qm = (qm * (HEAD_DIM ** -0.5 * LOG2E)).astype(BF16)
    yabm_ref[0, :, 2 * BR_WIDTH:] = _mem_attention(qm, mk_ref[0], mv_ref[0], eye_ref[...]).astype(BF16)


def _inproj(x, p, mk16, mv16, tm, prev_kv):
    n, t, d = x.shape
    n_prev = prev_kv[0].shape[0] if prev_kv else 0
    prev_kv = list(prev_kv) if prev_kv else []
    stacked = lambda layers: pl.BlockSpec((layers, 1, BR_WIDTH, tm), lambda i, j: (0, i, 0, j))
    consts = [p["norm1_g"], p["w_in_tok"], p["w_sb_kv_t"], p["w_df_kv_t"], p["conv_a_w"], p["gmlp_ln_g"], p["gmlp_ln_b"],
              p["gmlp_ws"], p["gmlp_b_full"], p["diff_qn_g"], p["diff_kn_g_col"], p["diff_kn_g"], p["mem_qn_g"],
              p["bd32"], p["bd64"], jnp.asarray(np.eye(tm), dtype=BF16)]
    seq = lambda w: pl.BlockSpec((1, tm, w), lambda i, j: (i, j, 0))
    seq_t = pl.BlockSpec((1, BR_WIDTH, tm), lambda i, j: (i, 0, j))
    mem_spec = lambda a: pl.BlockSpec((1,) + a.shape[1:], lambda i, j: (i, 0, 0))
    bf = lambda w: jax.ShapeDtypeStruct((n, t, w), BF16)
    bf_t = jax.ShapeDtypeStruct((n, BR_WIDTH, t), BF16)
    fl_t = jax.ShapeDtypeStruct((n_prev + 1, n, BR_WIDTH, t), F32)
    return pl.pallas_call(
        functools.partial(_inproj_kernel, tm=tm, n_prev=n_prev),
        grid=(n, t // tm),
        in_specs=[seq(d)] + [_const_spec(c) for c in consts] + [mem_spec(mk16), mem_spec(mv16)]
        + [stacked(n_prev)] * len(prev_kv),
        out_specs=[seq(3 * BR_WIDTH), seq(BR_WIDTH), seq(BR_WIDTH), seq_t, seq(BR_WIDTH), seq(BR_WIDTH), seq_t]
        + [stacked(n_prev + 1)] * 4 + [pl.BlockSpec((1, 2, BR_WIDTH), lambda i, j: (i, 0, 0))],
        out_shape=[bf(3 * BR_WIDTH), bf(BR_WIDTH), bf(BR_WIDTH), bf_t, bf(BR_WIDTH), bf(BR_WIDTH), bf_t] + [fl_t] * 4
        + [jax.ShapeDtypeStruct((n, 2, BR_WIDTH), F32)],
        scratch_shapes=[pltpu.VMEM((8, BR_WIDTH), F32)],
        compiler_params=_params("arbitrary", "arbitrary"),
        name="in_proj",
    )(x, *map(_operand, consts), mk16, mv16, *prev_kv)


def _sb_kernel(q_ref, k_ref, v_ref, uu_ref, eye_ref, o_ref, qm_ref, carry_ref, acc_ref, *, tb):
    qi = pl.program_id(1)
    q = q_ref[0]
    lane = _lane_iota(q.shape)
    for h in range(N_GROUPS):
        in_head = (lane >= HEAD_DIM * h) & (lane < HEAD_DIM * (h + 1))
        qm_ref[h * tb:(h + 1) * tb] = jnp.where(in_head, q, jnp.zeros_like(q))
    strictly_before = _row_iota((tb, tb)) < _lane_iota((tb, tb))
    heads = range(N_GROUPS)

    def block(kb, first):
        start = pl.multiple_of(kb * tb, tb)
        z_all = _dot_nt(k_ref[0, pl.ds(start, tb), :], qm_ref[...])
        lbs, l1ms = [], []
        for h in heads:
            lb, l1m = _log2_sigmoid_pair(z_all[:, h * tb:(h + 1) * tb])
            lbs.append(lb)
            l1ms.append(jnp.where(strictly_before, l1m, 0.0) if first else l1m)
        between = _dot(uu_ref[...], jnp.concatenate(l1ms, axis=1).astype(BF16))
        for h in heads:
            total = lbs[h] + between[:, h * tb:(h + 1) * tb]
            if not first:
                total = total + carry_ref[h:h + 1]
            a = jnp.exp2(total)
            if first:
                a = jnp.where(strictly_before, a, 0.0)
            pv = _dot(v_ref[0, HEAD_DIM * h:HEAD_DIM * (h + 1), pl.ds(start, tb)], a.astype(BF16))
            block_sum = jnp.sum(l1ms[h], axis=0, keepdims=True)
            if first:
                acc_ref[h] = pv
                carry_ref[h:h + 1] = block_sum
            else:
                acc_ref[h] += pv
                carry_ref[h:h + 1] += block_sum

    block(qi, True)

    def cond(state):
        i, live = state
        return (i < qi) & (live > SB_UNDERFLOW_LOG2)

    def body(state):
        i, _ = state
        block(qi - 1 - i, False)
        return i + 1, jnp.max(carry_ref[...])

    lax.while_loop(cond, body, (jnp.int32(0), jnp.max(carry_ref[...])))
    yt = jnp.concatenate([acc_ref[h] for h in heads], axis=0)
    o_ref[0] = _dot_nt(eye_ref[...], yt.astype(BF16)).astype(BF16)


def _sb_attention(q, k, v, tb):
    n, t, w = q.shape
    i = np.arange(tb)
    uu = jnp.asarray(i[None, :] > i[:, None], dtype=BF16)
    eye = jnp.asarray(np.eye(tb), dtype=BF16)
    return pl.pallas_call(
        functools.partial(_sb_kernel, tb=tb),
        grid=(n, t // tb),
        in_specs=[pl.BlockSpec((1, tb, w), lambda i, j: (i, j, 0)),
                  pl.BlockSpec((1, t, w), lambda i, j: (i, 0, 0)),
                  pl.BlockSpec((1, w, t), lambda i, j: (i, 0, 0)),
                  _full_spec(uu.shape), _full_spec(eye.shape)],
        out_specs=pl.BlockSpec((1, tb, w), lambda i, j: (i, j, 0)),
        out_shape=jax.ShapeDtypeStruct((n, t, w), BF16),
        scratch_shapes=[pltpu.VMEM((N_GROUPS * tb, w), BF16), pltpu.VMEM((N_GROUPS, tb), F32),
                        pltpu.VMEM((N_GROUPS, HEAD_DIM, tb), F32)],
        compiler_params=_params("arbitrary", "arbitrary"),
        name="sb_attn",
    )(q, k, v, uu, eye)


def _diff_kernel(q_ref, k_ref, v_ref, lp_ref, og_ref, bd64_ref, eye_ref, o_ref,
                 qm_ref, bias_ref, m_ref, l_ref, acc_ref, *, tb, lam_init):
    qi = pl.program_id(1)
    slopes = _alibi_slopes_log2()
    rel = _row_iota((tb, tb)) - _lane_iota((tb, tb))

    @pl.when(qi == 0)
    def _():
        for h in range(N_GROUPS):
            bias_ref[h] = slopes[h] * rel.astype(F32)

    q = q_ref[0]
    lane = _lane_iota(q.shape)
    for c in range(2):
        for h in range(N_GROUPS):
            lo = HEAD_DIM * h + DIFF_SUB * c
            ch = c * N_GROUPS + h
            qm_ref[ch * tb:(ch + 1) * tb] = jnp.where((lane >= lo) & (lane < lo + DIFF_SUB), q, jnp.zeros_like(q))

    def scores(kb):
        start = pl.multiple_of(kb * tb, tb)
        return _dot_nt(k_ref[0, pl.ds(start, tb), :], qm_ref[...])

    def block(kb, score_tile, first):
        start = pl.multiple_of(kb * tb, tb)
        off = ((kb - qi) * tb).astype(F32)
        probs, alphas = [], []
        for c in range(2):
            for h in range(N_GROUPS):
                ch = c * N_GROUPS + h
                s = score_tile(ch) + bias_ref[h]
                shift = slopes[h] * off
                if first:
                    s = jnp.where(rel <= 0, s, NEG_BIG)
                    m_new = jnp.max(s, axis=0, keepdims=True) + shift
                    p = jnp.exp2(s - (m_new - shift))
                    l_ref[ch:ch + 1] = jnp.sum(p, axis=0, keepdims=True)
                    alphas.append(None)
                else:
                    m_old = m_ref[ch:ch + 1]
                    m_new = jnp.maximum(m_old, jnp.max(s, axis=0, keepdims=True) + shift)
                    alpha = jnp.exp2(m_old - m_new)
                    p = jnp.exp2(s - (m_new - shift))
                    l_ref[ch:ch + 1] = alpha * l_ref[ch:ch + 1] + jnp.sum(p, axis=0, keepdims=True)
                    alphas.append(alpha)
                m_ref[ch:ch + 1] = m_new
                probs.append(p.astype(BF16))
        for h in range(N_GROUPS):
            vh = v_ref[0, HEAD_DIM * h:HEAD_DIM * (h + 1), pl.ds(start, tb)]
            pv = _dot(vh, jnp.concatenate([probs[h], probs[N_GROUPS + h]], axis=1))
            for c in range(2):
                ch = c * N_GROUPS + h
                new = pv[:, c * tb:(c + 1) * tb]
                acc_ref[ch] = new if first else alphas[ch] * acc_ref[ch] + new

    def run_block(kb, first):
        s_all = scores(kb)
        block(kb, lambda ch: s_all[:, ch * tb:(ch + 1) * tb], first)

    run_block(qi, True)

    def body(i, _):
        run_block(qi - 1 - i, False)
        return 0

    lax.fori_loop(0, qi, body, 0)
    lam = _lam_value(lp_ref[...], lam_init)
    ys = []
    for h in range(N_GROUPS):
        y0 = acc_ref[h] * (1.0 / l_ref[h:h + 1])
        y1 = acc_ref[N_GROUPS + h] * (lam / l_ref[N_GROUPS + h:N_GROUPS + h + 1])
        ys.append(y0 - y1)
    yt = _group_rms_t(jnp.concatenate(ys, axis=0), og_ref[...], bd64_ref[...], HEAD_DIM) * (1.0 - lam_init)
    o_ref[0] = _dot_nt(eye_ref[...], yt.astype(BF16)).astype(BF16)


def _diff_attention(q, k, v, lp, og_col, bd64, lam_init, tb):
    n, t, w = q.shape
    maps = 2 * N_GROUPS
    eye = jnp.asarray(np.eye(tb), dtype=BF16)
    return pl.pallas_call(
        functools.partial(_diff_kernel, tb=tb, lam_init=lam_init),
        grid=(n, t // tb),
        in_specs=[pl.BlockSpec((1, tb, w), lambda i, j: (i, j, 0)),
                  pl.BlockSpec((1, t, w), lambda i, j: (i, 0, 0)),
                  pl.BlockSpec((1, w, t), lambda i, j: (i, 0, 0))]
        + [_const_spec(c) for c in (lp, og_col, bd64, eye)],
        out_specs=pl.BlockSpec((1, tb, w), lambda i, j: (i, j, 0)),
        out_shape=jax.ShapeDtypeStruct((n, t, w), BF16),
        scratch_shapes=[pltpu.VMEM((maps * tb, w), BF16), pltpu.VMEM((N_GROUPS, tb, tb), F32),
                        pltpu.VMEM((maps, tb), F32), pltpu.VMEM((maps, tb), F32),
                        pltpu.VMEM((maps, HEAD_DIM, tb), F32)],
        compiler_params=_params("arbitrary", "arbitrary"),
        name="diff_attn",
    )(q, k, v, *map(_operand, (lp, og_col, bd64, eye)))


def _merge_kernel(x_ref, yabm_ref, yc_ref, yd_ref, g1_ref, wg_ref, bg_ref, wb_ref, wo_ref, o_ref):
    x = x_ref[0]
    hb = _rms(x, g1_ref[...]).astype(BF16)
    ys = [yabm_ref[0, :, 0:BR_WIDTH], yabm_ref[0, :, BR_WIDTH:2 * BR_WIDTH], yc_ref[0], yd_ref[0],
          yabm_ref[0, :, 2 * BR_WIDTH:]]
    acc = None
    for b in range(5):
        gate = jax.nn.sigmoid(_dot(hb, wg_ref[:, b * D_MODEL:(b + 1) * D_MODEL]) + bg_ref[:, b * D_MODEL:(b + 1) * D_MODEL])
        term = gate * _dot(ys[b], wb_ref[b])
        acc = term if acc is None else acc + term
    o_ref[0] = x + _dot(acc.astype(BF16), wo_ref[...])


def _merge(x, yabm, yc, yd, p, tm):
    n, t, d = x.shape
    consts = [p["norm1_g"], p["w_gate"], p["b_gate"], p["w_branch"], p["w_o"]]
    seq = lambda w: pl.BlockSpec((1, tm, w), lambda i, j: (i, j, 0))
    return pl.pallas_call(
        _merge_kernel,
        grid=(n, t // tm),
        in_specs=[seq(d), seq(3 * BR_WIDTH), seq(BR_WIDTH), seq(BR_WIDTH)] + [_const_spec(c) for c in consts],
        out_specs=seq(d),
        out_shape=jax.ShapeDtypeStruct((n, t, d), F32),
        compiler_params=_params("arbitrary", "arbitrary"),
        name="merge",
    )(x, yabm, yc, yd, *map(_operand, consts))


SUBLANES = 8
MXU_WIDTH = 256
FFN_CHUNKS = ((0, 6 * MXU_WIDTH), (6 * MXU_WIDTH, D_FF - 6 * MXU_WIDTH))


def _ffn_kernel(x_ref, g2_ref, wup_ref, cw_ref, cb_ref, wdn_ref, o_ref, st_ref, carry_ref, *, tm):
    t = pl.program_id(1)
    x = x_ref[0]
    hb = _rms(x, g2_ref[...]).astype(BF16)

    @pl.when(t == 0)
    def _():
        carry_ref[...] = jnp.zeros_like(carry_ref)

    def conv(lo, width):
        cols = slice(lo, lo + width)
        row = _row_iota((tm, width))
        u = _dot(hb, wup_ref[:, cols])
        prev = carry_ref[:, cols]
        p1, p2 = prev[7:8], prev[6:7]
        s1 = jnp.where(row == 0, p1, pltpu.roll(u, 1, axis=0))
        s2 = jnp.where(row == 0, p2, jnp.where(row == 1, p1, pltpu.roll(u, 2, axis=0)))
        carry_ref[:, cols] = u[tm - 8:tm]
        st_ref[0, :, cols] = u[tm - 2:tm]
        return cw_ref[0:1, cols] * s2 + cw_ref[1:2, cols] * s1 + cw_ref[2:3, cols] * u + cb_ref[:, cols]

    acc = x
    for lo, width in FFN_CHUNKS:
        gate = conv(lo, width)
        val = conv(D_FF + lo, width)
        act = (gate * jax.nn.sigmoid(gate) * val).astype(BF16)
        acc = acc + _dot(act, wdn_ref[lo:lo + width, :])
    o_ref[0] = acc


def _ffn(x, p, tm):
    n, t, d = x.shape
    consts = [p["norm2_g"], p["w_up"], p["conv_ffn_w"], p["conv_ffn_b"], p["w_down"]]
    seq = pl.BlockSpec((1, tm, d), lambda i, j: (i, j, 0))
    return pl.pallas_call(
        functools.partial(_ffn_kernel, tm=tm),
        grid=(n, t // tm),
        in_specs=[seq] + [_const_spec(c) for c in consts],
        out_specs=[seq, pl.BlockSpec((1, 2, 2 * D_FF), lambda i, j: (i, 0, 0))],
        out_shape=[jax.ShapeDtypeStruct((n, t, d), F32), jax.ShapeDtypeStruct((n, 2, 2 * D_FF), F32)],
        scratch_shapes=[pltpu.VMEM((8, 2 * D_FF), F32)],
        compiler_params=_params("arbitrary", "arbitrary"),
        name="conv_ffn",
    )(x, *map(_operand, consts))


def _ffn_step_kernel(x_ref, p0_ref, p1_ref, g2_ref, wup_ref, cw_ref, cb_ref, wdn_ref, o_ref, u_ref):
    x = x_ref[...]
    hb = _rms(x, g2_ref[...]).astype(BF16)

    def conv(lo, width):
        cols = slice(lo, lo + width)
        u = _dot(hb, wup_ref[:, cols])
        u_ref[:, cols] = u
        return (cw_ref[0:1, cols] * p0_ref[:, cols] + cw_ref[1:2, cols] * p1_ref[:, cols]
                + cw_ref[2:3, cols] * u + cb_ref[:, cols])

    acc = x
    for lo, width in FFN_CHUNKS:
        gate = conv(lo, width)
        val = conv(D_FF + lo, width)
        act = (gate * jax.nn.sigmoid(gate) * val).astype(BF16)
        acc = acc + _dot(act, wdn_ref[lo:lo + width, :])
    o_ref[...] = acc


def _ffn_step(x, prev0, prev1, p):
    m, d = x.shape
    consts = [p["norm2_g"], p["w_up"], p["conv_ffn_w"], p["conv_ffn_b"], p["w_down"]]
    args = [x, prev0, prev1] + consts
    return pl.pallas_call(
        _ffn_step_kernel,
        grid=(1,),
        in_specs=[_const_spec(a) for a in args],
        out_specs=[_full_spec((m, d)), _full_spec((m, 2 * D_FF))],
        out_shape=[jax.ShapeDtypeStruct((m, d), F32), jax.ShapeDtypeStruct((m, 2 * D_FF), F32)],
        compiler_params=_params("arbitrary"),
        name="conv_ffn_step",
    )(*map(_operand, args))


def _inproj_step_kernel(x_ref, c0_ref, c1_ref, g1_ref, win_ref, cw_ref, lng_ref, lnb_ref, ws_ref, gb_ref,
                        dqg_ref, dkg_ref, mqg_ref, bd32_ref, bd64_ref,
                        yab_ref, q_ref, ksb_ref, vsb_ref, kdf_ref, vdf_ref, vn_ref, ach_ref):
    hb = _rms(x_ref[...], g1_ref[...]).astype(BF16)

    def proj(lo, hi):
        return _dot(hb, win_ref[:, lo:hi])

    pa = proj(0, 3 * BR_WIDTH)
    ach = pa[:, BR_WIDTH:2 * BR_WIDTH] * pa[:, 2 * BR_WIDTH:]
    cw = cw_ref[...]
    yab_ref[:, 0:BR_WIDTH] = (pa[:, :BR_WIDTH] * (cw[0:1] * c0_ref[...] + cw[1:2] * c1_ref[...] + cw[2:3] * ach)).astype(BF16)
    ach_ref[...] = ach

    pg = proj(3 * BR_WIDTH, 5 * BR_WIDTH)
    vn = _layer_norm(pg[:, BR_WIDTH:], lng_ref[...], lnb_ref[...])
    vn_ref[...] = vn
    lane = _lane_iota((1, BR_WIDTH))
    w00 = jnp.zeros((1, BR_WIDTH), F32)
    for g in range(N_GROUPS):
        w00 = jnp.where((lane >= HEAD_DIM * g) & (lane < HEAD_DIM * (g + 1)), ws_ref[g, 0:1, 0:1], w00)
    yab_ref[:, BR_WIDTH:] = (pg[:, :BR_WIDTH] * (w00 * vn + gb_ref[0:1, :])).astype(BF16)

    ps = proj(5 * BR_WIDTH, 8 * BR_WIDTH)
    q_ref[:, 0:BR_WIDTH] = (ps[:, :BR_WIDTH] * (HEAD_DIM ** -0.5)).astype(BF16)
    ksb_ref[...] = ps[:, BR_WIDTH:2 * BR_WIDTH]
    vsb_ref[...] = ps[:, 2 * BR_WIDTH:]

    pd = proj(8 * BR_WIDTH, 11 * BR_WIDTH)
    bd32 = bd32_ref[...]
    qd = _group_rms(pd[:, :BR_WIDTH], dqg_ref[...], bd32, DIFF_SUB)
    q_ref[:, BR_WIDTH:2 * BR_WIDTH] = (qd * (DIFF_SUB ** -0.5 * LOG2E)).astype(BF16)
    kdf_ref[...] = _group_rms(pd[:, BR_WIDTH:2 * BR_WIDTH], dkg_ref[...], bd32, DIFF_SUB)
    vdf_ref[...] = pd[:, 2 * BR_WIDTH:]

    qm = _group_rms(proj(11 * BR_WIDTH, 12 * BR_WIDTH), mqg_ref[...], bd64_ref[...], HEAD_DIM)
    q_ref[:, 2 * BR_WIDTH:] = (qm * (HEAD_DIM ** -0.5 * LOG2E)).astype(BF16)


def _inproj_step(x, c0, c1, p):
    m, d = x.shape
    consts = [p["norm1_g"], p["w_in"], p["conv_a_w"], p["gmlp_ln_g"], p["gmlp_ln_b"], p["gmlp_ws"],
              p["gmlp_b_full"], p["diff_qn_g"], p["diff_kn_g"], p["mem_qn_g"], p["bd32"], p["bd64"]]
    args = [x, c0, c1] + consts
    fl = jax.ShapeDtypeStruct((m, BR_WIDTH), F32)
    out_shape = [jax.ShapeDtypeStruct((m, 2 * BR_WIDTH), BF16), jax.ShapeDtypeStruct((m, 3 * BR_WIDTH), BF16)] + [fl] * 6
    return pl.pallas_call(
        _inproj_step_kernel,
        grid=(1,),
        in_specs=[_const_spec(a) for a in args],
        out_specs=[_full_spec(s.shape) for s in out_shape],
        out_shape=out_shape,
        compiler_params=_params("arbitrary"),
        name="in_proj_step",
    )(*map(_operand, args))


def _decode_kernel(pt_ref, q_ref, kn_ref, vn_ref, mk_ref, mv_ref, lp_ref, og_ref, bd64_ref, ll_ref, ones_ref,
                   ksb_hbm, vsb_hbm, kdf_hbm, vdf_hbm, o_ref, buf_ref, sem_ref, *, layer, n_samples, n_pages, page,
                   lam_init):
    step = pl.program_id(0)
    pools = (ksb_hbm, vsb_hbm, kdf_hbm, vdf_hbm)

    def page_copy(sample, slot, c, j):
        return pltpu.make_async_copy(pools[c].at[layer, pt_ref[sample, j]], buf_ref.at[slot, c, j], sem_ref.at[slot])

    def for_each_page(sample, slot, fn):
        for c in range(len(pools)):
            for j in range(n_pages):
                fn(page_copy(sample, slot, c, j))

    ahead = DECODE_SLOTS - 1

    @pl.when(step == 0)
    def _():
        for s in range(min(ahead, n_samples)):
            for_each_page(s, s, lambda cp: cp.start())

    @pl.when(step + ahead < n_samples)
    def _():
        for_each_page(step + ahead, lax.rem(step + ahead, DECODE_SLOTS), lambda cp: cp.start())

    slot = lax.rem(step, DECODE_SLOTS)
    for_each_page(step, slot, lambda cp: cp.wait())
    ksb, vsb, kdf, vdf = ([buf_ref.at[slot, c, j] for j in range(n_pages)] for c in range(len(pools)))
    rows = 2 * N_GROUPS
    n_sub = HEAD_DIM // DIFF_SUB
    ones = ones_ref[...]

    def q_column(col):
        qrow = jnp.broadcast_to(q_ref[0, :, col * BR_WIDTH:(col + 1) * BR_WIDTH].astype(F32), (BR_WIDTH, BR_WIDTH))
        diag = jnp.where(_row_iota(qrow.shape) == _lane_iota(qrow.shape), qrow, 0.0).astype(BF16)
        return _dot(diag, ones)

    def group_scores(kt, qcol):
        w = kt.shape[1]
        part = jnp.sum((kt * qcol).reshape(rows, DIFF_SUB // SUBLANES, SUBLANES, w), axis=1)
        tiles = [part[g] for g in range(rows)]
        sub = _row_iota((SUBLANES, w))
        for k in (4, 2, 1):
            upper = (sub & k) != 0
            half = len(tiles) // 2
            nxt = []
            for a in range(half):
                x, y = tiles[a], tiles[a + half]
                keep = jnp.where(upper, y, x)
                send = jnp.where(upper, x, y)
                down = pltpu.roll(send, k, axis=0)
                partner = down if 2 * k == SUBLANES else jnp.where(upper, down,
                                                                   pltpu.roll(send, SUBLANES - k, axis=0))
                nxt.append(keep + partner)
            tiles = nxt
        return tiles[0]

    def pair_sum(z):
        r = z.shape[0]
        even = (_row_iota(z.shape) & 1) == 0
        return z + jnp.where(even, pltpu.roll(z, r - 1, axis=0), pltpu.roll(z, 1, axis=0))

    def head_rows(w):
        return jnp.concatenate([jnp.broadcast_to(w[n_sub * h:n_sub * h + 1], (HEAD_DIM, w.shape[1]))
                                for h in range(N_GROUPS)], axis=0)

    def reduce_positions(acc):
        hi, lo = _split_bf16(acc)
        ones_row = jnp.ones((rows, acc.shape[1]), BF16)
        return _dot_nt(ones_row, hi) + _dot_nt(ones_row, lo)

    qc = q_column(0)
    z = jnp.concatenate([pair_sum(group_scores(ksb[pg][...], qc)) for pg in range(n_pages)], axis=0)
    lb, l1m = _log_sigmoid_pair(z)
    hi, lo = _split_bf16(l1m)
    hl = jnp.concatenate([hi, lo], axis=1)
    within = _dot(hl, ll_ref[...])
    total = _dot(hl, ones)
    carry = jnp.zeros((rows, page), F32)
    carries = [None] * n_pages
    for pg in reversed(range(n_pages)):
        carries[pg] = carry
        carry = carry + total[rows * pg:rows * (pg + 1)]
    a = jnp.exp(lb + within + jnp.concatenate(carries, axis=0))
    acc = jnp.zeros((BR_WIDTH, page), F32)
    for pg in range(n_pages):
        acc = acc + head_rows(a[rows * pg:rows * (pg + 1)]) * vsb[pg][...]
    y_c = reduce_positions(acc)[0:1]

    qc = q_column(1)
    s = jnp.concatenate([group_scores(kdf[pg][...], qc) for pg in range(n_pages)], axis=0)
    slopes = _alibi_slopes_log2()
    all_rows = _row_iota((rows * n_pages, page))
    head = _div_pow2(_rem_pow2(all_rows, rows), n_sub)
    sl = jnp.full(all_rows.shape, slopes[0], F32)
    for h in range(1, N_GROUPS):
        sl = jnp.where(head == h, slopes[h], sl)
    past = n_pages * page
    k_pos = _div_pow2(all_rows, rows) * page + _lane_iota(all_rows.shape)
    s = s + sl * (k_pos - past).astype(F32)
    row8 = _row_iota((rows, BR_WIDTH))
    group_lanes = _div_pow2(_lane_iota((rows, BR_WIDTH)), DIFF_SUB) == row8
    q8 = jnp.broadcast_to(q_ref[0, :, BR_WIDTH:2 * BR_WIDTH].astype(F32), (rows, BR_WIDTH))
    kn8 = jnp.broadcast_to(kn_ref[0], (rows, BR_WIDTH))
    s_new = jnp.sum(jnp.where(group_lanes, q8 * kn8, 0.0), axis=1, keepdims=True)
    m8 = s[0:rows]
    for pg in range(1, n_pages):
        m8 = jnp.maximum(m8, s[rows * pg:rows * (pg + 1)])
    m = jnp.maximum(jnp.max(m8, axis=1, keepdims=True), s_new)
    p_new = jnp.exp2(s_new - m)
    m_all = jnp.concatenate([jnp.broadcast_to(m, (rows, page))] * n_pages, axis=0)
    p = jnp.exp2(s - m_all)
    l8 = p[0:rows]
    for pg in range(1, n_pages):
        l8 = l8 + p[rows * pg:rows * (pg + 1)]
    l = jnp.sum(l8, axis=1, keepdims=True) + p_new
    lam = _lam_value(lp_ref[...], lam_init)
    first_map = (_row_iota((rows, 1)) & 1) == 0
    coef = jnp.where(first_map, 1.0, -lam) / l
    w = pair_sum(p * jnp.concatenate([jnp.broadcast_to(coef, (rows, page))] * n_pages, axis=0))
    acc = jnp.zeros((BR_WIDTH, page), F32)
    for pg in range(n_pages):
        acc = acc + head_rows(w[rows * pg:rows * (pg + 1)]) * vdf[pg][...]
    w_new = pair_sum(p_new * coef)
    head_lanes = (_div_pow2(_lane_iota((rows, BR_WIDTH)), HEAD_DIM) * n_sub) == row8
    vn8 = jnp.broadcast_to(vn_ref[0], (rows, BR_WIDTH))
    y_new = jnp.sum(jnp.where(head_lanes, w_new * vn8, 0.0), axis=0, keepdims=True)
    y_d = reduce_positions(acc)[0:1] + y_new
    y_d = _group_rms(jnp.broadcast_to(y_d, (rows, BR_WIDTH)), og_ref[...], bd64_ref[...], HEAD_DIM)[0:1]
    y_d = y_d * (1.0 - lam_init)

    qc = q_column(2)
    n_mem = mk_ref.shape[2]
    qc = jnp.concatenate([qc] * (n_mem // page), axis=1)
    s = pair_sum(group_scores(mk_ref[0], qc))
    p = jnp.exp2(s - jnp.max(s, axis=1, keepdims=True))
    p = p / jnp.sum(p, axis=1, keepdims=True)
    y_m = reduce_positions(head_rows(p) * mv_ref[0])[0:1]

    o_ref[0, :, 0:BR_WIDTH] = y_c.astype(BF16)
    o_ref[0, :, BR_WIDTH:2 * BR_WIDTH] = y_d.astype(BF16)
    o_ref[0, :, 2 * BR_WIDTH:] = y_m.astype(BF16)


def _decode_attention(layer, page_table, q, k_new, v_new, caches, mem_k, mem_v, p, lam_init):
    b = q.shape[0]
    n_pages = page_table.shape[1]
    page = caches[0].shape[3]
    ll = _suffix_ones(page)
    row_spec = lambda w: pl.BlockSpec((1, 1, w), lambda i, pt: (i, 0, 0))
    mem_spec = pl.BlockSpec((None, 1) + mem_k.shape[2:], lambda i, pt: (layer, i, 0, 0))
    consts = [p["diff_lambda"], p["diff_out_g"], p["bd64"], ll, jnp.ones((BR_WIDTH, page), BF16)]
    grid_spec = pltpu.PrefetchScalarGridSpec(
        num_scalar_prefetch=1,
        grid=(b,),
        in_specs=[row_spec(3 * BR_WIDTH), row_spec(BR_WIDTH), row_spec(BR_WIDTH), mem_spec, mem_spec]
        + [_const_spec(c) for c in consts] + [pl.BlockSpec(memory_space=pl.ANY)] * len(caches),
        out_specs=row_spec(3 * BR_WIDTH),
        scratch_shapes=[pltpu.VMEM((DECODE_SLOTS, len(caches), n_pages, BR_WIDTH, page), F32),
                        pltpu.SemaphoreType.DMA((DECODE_SLOTS,))],
    )
    return pl.pallas_call(
        functools.partial(_decode_kernel, layer=layer, n_samples=b, n_pages=n_pages, page=page, lam_init=lam_init),
        grid_spec=grid_spec,
        out_shape=jax.ShapeDtypeStruct((b, 1, 3 * BR_WIDTH), BF16),
        compiler_params=_params("arbitrary"),
        name="decode_attn",
    )(page_table, q, k_new, v_new, mem_k, mem_v, *map(_operand, consts), *caches)


PROMPT_TILE = 512
DECODE_SLOTS = 3
ATTN_BLOCK = 256


def _stacked_params(w):
    depth = w["w_in"].shape[0]
    f32 = lambda k: w[k].astype(F32)
    bf16 = lambda k: w[k].astype(BF16)
    row = lambda k: f32(k).reshape(depth, 1, -1)
    tile4 = lambda k: jnp.tile(f32(k), (1, N_GROUPS)).reshape(depth, 1, -1)
    col4 = lambda k: jnp.tile(f32(k), (1, N_GROUPS)).reshape(depth, -1, 1)
    w_in = bf16("w_in")
    return dict(
        norm1_g=row("norm1_g"), w_in=w_in, conv_a_w=f32("conv_a_w"),
        w_sb_kv_t=jnp.swapaxes(w_in[:, :, 6 * BR_WIDTH:8 * BR_WIDTH], 1, 2),
        w_df_kv_t=jnp.swapaxes(w_in[:, :, 9 * BR_WIDTH:11 * BR_WIDTH], 1, 2),
        w_in_tok=jnp.concatenate([w_in[:, :, :7 * BR_WIDTH], w_in[:, :, 8 * BR_WIDTH:10 * BR_WIDTH],
                                  w_in[:, :, 11 * BR_WIDTH:]], axis=2),
        diff_kn_g_col=col4("diff_kn_g"), mem_kn_g_col=col4("mem_kn_g"), diff_out_g_col=col4("diff_out_g"),
        w_mem_kv_t=jnp.swapaxes(bf16("w_mem_kv"), 1, 2),
        gmlp_ln_g=row("gmlp_ln_g"), gmlp_ln_b=row("gmlp_ln_b"), gmlp_ws=f32("gmlp_ws"),
        gmlp_b_full=jnp.repeat(jnp.swapaxes(f32("gmlp_b"), 1, 2), HEAD_DIM, axis=2),
        diff_qn_g=tile4("diff_qn_g"), diff_kn_g=tile4("diff_kn_g"), diff_out_g=tile4("diff_out_g"),
        diff_lambda=f32("diff_lambda"),
        mem_norm_g=row("mem_norm_g"), mem_qn_g=tile4("mem_qn_g"),
        w_branch=bf16("w_branch"), w_gate=bf16("w_gate"), b_gate=row("b_gate"),
        w_o=bf16("w_o"), norm2_g=row("norm2_g"), w_up=bf16("w_up"),
        conv_ffn_w=f32("conv_ffn_w"), conv_ffn_b=row("conv_ffn_b"), w_down=bf16("w_down"),
    )


def _layer_params(l, stacked):
    p = {k: _Layered(v, l) for k, v in stacked.items()}
    p.update(bd32=_block_ones(DIFF_SUB), bd64=_block_ones(HEAD_DIM))
    return p


def _prompt_layer(x, mem, p, lam_init, prev_kv):
    n, t, _ = x.shape
    tm = min(PROMPT_TILE, t)
    tb = min(ATTN_BLOCK, t)
    mk, mv, mk16, mv16 = _mem_kv(mem, p["mem_norm_g"], p["w_mem_kv_t"], p["mem_kn_g_col"], p["bd64"])
    (yabm, qsb, ksb16, vsb16, qdf, kdf16, vdf16, *kv, ca) = _inproj(x, p, mk16, mv16, tm, prev_kv)
    yc = _sb_attention(qsb, ksb16, vsb16, tb)
    yd = _diff_attention(qdf, kdf16, vdf16, p["diff_lambda"], p["diff_out_g_col"], p["bd64"], lam_init, tb)
    x1 = _merge(x, yabm, yc, yd, p, tm)
    x2, cf = _ffn(x1, p, tm)
    return x2, kv, (ca, mk, mv, cf)


def _sample_layer(l, x, conv_a, conv_ffn, page_table, caches, mem_k, mem_v, p, lam_init):
    b = x.shape[0]
    yab, q, ksb, vsb, kdf, vdf, vn, ach = _inproj_step(x, conv_a[:, 0], conv_a[:, 1], p)
    row3 = lambda a: a.reshape(b, 1, -1)
    ycdm = _decode_attention(l, page_table, row3(q), row3(kdf), row3(vdf), caches, mem_k, mem_v, p, lam_init)
    yabm = jnp.concatenate([yab, ycdm[:, 0, 2 * BR_WIDTH:]], axis=1)
    x1 = _merge(x[None], yabm[None], ycdm[None, :, 0, 0:BR_WIDTH], ycdm[None, :, 0, BR_WIDTH:2 * BR_WIDTH], p, b)[0]
    x2, u = _ffn_step(x1, conv_ffn[:, 0], conv_ffn[:, 1], p)
    ca_new = jnp.stack([conv_a[:, 1], ach], axis=1)
    cf_new = jnp.stack([conv_ffn[:, 1], u], axis=1)
    return x2, ca_new, vn, ksb, vsb, kdf, vdf, cf_new


def kernel(x_prompt, x_sample, state_conv_a, cache_k_sb, cache_v_sb, cache_k_diff, cache_v_diff, cache_mem_k,
           cache_mem_v, state_conv_ffn, page_table, mem_prompt, norm1_g, w_in, conv_a_w, gmlp_ln_g, gmlp_ln_b,
           gmlp_ws, gmlp_b, diff_qn_g, diff_kn_g, diff_lambda, diff_out_g, mem_norm_g, w_mem_kv, mem_qn_g,
           mem_kn_g, w_branch, w_gate, b_gate, w_o, norm2_g, w_up, conv_ffn_w, conv_ffn_b, w_down):
    w = dict(norm1_g=norm1_g, w_in=w_in, conv_a_w=conv_a_w, gmlp_ln_g=gmlp_ln_g, gmlp_ln_b=gmlp_ln_b,
             gmlp_ws=gmlp_ws, gmlp_b=gmlp_b, diff_qn_g=diff_qn_g, diff_kn_g=diff_kn_g, diff_lambda=diff_lambda,
             diff_out_g=diff_out_g, mem_norm_g=mem_norm_g, w_mem_kv=w_mem_kv, mem_qn_g=mem_qn_g,
             mem_kn_g=mem_kn_g, w_branch=w_branch, w_gate=w_gate, b_gate=b_gate, w_o=w_o, norm2_g=norm2_g,
             w_up=w_up, conv_ffn_w=conv_ffn_w, conv_ffn_b=conv_ffn_b, w_down=w_down)
    depth = w_in.shape[0]
    n_p, t_p, _ = x_prompt.shape
    n_s = x_sample.shape[0]
    flat = lambda c: jnp.transpose(c, (0, 1, 3, 4, 2)).reshape(c.shape[:2] + (BR_WIDTH, c.shape[2]))
    caches = [flat(cache_k_sb), flat(cache_v_sb), flat(cache_k_diff), flat(cache_v_diff)]
    mem_k, mem_v = flat(cache_mem_k), flat(cache_mem_v)
    xp, xs = x_prompt, x_sample[:, 0]
    outs_p, outs_s = [], []
    stacked = _stacked_params(w)
    kv_p = None
    for l in range(depth):
        p = _layer_params(l, stacked)
        lam_init = 0.8 - 0.6 * math.exp(-0.3 * l)
        xp, kv_p, rest_p = _prompt_layer(xp, mem_prompt, p, lam_init, kv_p)
        outs_p.append(rest_p)
        xs, *rest_s = _sample_layer(l, xs, state_conv_a[l], state_conv_ffn[l], page_table, caches, mem_k, mem_v,
                                    p, lam_init)
        outs_s.append(rest_s)
    heads = lambda a: a.reshape(a.shape[:-1] + (N_GROUPS, HEAD_DIM))
    stack_p = lambda i: jnp.stack([o[i] for o in outs_p], axis=0)
    stack_s = lambda i: jnp.stack([o[i] for o in outs_s], axis=0)
    step = lambda a: a.reshape(depth, n_s, 1, -1)

    def heads_t(a):
        d0, n, _, t = a.shape
        return jnp.transpose(a.reshape(d0, n, N_GROUPS, HEAD_DIM, t), (0, 1, 4, 2, 3))

    ksb_p, vsb_p, kdf_p, vdf_p = kv_p
    return (xp, xs[:, None, :],
            stack_p(0), stack_s(0), step(stack_s(1)),
            heads_t(ksb_p), heads_t(vsb_p), heads(step(stack_s(2))), heads(step(stack_s(3))),
            heads_t(kdf_p), heads_t(vdf_p), heads(step(stack_s(4))), heads(step(stack_s(5))),
            heads_t(stack_p(1)), heads_t(stack_p(2)),
            stack_p(3), stack_s(6))
```

```python
import functools
import math
from typing import NamedTuple

import numpy as np
import jax
import jax.numpy as jnp
from jax import lax
from jax.experimental import pallas as pl
from jax.experimental.pallas import tpu as pltpu

D_MODEL = 1024
HEAD_DIM = 64
BR_WIDTH = 256
N_GROUPS = 4
DIFF_SUB = 32
CHUNK = 128
D_FF = 2816
EPS = 1e-6
NEG_BIG = -1e30
SB_UNDERFLOW_LOG2 = -150.0
LOG2E = 1.4426950408889634
VMEM_LIMIT_BYTES = 56 * 1024 * 1024

F32 = jnp.float32
BF16 = jnp.bfloat16


def _dot(a, b):
    return jnp.dot(a, b, preferred_element_type=F32)


def _dot_nt(a, b):
    return lax.dot_general(a, b, (((1,), (1,)), ((), ())), preferred_element_type=F32)


def _split_bf16(x):
    hi = x.astype(BF16)
    lo = (x - hi.astype(F32)).astype(BF16)
    return hi, lo


def _rms(x, g):
    ms = jnp.mean(x * x, axis=-1, keepdims=True)
    return x * lax.rsqrt(ms + EPS) * g


def _group_rms(z, g, ones_bd, group):
    ms = _dot((z * z).astype(BF16), ones_bd) * (1.0 / group)
    return z * lax.rsqrt(ms + EPS) * g


def _lane_iota(shape):
    return lax.broadcasted_iota(jnp.int32, shape, len(shape) - 1)


def _row_iota(shape):
    return lax.broadcasted_iota(jnp.int32, shape, len(shape) - 2)


def _div_pow2(x, d):
    assert d & (d - 1) == 0
    return lax.shift_right_logical(x, d.bit_length() - 1)


def _rem_pow2(x, d):
    assert d & (d - 1) == 0
    return x & (d - 1)


def _head_select(parts):
    lane = _lane_iota(parts[0].shape)
    out = parts[0]
    for h in range(1, N_GROUPS):
        out = jnp.where(lane >= HEAD_DIM * h, parts[h], out)
    return out


def _stack_heads(q, n_sub):
    lane = _lane_iota(q.shape)
    width = HEAD_DIM // n_sub
    zero = jnp.zeros_like(q)
    parts = []
    for c in range(n_sub):
        for h in range(N_GROUPS):
            lo = HEAD_DIM * h + width * c
            parts.append(jnp.where((lane >= lo) & (lane < lo + width), q, zero))
    return jnp.concatenate(parts, axis=0)


def _log_sigmoid_pair(z):
    lb = jnp.minimum(z, 0.0) - jnp.log(1.0 + jnp.exp(-jnp.abs(z)))
    return lb, lb - z


def _log2_sigmoid_pair(z2):
    lb = jnp.minimum(z2, 0.0) - jnp.log2(1.0 + jnp.exp2(-jnp.abs(z2)))
    return lb, lb - z2


def _lam_value(lp, lam_init):
    a = jnp.sum(lp[0:1] * lp[1:2], axis=1, keepdims=True)
    b = jnp.sum(lp[2:3] * lp[3:4], axis=1, keepdims=True)
    return jnp.exp(a) - jnp.exp(b) + lam_init


def _block_ones(group):
    i = np.arange(BR_WIDTH)
    return jnp.asarray((i[:, None] // group) == (i[None, :] // group), dtype=BF16)


def _suffix_ones(n):
    i = np.arange(n)
    l = (i[:, None] > i[None, :])
    return jnp.asarray(np.concatenate([l, l], axis=0), dtype=BF16)


def _alibi_slopes_log2():
    return [LOG2E * 2.0 ** (-8.0 * (h + 1) / N_GROUPS) for h in range(N_GROUPS)]


def _full_spec(shape):
    nd = len(shape)
    return pl.BlockSpec(shape, lambda *_: (0,) * nd, pipeline_mode=pl.Buffered(1))


class _Layered(NamedTuple):
    array: jax.Array
    layer: int

    @property
    def shape(self):
        return self.array.shape[1:]


def _const_spec(c):
    if isinstance(c, _Layered):
        rest = (0,) * len(c.shape)
        return pl.BlockSpec((None,) + c.shape, lambda *_: (c.layer,) + rest, pipeline_mode=pl.Buffered(1))
    return _full_spec(c.shape)


def _operand(c):
    return c.array if isinstance(c, _Layered) else c


def _params(*sem):
    return pltpu.CompilerParams(dimension_semantics=sem, vmem_limit_bytes=VMEM_LIMIT_BYTES)


def _group_rms_t(zt, g_col, ones_bd, group):
    hi, lo = _split_bf16(zt * zt)
    ms = (_dot(ones_bd, hi) + _dot(ones_bd, lo)) * (1.0 / group)
    return zt * lax.rsqrt(ms + EPS) * g_col


def _transpose_bf16(xt, eye):
    return _dot_nt(eye, xt)


def _memkv_kernel(mem_ref, g_ref, wt_ref, kg_ref, bd64_ref, eye_ref, k_ref, v_ref, k16_ref, v16_ref):
    hb = _rms(mem_ref[0], g_ref[...]).astype(BF16)
    kvt = _dot_nt(wt_ref[...], hb)
    k = _group_rms_t(kvt[:BR_WIDTH], kg_ref[...], bd64_ref[...], HEAD_DIM)
    v = kvt[BR_WIDTH:]
    k_ref[0] = k
    v_ref[0] = v
    k16_ref[0] = _transpose_bf16(k.astype(BF16), eye_ref[...]).astype(BF16)
    v16_ref[0] = v.astype(BF16)


def _mem_kv(mem, g, w, kg, bd64):
    n, m, d = mem.shape
    blk = pl.BlockSpec((1, BR_WIDTH, m), lambda i: (i, 0, 0))
    eye = jnp.asarray(np.eye(m), dtype=BF16)
    return pl.pallas_call(
        _memkv_kernel,
        grid=(n,),
        in_specs=[pl.BlockSpec((1, m, d), lambda i: (i, 0, 0))] + [_const_spec(c) for c in (g, w, kg, bd64, eye)],
        out_specs=[blk, blk, pl.BlockSpec((1, m, BR_WIDTH), lambda i: (i, 0, 0)), blk],
        out_shape=[jax.ShapeDtypeStruct((n, BR_WIDTH, m), F32)] * 2
        + [jax.ShapeDtypeStruct((n, m, BR_WIDTH), BF16), jax.ShapeDtypeStruct((n, BR_WIDTH, m), BF16)],
        compiler_params=_params("arbitrary"),
        name="mem_kv",
    )(mem, *map(_operand, (g, w, kg, bd64, eye)))


def _layer_norm(x, g, b):
    mu = jnp.mean(x, axis=-1, keepdims=True)
    xc = x - mu
    return xc * lax.rsqrt(jnp.mean(xc * xc, axis=-1, keepdims=True) + EPS) * g + b


def _mem_attention(qm, mk, mvt, eye):
    r = qm.shape[0]
    st = _dot_nt(mk, _stack_heads(qm, 1))
    p = jnp.exp2(st - jnp.max(st, axis=0, keepdims=True))
    inv = 1.0 / jnp.sum(p, axis=0, keepdims=True)
    ot = _dot(mvt, p.astype(BF16))
    yt = jnp.concatenate([ot[HEAD_DIM * h:HEAD_DIM * (h + 1), h * r:(h + 1) * r] * inv[:, h * r:(h + 1) * r]
                          for h in range(N_GROUPS)], axis=0)
    return _transpose_bf16(yt.astype(BF16), eye)


def _inproj_kernel(x_ref, g1_ref, win_ref, wsbt_ref, wdft_ref, cw_ref, lng_ref, lnb_ref, ws_ref, gb_ref, dqg_ref,
                   dkg_ref, dkgr_ref, mqg_ref, bd32_ref, bd64_ref, eye_ref, mk_ref, mv_ref, *rest, tm, n_prev):
    prev_kv = rest[:4] if n_prev else ()
    (yabm_ref, qsb_ref, ksb16_ref, vsb16_ref, qdf_ref, kdf16_ref, vdf16_ref,
     ksb_ref, vsb_ref, kdf_ref, vdf_ref, ca_ref, carry_ref) = rest[len(prev_kv):]
    for prev_ref, out_ref in zip(prev_kv, (ksb_ref, vsb_ref, kdf_ref, vdf_ref)):
        for l in range(n_prev):
            out_ref[l, 0] = prev_ref[l, 0]
    t = pl.program_id(1)
    hb = _rms(x_ref[0], g1_ref[...]).astype(BF16)

    p_all = _dot(hb, win_ref[...])

    def proj(lo, hi):
        return p_all[:, lo:hi]

    pa = proj(0, 3 * BR_WIDTH)
    ach = pa[:, BR_WIDTH:2 * BR_WIDTH] * pa[:, 2 * BR_WIDTH:]

    @pl.when(t == 0)
    def _():
        carry_ref[...] = jnp.zeros_like(carry_ref)

    prev = carry_ref[...]
    p1, p2 = prev[7:8], prev[6:7]
    row = _row_iota(ach.shape)
    s1 = jnp.where(row == 0, p1, pltpu.roll(ach, 1, axis=0))
    s2 = jnp.where(row == 0, p2, jnp.where(row == 1, p1, pltpu.roll(ach, 2, axis=0)))
    cw = cw_ref[...]
    yabm_ref[0, :, 0:BR_WIDTH] = (pa[:, :BR_WIDTH] * (cw[0:1] * s2 + cw[1:2] * s1 + cw[2:3] * ach)).astype(BF16)
    carry_ref[...] = ach[tm - 8:tm]
    ca_ref[0] = ach[tm - 2:tm]

    pg = proj(3 * BR_WIDTH, 5 * BR_WIDTH)
    vn = _layer_norm(pg[:, BR_WIDTH:], lng_ref[...], lnb_ref[...])
    tril = _row_iota((CHUNK, CHUNK)) >= _lane_iota((CHUNK, CHUNK))
    wsm = [jnp.where(tril, ws_ref[g], 0.0).astype(BF16) for g in range(N_GROUPS)]
    gb = gb_ref[...]
    n_chunks = tm // CHUNK
    vc = jnp.concatenate([vn[c * CHUNK:(c + 1) * CHUNK] for c in range(n_chunks)], axis=1).astype(BF16)
    channel = _rem_pow2(_lane_iota(vc.shape), BR_WIDTH)
    mixed = _dot(wsm[0], vc)
    for g in range(1, N_GROUPS):
        mixed = jnp.where(channel >= HEAD_DIM * g, _dot(wsm[g], vc), mixed)
    for c in range(n_chunks):
        yabm_ref[0, c * CHUNK:(c + 1) * CHUNK, BR_WIDTH:2 * BR_WIDTH] = (
            pg[c * CHUNK:(c + 1) * CHUNK, :BR_WIDTH] * (mixed[:, c * BR_WIDTH:(c + 1) * BR_WIDTH] + gb)).astype(BF16)

    pq = proj(5 * BR_WIDTH, 10 * BR_WIDTH)
    qsb_ref[0] = (pq[:, :BR_WIDTH] * (HEAD_DIM ** -0.5 * LOG2E)).astype(BF16)
    kvt = _dot_nt(wsbt_ref[...], hb)
    ksb_ref[n_prev, 0] = kvt[:BR_WIDTH]
    vsb_ref[n_prev, 0] = kvt[BR_WIDTH:]
    ksb16_ref[0] = pq[:, BR_WIDTH:2 * BR_WIDTH].astype(BF16)
    vsb16_ref[0] = kvt[BR_WIDTH:].astype(BF16)

    bd32 = bd32_ref[...]
    qd = _group_rms(pq[:, 2 * BR_WIDTH:3 * BR_WIDTH], dqg_ref[...], bd32, DIFF_SUB)
    qdf_ref[0] = (qd * (DIFF_SUB ** -0.5 * LOG2E)).astype(BF16)
    kvt = _dot_nt(wdft_ref[...], hb)
    kd = _group_rms_t(kvt[:BR_WIDTH], dkg_ref[...], bd32, DIFF_SUB)
    kdf_ref[n_prev, 0] = kd
    vdf_ref[n_prev, 0] = kvt[BR_WIDTH:]
    kdf16_ref[0] = _group_rms(pq[:, 3 * BR_WIDTH:4 * BR_WIDTH], dkgr_ref[...], bd32, DIFF_SUB).astype(BF16)
    vdf16_ref[0] = kvt[BR_WIDTH:].astype(BF16)

    qm = _group_rms(pq[:, 4 * BR_WIDTH:], mqg_ref[...], bd64_ref[...], HEAD_DIM)
    qm = (qm * (HEAD_DIM ** -0.5 * LOG2E)).astype(BF16)
    yabm_ref[0, :, 2 * BR_WIDTH:] = _mem_attention(qm, mk_ref[0], mv_ref[0], eye_ref[...]).astype(BF16)


def _inproj(x, p, mk16, mv16, tm, prev_kv):
    n, t, d = x.shape
    n_prev = prev_kv[0].shape[0] if prev_kv else 0
    prev_kv = list(prev_kv) if prev_kv else []
    stacked = lambda layers: pl.BlockSpec((layers, 1, BR_WIDTH, tm), lambda i, j: (0, i, 0, j))
    consts = [p["norm1_g"], p["w_in_tok"], p["w_sb_kv_t"], p["w_df_kv_t"], p["conv_a_w"], p["gmlp_ln_g"], p["gmlp_ln_b"],
              p["gmlp_ws"], p["gmlp_b_full"], p["diff_qn_g"], p["diff_kn_g_col"], p["diff_kn_g"], p["mem_qn_g"],
              p["bd32"], p["bd64"], jnp.asarray(np.eye(tm), dtype=BF16)]
    seq = lambda w: pl.BlockSpec((1, tm, w), lambda i, j: (i, j, 0))
    seq_t = pl.BlockSpec((1, BR_WIDTH, tm), lambda i, j: (i, 0, j))
    mem_spec = lambda a: pl.BlockSpec((1,) + a.shape[1:], lambda i, j: (i, 0, 0))
    bf = lambda w: jax.ShapeDtypeStruct((n, t, w), BF16)
    bf_t = jax.ShapeDtypeStruct((n, BR_WIDTH, t), BF16)
    fl_t = jax.ShapeDtypeStruct((n_prev + 1, n, BR_WIDTH, t), F32)
    return pl.pallas_call(
        functools.partial(_inproj_kernel, tm=tm, n_prev=n_prev),
        grid=(n, t // tm),
        in_specs=[seq(d)] + [_const_spec(c) for c in consts] + [mem_spec(mk16), mem_spec(mv16)]
        + [stacked(n_prev)] * len(prev_kv),
        out_specs=[seq(3 * BR_WIDTH), seq(BR_WIDTH), seq(BR_WIDTH), seq_t, seq(BR_WIDTH), seq(BR_WIDTH), seq_t]
        + [stacked(n_prev + 1)] * 4 + [pl.BlockSpec((1, 2, BR_WIDTH), lambda i, j: (i, 0, 0))],
        out_shape=[bf(3 * BR_WIDTH), bf(BR_WIDTH), bf(BR_WIDTH), bf_t, bf(BR_WIDTH), bf(BR_WIDTH), bf_t] + [fl_t] * 4
        + [jax.ShapeDtypeStruct((n, 2, BR_WIDTH), F32)],
        scratch_shapes=[pltpu.VMEM((8, BR_WIDTH), F32)],
        compiler_params=_params("arbitrary", "arbitrary"),
        name="in_proj",
    )(x, *map(_operand, consts), mk16, mv16, *prev_kv)


def _sb_kernel(q_ref, k_ref, v_ref, uu_ref, eye_ref, o_ref, qm_ref, carry_ref, acc_ref, *, tb):
    qi = pl.program_id(1)
    q = q_ref[0]
    lane = _lane_iota(q.shape)
    for h in range(N_GROUPS):
        in_head = (lane >= HEAD_DIM * h) & (lane < HEAD_DIM * (h + 1))
        qm_ref[h * tb:(h + 1) * tb] = jnp.where(in_head, q, jnp.zeros_like(q))
    strictly_before = _row_iota((tb, tb)) < _lane_iota((tb, tb))
    heads = range(N_GROUPS)

    def block(kb, first):
        start = pl.multiple_of(kb * tb, tb)
        z_all = _dot_nt(k_ref[0, pl.ds(start, tb), :], qm_ref[...])
        lbs, l1ms = [], []
        for h in heads:
            lb, l1m = _log2_sigmoid_pair(z_all[:, h * tb:(h + 1) * tb])
            lbs.append(lb)
            l1ms.append(jnp.where(strictly_before, l1m, 0.0) if first else l1m)
        between = _dot(uu_ref[...], jnp.concatenate(l1ms, axis=1).astype(BF16))
        for h in heads:
            total = lbs[h] + between[:, h * tb:(h + 1) * tb]
            if not first:
                total = total + carry_ref[h:h + 1]
            a = jnp.exp2(total)
            if first:
                a = jnp.where(strictly_before, a, 0.0)
            pv = _dot(v_ref[0, HEAD_DIM * h:HEAD_DIM * (h + 1), pl.ds(start, tb)], a.astype(BF16))
            block_sum = jnp.sum(l1ms[h], axis=0, keepdims=True)
            if first:
                acc_ref[h] = pv
                carry_ref[h:h + 1] = block_sum
            else:
                acc_ref[h] += pv
                carry_ref[h:h + 1] += block_sum

    block(qi, True)

    def cond(state):
        i, live = state
        return (i < qi) & (live > SB_UNDERFLOW_LOG2)

    def body(state):
        i, _ = state
        block(qi - 1 - i, False)
        return i + 1, jnp.max(carry_ref[...])

    lax.while_loop(cond, body, (jnp.int32(0), jnp.max(carry_ref[...])))
    yt = jnp.concatenate([acc_ref[h] for h in heads], axis=0)
    o_ref[0] = _dot_nt(eye_ref[...], yt.astype(BF16)).astype(BF16)


def _sb_attention(q, k, v, tb):
    n, t, w = q.shape
    i = np.arange(tb)
    uu = jnp.asarray(i[None, :] > i[:, None], dtype=BF16)
    eye = jnp.asarray(np.eye(tb), dtype=BF16)
    return pl.pallas_call(
        functools.partial(_sb_kernel, tb=tb),
        grid=(n, t // tb),
        in_specs=[pl.BlockSpec((1, tb, w), lambda i, j: (i, j, 0)),
                  pl.BlockSpec((1, t, w), lambda i, j: (i, 0, 0)),
                  pl.BlockSpec((1, w, t), lambda i, j: (i, 0, 0)),
                  _full_spec(uu.shape), _full_spec(eye.shape)],
        out_specs=pl.BlockSpec((1, tb, w), lambda i, j: (i, j, 0)),
        out_shape=jax.ShapeDtypeStruct((n, t, w), BF16),
        scratch_shapes=[pltpu.VMEM((N_GROUPS * tb, w), BF16), pltpu.VMEM((N_GROUPS, tb), F32),
                        pltpu.VMEM((N_GROUPS, HEAD_DIM, tb), F32)],
        compiler_params=_params("arbitrary", "arbitrary"),
        name="sb_attn",
    )(q, k, v, uu, eye)


def _diff_kernel(q_ref, k_ref, v_ref, lp_ref, og_ref, bd64_ref, eye_ref, o_ref,
                 qm_ref, bias_ref, m_ref, l_ref, acc_ref, *, tb, lam_init):
    qi = pl.program_id(1)
    slopes = _alibi_slopes_log2()
    rel = _row_iota((tb, tb)) - _lane_iota((tb, tb))

    @pl.when(qi == 0)
    def _():
        for h in range(N_GROUPS):
            bias_ref[h] = slopes[h] * rel.astype(F32)

    q = q_ref[0]
    lane = _lane_iota(q.shape)
    for c in range(2):
        for h in range(N_GROUPS):
            lo = HEAD_DIM * h + DIFF_SUB * c
            ch = c * N_GROUPS + h
            qm_ref[ch * tb:(ch + 1) * tb] = jnp.where((lane >= lo) & (lane < lo + DIFF_SUB), q, jnp.zeros_like(q))

    def scores(kb):
        start = pl.multiple_of(kb * tb, tb)
        return _dot_nt(k_ref[0, pl.ds(start, tb), :], qm_ref[...])

    def block(kb, score_tile, first):
        start = pl.multiple_of(kb * tb, tb)
        off = ((kb - qi) * tb).astype(F32)
        probs, alphas = [], []
        for c in range(2):
            for h in range(N_GROUPS):
                ch = c * N_GROUPS + h
                s = score_tile(ch) + bias_ref[h]
                shift = slopes[h] * off
                if first:
                    s = jnp.where(rel <= 0, s, NEG_BIG)
                    m_new = jnp.max(s, axis=0, keepdims=True) + shift
                    p = jnp.exp2(s - (m_new - shift))
                    l_ref[ch:ch + 1] = jnp.sum(p, axis=0, keepdims=True)
                    alphas.append(None)
                else:
                    m_old = m_ref[ch:ch + 1]
                    m_new = jnp.maximum(m_old, jnp.max(s, axis=0, keepdims=True) + shift)
                    alpha = jnp.exp2(m_old - m_new)
                    p = jnp.exp2(s - (m_new - shift))
                    l_ref[ch:ch + 1] = alpha * l_ref[ch:ch + 1] + jnp.sum(p, axis=0, keepdims=True)
                    alphas.append(alpha)
                m_ref[ch:ch + 1] = m_new
                probs.append(p.astype(BF16))
        for h in range(N_GROUPS):
            vh = v_ref[0, HEAD_DIM * h:HEAD_DIM * (h + 1), pl.ds(start, tb)]
            pv = _dot(vh, jnp.concatenate([probs[h], probs[N_GROUPS + h]], axis=1))
            for c in range(2):
                ch = c * N_GROUPS + h
                new = pv[:, c * tb:(c + 1) * tb]
                acc_ref[ch] = new if first else alphas[ch] * acc_ref[ch] + new

    def run_block(kb, first):
        s_all = scores(kb)
        block(kb, lambda ch: s_all[:, ch * tb:(ch + 1) * tb], first)

    run_block(qi, True)

    def body(i, _):
        run_block(qi - 1 - i, False)
        return 0

    lax.fori_loop(0, qi, body, 0)
    lam = _lam_value(lp_ref[...], lam_init)
    ys = []
    for h in range(N_GROUPS):
        y0 = acc_ref[h] / l_ref[h:h + 1]
        y1 = acc_ref[N_GROUPS + h] / l_ref[N_GROUPS + h:N_GROUPS + h + 1]
        ys.append(y0 - lam * y1)
    yt = _group_rms_t(jnp.concatenate(ys, axis=0), og_ref[...], bd64_ref[...], HEAD_DIM) * (1.0 - lam_init)
    o_ref[0] = _dot_nt(eye_ref[...], yt.astype(BF16)).astype(BF16)


def _diff_attention(q, k, v, lp, og_col, bd64, lam_init, tb):
    n, t, w = q.shape
    maps = 2 * N_GROUPS
    eye = jnp.asarray(np.eye(tb), dtype=BF16)
    return pl.pallas_call(
        functools.partial(_diff_kernel, tb=tb, lam_init=lam_init),
        grid=(n, t // tb),
        in_specs=[pl.BlockSpec((1, tb, w), lambda i, j: (i, j, 0)),
                  pl.BlockSpec((1, t, w), lambda i, j: (i, 0, 0)),
                  pl.BlockSpec((1, w, t), lambda i, j: (i, 0, 0))]
        + [_const_spec(c) for c in (lp, og_col, bd64, eye)],
        out_specs=pl.BlockSpec((1, tb, w), lambda i, j: (i, j, 0)),
        out_shape=jax.ShapeDtypeStruct((n, t, w), BF16),
        scratch_shapes=[pltpu.VMEM((maps * tb, w), BF16), pltpu.VMEM((N_GROUPS, tb, tb), F32),
                        pltpu.VMEM((maps, tb), F32), pltpu.VMEM((maps, tb), F32),
                        pltpu.VMEM((maps, HEAD_DIM, tb), F32)],
        compiler_params=_params("arbitrary", "arbitrary"),
        name="diff_attn",
    )(q, k, v, *map(_operand, (lp, og_col, bd64, eye)))


def _merge_kernel(x_ref, yabm_ref, yc_ref, yd_ref, g1_ref, wg_ref, bg_ref, wb_ref, wo_ref, o_ref):
    x = x_ref[0]
    hb = _rms(x, g1_ref[...]).astype(BF16)
    ys = [yabm_ref[0, :, 0:BR_WIDTH], yabm_ref[0, :, BR_WIDTH:2 * BR_WIDTH], yc_ref[0], yd_ref[0],
          yabm_ref[0, :, 2 * BR_WIDTH:]]
    acc = None
    for b in range(5):
        gate = jax.nn.sigmoid(_dot(hb, wg_ref[:, b * D_MODEL:(b + 1) * D_MODEL]) + bg_ref[:, b * D_MODEL:(b + 1) * D_MODEL])
        term = gate * _dot(ys[b], wb_ref[b])
        acc = term if acc is None else acc + term
    o_ref[0] = x + _dot(acc.astype(BF16), wo_ref[...])


def _merge(x, yabm, yc, yd, p, tm):
    n, t, d = x.shape
    consts = [p["norm1_g"], p["w_gate"], p["b_gate"], p["w_branch"], p["w_o"]]
    seq = lambda w: pl.BlockSpec((1, tm, w), lambda i, j: (i, j, 0))
    return pl.pallas_call(
        _merge_kernel,
        grid=(n, t // tm),
        in_specs=[seq(d), seq(3 * BR_WIDTH), seq(BR_WIDTH), seq(BR_WIDTH)] + [_const_spec(c) for c in consts],
        out_specs=seq(d),
        out_shape=jax.ShapeDtypeStruct((n, t, d), F32),
        compiler_params=_params("arbitrary", "arbitrary"),
        name="merge",
    )(x, yabm, yc, yd, *map(_operand, consts))


SUBLANES = 8
MXU_WIDTH = 256
FFN_CHUNKS = ((0, 6 * MXU_WIDTH), (6 * MXU_WIDTH, D_FF - 6 * MXU_WIDTH))


def _ffn_kernel(x_ref, g2_ref, wup_ref, cw_ref, cb_ref, wdn_ref, o_ref, st_ref, carry_ref, *, tm):
    t = pl.program_id(1)
    x = x_ref[0]
    hb = _rms(x, g2_ref[...]).astype(BF16)

    @pl.when(t == 0)
    def _():
        carry_ref[...] = jnp.zeros_like(carry_ref)

    def conv(lo, width):
        cols = slice(lo, lo + width)
        row = _row_iota((tm, width))
        u = _dot(hb, wup_ref[:, cols])
        prev = carry_ref[:, cols]
        p1, p2 = prev[7:8], prev[6:7]
        s1 = jnp.where(row == 0, p1, pltpu.roll(u, 1, axis=0))
        s2 = jnp.where(row == 0, p2, jnp.where(row == 1, p1, pltpu.roll(u, 2, axis=0)))
        carry_ref[:, cols] = u[tm - 8:tm]
        st_ref[0, :, cols] = u[tm - 2:tm]
        return cw_ref[0:1, cols] * s2 + cw_ref[1:2, cols] * s1 + cw_ref[2:3, cols] * u + cb_ref[:, cols]

    acc = x
    for lo, width in FFN_CHUNKS:
        gate = conv(lo, width)
        val = conv(D_FF + lo, width)
        act = (gate * jax.nn.sigmoid(gate) * val).astype(BF16)
        acc = acc + _dot(act, wdn_ref[lo:lo + width, :])
    o_ref[0] = acc


def _ffn(x, p, tm):
    n, t, d = x.shape
    consts = [p["norm2_g"], p["w_up"], p["conv_ffn_w"], p["conv_ffn_b"], p["w_down"]]
    seq = pl.BlockSpec((1, tm, d), lambda i, j: (i, j, 0))
    return pl.pallas_call(
        functools.partial(_ffn_kernel, tm=tm),
        grid=(n, t // tm),
        in_specs=[seq] + [_const_spec(c) for c in consts],
        out_specs=[seq, pl.BlockSpec((1, 2, 2 * D_FF), lambda i, j: (i, 0, 0))],
        out_shape=[jax.ShapeDtypeStruct((n, t, d), F32), jax.ShapeDtypeStruct((n, 2, 2 * D_FF), F32)],
        scratch_shapes=[pltpu.VMEM((8, 2 * D_FF), F32)],
        compiler_params=_params("arbitrary", "arbitrary"),
        name="conv_ffn",
    )(x, *map(_operand, consts))


def _ffn_step_kernel(x_ref, p0_ref, p1_ref, g2_ref, wup_ref, cw_ref, cb_ref, wdn_ref, o_ref, u_ref):
    x = x_ref[...]
    hb = _rms(x, g2_ref[...]).astype(BF16)

    def conv(lo, width):
        cols = slice(lo, lo + width)
        u = _dot(hb, wup_ref[:, cols])
        u_ref[:, cols] = u
        return (cw_ref[0:1, cols] * p0_ref[:, cols] + cw_ref[1:2, cols] * p1_ref[:, cols]
                + cw_ref[2:3, cols] * u + cb_ref[:, cols])

    acc = x
    for lo, width in FFN_CHUNKS:
        gate = conv(lo, width)
        val = conv(D_FF + lo, width)
        act = (gate * jax.nn.sigmoid(gate) * val).astype(BF16)
        acc = acc + _dot(act, wdn_ref[lo:lo + width, :])
    o_ref[...] = acc


def _ffn_step(x, prev0, prev1, p):
    m, d = x.shape
    consts = [p["norm2_g"], p["w_up"], p["conv_ffn_w"], p["conv_ffn_b"], p["w_down"]]
    args = [x, prev0, prev1] + consts
    return pl.pallas_call(
        _ffn_step_kernel,
        grid=(1,),
        in_specs=[_const_spec(a) for a in args],
        out_specs=[_full_spec((m, d)), _full_spec((m, 2 * D_FF))],
        out_shape=[jax.ShapeDtypeStruct((m, d), F32), jax.ShapeDtypeStruct((m, 2 * D_FF), F32)],
        compiler_params=_params("arbitrary"),
        name="conv_ffn_step",
    )(*map(_operand, args))


def _inproj_step_kernel(x_ref, c0_ref, c1_ref, g1_ref, win_ref, cw_ref, lng_ref, lnb_ref, ws_ref, gb_ref,
                        dqg_ref, dkg_ref, mqg_ref, bd32_ref, bd64_ref,
                        yab_ref, q_ref, ksb_ref, vsb_ref, kdf_ref, vdf_ref, vn_ref, ach_ref):
    hb = _rms(x_ref[...], g1_ref[...]).astype(BF16)

    def proj(lo, hi):
        return _dot(hb, win_ref[:, lo:hi])

    pa = proj(0, 3 * BR_WIDTH)
    ach = pa[:, BR_WIDTH:2 * BR_WIDTH] * pa[:, 2 * BR_WIDTH:]
    cw = cw_ref[...]
    yab_ref[:, 0:BR_WIDTH] = (pa[:, :BR_WIDTH] * (cw[0:1] * c0_ref[...] + cw[1:2] * c1_ref[...] + cw[2:3] * ach)).astype(BF16)
    ach_ref[...] = ach

    pg = proj(3 * BR_WIDTH, 5 * BR_WIDTH)
    vn = _layer_norm(pg[:, BR_WIDTH:], lng_ref[...], lnb_ref[...])
    vn_ref[...] = vn
    lane = _lane_iota((1, BR_WIDTH))
    w00 = jnp.zeros((1, BR_WIDTH), F32)
    for g in range(N_GROUPS):
        w00 = jnp.where((lane >= HEAD_DIM * g) & (lane < HEAD_DIM * (g + 1)), ws_ref[g, 0:1, 0:1], w00)
    yab_ref[:, BR_WIDTH:] = (pg[:, :BR_WIDTH] * (w00 * vn + gb_ref[0:1, :])).astype(BF16)

    ps = proj(5 * BR_WIDTH, 8 * BR_WIDTH)
    q_ref[:, 0:BR_WIDTH] = (ps[:, :BR_WIDTH] * (HEAD_DIM ** -0.5)).astype(BF16)
    ksb_ref[...] = ps[:, BR_WIDTH:2 * BR_WIDTH]
    vsb_ref[...] = ps[:, 2 * BR_WIDTH:]

    pd = proj(8 * BR_WIDTH, 11 * BR_WIDTH)
    bd32 = bd32_ref[...]
    qd = _group_rms(pd[:, :BR_WIDTH], dqg_ref[...], bd32, DIFF_SUB)
    q_ref[:, BR_WIDTH:2 * BR_WIDTH] = (qd * (DIFF_SUB ** -0.5 * LOG2E)).astype(BF16)
    kdf_ref[...] = _group_rms(pd[:, BR_WIDTH:2 * BR_WIDTH], dkg_ref[...], bd32, DIFF_SUB)
    vdf_ref[...] = pd[:, 2 * BR_WIDTH:]

    qm = _group_rms(proj(11 * BR_WIDTH, 12 * BR_WIDTH), mqg_ref[...], bd64_ref[...], HEAD_DIM)
    q_ref[:, 2 * BR_WIDTH:] = (qm * (HEAD_DIM ** -0.5 * LOG2E)).astype(BF16)


def _inproj_step(x, c0, c1, p):
    m, d = x.shape
    consts = [p["norm1_g"], p["w_in"], p["conv_a_w"], p["gmlp_ln_g"], p["gmlp_ln_b"], p["gmlp_ws"],
              p["gmlp_b_full"], p["diff_qn_g"], p["diff_kn_g"], p["mem_qn_g"], p["bd32"], p["bd64"]]
    args = [x, c0, c1] + consts
    fl = jax.ShapeDtypeStruct((m, BR_WIDTH), F32)
    out_shape = [jax.ShapeDtypeStruct((m, 2 * BR_WIDTH), BF16), jax.ShapeDtypeStruct((m, 3 * BR_WIDTH), BF16)] + [fl] * 6
    return pl.pallas_call(
        _inproj_step_kernel,
        grid=(1,),
        in_specs=[_const_spec(a) for a in args],
        out_specs=[_full_spec(s.shape) for s in out_shape],
        out_shape=out_shape,
        compiler_params=_params("arbitrary"),
        name="in_proj_step",
    )(*map(_operand, args))


def _decode_kernel(pt_ref, q_ref, kn_ref, vn_ref, mk_ref, mv_ref, lp_ref, og_ref, bd64_ref, ll_ref, ones_ref,
                   ksb_hbm, vsb_hbm, kdf_hbm, vdf_hbm, o_ref, buf_ref, sem_ref, *, layer, n_samples, n_pages, page,
                   lam_init):
    step = pl.program_id(0)
    pools = (ksb_hbm, vsb_hbm, kdf_hbm, vdf_hbm)

    def page_copy(sample, slot, c, j):
        return pltpu.make_async_copy(pools[c].at[layer, pt_ref[sample, j]], buf_ref.at[slot, c, j], sem_ref.at[slot])

    def for_each_page(sample, slot, fn):
        for c in range(len(pools)):
            for j in range(n_pages):
                fn(page_copy(sample, slot, c, j))

    ahead = DECODE_SLOTS - 1

    @pl.when(step == 0)
    def _():
        for s in range(min(ahead, n_samples)):
            for_each_page(s, s, lambda cp: cp.start())

    @pl.when(step + ahead < n_samples)
    def _():
        for_each_page(step + ahead, lax.rem(step + ahead, DECODE_SLOTS), lambda cp: cp.start())

    slot = lax.rem(step, DECODE_SLOTS)
    for_each_page(step, slot, lambda cp: cp.wait())
    ksb, vsb, kdf, vdf = ([buf_ref.at[slot, c, j] for j in range(n_pages)] for c in range(len(pools)))
    rows = 2 * N_GROUPS
    n_sub = HEAD_DIM // DIFF_SUB
    ones = ones_ref[...]

    def q_column(col):
        qrow = jnp.broadcast_to(q_ref[0, :, col * BR_WIDTH:(col + 1) * BR_WIDTH].astype(F32), (BR_WIDTH, BR_WIDTH))
        diag = jnp.where(_row_iota(qrow.shape) == _lane_iota(qrow.shape), qrow, 0.0).astype(BF16)
        return _dot(diag, ones)

    def group_scores(kt, qcol):
        w = kt.shape[1]
        part = jnp.sum((kt * qcol).reshape(rows, DIFF_SUB // SUBLANES, SUBLANES, w), axis=1)
        tiles = [part[g] for g in range(rows)]
        sub = _row_iota((SUBLANES, w))
        for k in (4, 2, 1):
            upper = (sub & k) != 0
            half = len(tiles) // 2
            nxt = []
            for a in range(half):
                x, y = tiles[a], tiles[a + half]
                keep = jnp.where(upper, y, x)
                send = jnp.where(upper, x, y)
                down = pltpu.roll(send, k, axis=0)
                partner = down if 2 * k == SUBLANES else jnp.where(upper, down,
                                                                   pltpu.roll(send, SUBLANES - k, axis=0))
                nxt.append(keep + partner)
            tiles = nxt
        return tiles[0]

    def pair_sum(z):
        r = z.shape[0]
        even = (_row_iota(z.shape) & 1) == 0
        return z + jnp.where(even, pltpu.roll(z, r - 1, axis=0), pltpu.roll(z, 1, axis=0))

    def head_rows(w):
        return jnp.concatenate([jnp.broadcast_to(w[n_sub * h:n_sub * h + 1], (HEAD_DIM, w.shape[1]))
                                for h in range(N_GROUPS)], axis=0)

    def reduce_positions(acc):
        hi, lo = _split_bf16(acc)
        ones_row = jnp.ones((rows, acc.shape[1]), BF16)
        return _dot_nt(ones_row, hi) + _dot_nt(ones_row, lo)

    qc = q_column(0)
    z = jnp.concatenate([pair_sum(group_scores(ksb[pg][...], qc)) for pg in range(n_pages)], axis=0)
    lb, l1m = _log_sigmoid_pair(z)
    hi, lo = _split_bf16(l1m)
    hl = jnp.concatenate([hi, lo], axis=1)
    within = _dot(hl, ll_ref[...])
    total = _dot(hl, ones)
    carry = jnp.zeros((rows, page), F32)
    carries = [None] * n_pages
    for pg in reversed(range(n_pages)):
        carries[pg] = carry
        carry = carry + total[rows * pg:rows * (pg + 1)]
    a = jnp.exp(lb + within + jnp.concatenate(carries, axis=0))
    acc = jnp.zeros((BR_WIDTH, page), F32)
    for pg in range(n_pages):
        acc = acc + head_rows(a[rows * pg:rows * (pg + 1)]) * vsb[pg][...]
    y_c = reduce_positions(acc)[0:1]

    qc = q_column(1)
    s = jnp.concatenate([group_scores(kdf[pg][...], qc) for pg in range(n_pages)], axis=0)
    slopes = _alibi_slopes_log2()
    all_rows = _row_iota((rows * n_pages, page))
    head = _div_pow2(_rem_pow2(all_rows, rows), n_sub)
    sl = jnp.full(all_rows.shape, slopes[0], F32)
    for h in range(1, N_GROUPS):
        sl = jnp.where(head == h, slopes[h], sl)
    past = n_pages * page
    k_pos = _div_pow2(all_rows, rows) * page + _lane_iota(all_rows.shape)
    s = s + sl * (k_pos - past).astype(F32)
    row8 = _row_iota((rows, BR_WIDTH))
    group_lanes = _div_pow2(_lane_iota((rows, BR_WIDTH)), DIFF_SUB) == row8
    q8 = jnp.broadcast_to(q_ref[0, :, BR_WIDTH:2 * BR_WIDTH].astype(F32), (rows, BR_WIDTH))
    kn8 = jnp.broadcast_to(kn_ref[0], (rows, BR_WIDTH))
    s_new = jnp.sum(jnp.where(group_lanes, q8 * kn8, 0.0), axis=1, keepdims=True)
    m8 = s[0:rows]
    for pg in range(1, n_pages):
        m8 = jnp.maximum(m8, s[rows * pg:rows * (pg + 1)])
    m = jnp.maximum(jnp.max(m8, axis=1, keepdims=True), s_new)
    p_new = jnp.exp2(s_new - m)
    m_all = jnp.concatenate([jnp.broadcast_to(m, (rows, page))] * n_pages, axis=0)
    p = jnp.exp2(s - m_all)
    l8 = p[0:rows]
    for pg in range(1, n_pages):
        l8 = l8 + p[rows * pg:rows * (pg + 1)]
    l = jnp.sum(l8, axis=1, keepdims=True) + p_new
    lam = _lam_value(lp_ref[...], lam_init)
    first_map = (_row_iota((rows, 1)) & 1) == 0
    coef = jnp.where(first_map, 1.0, -lam) / l
    w = pair_sum(p * jnp.concatenate([jnp.broadcast_to(coef, (rows, page))] * n_pages, axis=0))
    acc = jnp.zeros((BR_WIDTH, page), F32)
    for pg in range(n_pages):
        acc = acc + head_rows(w[rows * pg:rows * (pg + 1)]) * vdf[pg][...]
    w_new = pair_sum(p_new * coef)
    head_lanes = (_div_pow2(_lane_iota((rows, BR_WIDTH)), HEAD_DIM) * n_sub) == row8
    vn8 = jnp.broadcast_to(vn_ref[0], (rows, BR_WIDTH))
    y_new = jnp.sum(jnp.where(head_lanes, w_new * vn8, 0.0), axis=0, keepdims=True)
    y_d = reduce_positions(acc)[0:1] + y_new
    y_d = _group_rms(jnp.broadcast_to(y_d, (rows, BR_WIDTH)), og_ref[...], bd64_ref[...], HEAD_DIM)[0:1]
    y_d = y_d * (1.0 - lam_init)

    qc = q_column(2)
    n_mem = mk_ref.shape[2]
    qc = jnp.concatenate([qc] * (n_mem // page), axis=1)
    s = pair_sum(group_scores(mk_ref[0], qc))
    p = jnp.exp2(s - jnp.max(s, axis=1, keepdims=True))
    p = p / jnp.sum(p, axis=1, keepdims=True)
    y_m = reduce_positions(head_rows(p) * mv_ref[0])[0:1]

    o_ref[0, :, 0:BR_WIDTH] = y_c.astype(BF16)
    o_ref[0, :, BR_WIDTH:2 * BR_WIDTH] = y_d.astype(BF16)
    o_ref[0, :, 2 * BR_WIDTH:] = y_m.astype(BF16)


def _decode_attention(layer, page_table, q, k_new, v_new, caches, mem_k, mem_v, p, lam_init):
    b = q.shape[0]
    n_pages = page_table.shape[1]
    page = caches[0].shape[3]
    ll = _suffix_ones(page)
    row_spec = lambda w: pl.BlockSpec((1, 1, w), lambda i, pt: (i, 0, 0))
    mem_spec = pl.BlockSpec((None, 1) + mem_k.shape[2:], lambda i, pt: (layer, i, 0, 0))
    consts = [p["diff_lambda"], p["diff_out_g"], p["bd64"], ll, jnp.ones((BR_WIDTH, page), BF16)]
    grid_spec = pltpu.PrefetchScalarGridSpec(
        num_scalar_prefetch=1,
        grid=(b,),
        in_specs=[row_spec(3 * BR_WIDTH), row_spec(BR_WIDTH), row_spec(BR_WIDTH), mem_spec, mem_spec]
        + [_const_spec(c) for c in consts] + [pl.BlockSpec(memory_space=pl.ANY)] * len(caches),
        out_specs=row_spec(3 * BR_WIDTH),
        scratch_shapes=[pltpu.VMEM((DECODE_SLOTS, len(caches), n_pages, BR_WIDTH, page), F32),
                        pltpu.SemaphoreType.DMA((DECODE_SLOTS,))],
    )
    return pl.pallas_call(
        functools.partial(_decode_kernel, layer=layer, n_samples=b, n_pages=n_pages, page=page, lam_init=lam_init),
        grid_spec=grid_spec,
        out_shape=jax.ShapeDtypeStruct((b, 1, 3 * BR_WIDTH), BF16),
        compiler_params=_params("arbitrary"),
        name="decode_attn",
    )(page_table, q, k_new, v_new, mem_k, mem_v, *map(_operand, consts), *caches)


PROMPT_TILE = 512
DECODE_SLOTS = 4
ATTN_BLOCK = 256


def _stacked_params(w):
    depth = w["w_in"].shape[0]
    f32 = lambda k: w[k].astype(F32)
    bf16 = lambda k: w[k].astype(BF16)
    row = lambda k: f32(k).reshape(depth, 1, -1)
    tile4 = lambda k: jnp.tile(f32(k), (1, N_GROUPS)).reshape(depth, 1, -1)
    col4 = lambda k: jnp.tile(f32(k), (1, N_GROUPS)).reshape(depth, -1, 1)
    w_in = bf16("w_in")
    return dict(
        norm1_g=row("norm1_g"), w_in=w_in, conv_a_w=f32("conv_a_w"),
        w_sb_kv_t=jnp.swapaxes(w_in[:, :, 6 * BR_WIDTH:8 * BR_WIDTH], 1, 2),
        w_df_kv_t=jnp.swapaxes(w_in[:, :, 9 * BR_WIDTH:11 * BR_WIDTH], 1, 2),
        w_in_tok=jnp.concatenate([w_in[:, :, :7 * BR_WIDTH], w_in[:, :, 8 * BR_WIDTH:10 * BR_WIDTH],
                                  w_in[:, :, 11 * BR_WIDTH:]], axis=2),
        diff_kn_g_col=col4("diff_kn_g"), mem_kn_g_col=col4("mem_kn_g"), diff_out_g_col=col4("diff_out_g"),
        w_mem_kv_t=jnp.swapaxes(bf16("w_mem_kv"), 1, 2),
        gmlp_ln_g=row("gmlp_ln_g"), gmlp_ln_b=row("gmlp_ln_b"), gmlp_ws=f32("gmlp_ws"),
        gmlp_b_full=jnp.repeat(jnp.swapaxes(f32("gmlp_b"), 1, 2), HEAD_DIM, axis=2),
        diff_qn_g=tile4("diff_qn_g"), diff_kn_g=tile4("diff_kn_g"), diff_out_g=tile4("diff_out_g"),
        diff_lambda=f32("diff_lambda"),
        mem_norm_g=row("mem_norm_g"), mem_qn_g=tile4("mem_qn_g"),
        w_branch=bf16("w_branch"), w_gate=bf16("w_gate"), b_gate=row("b_gate"),
        w_o=bf16("w_o"), norm2_g=row("norm2_g"), w_up=bf16("w_up"),
        conv_ffn_w=f32("conv_ffn_w"), conv_ffn_b=row("conv_ffn_b"), w_down=bf16("w_down"),
    )


def _layer_params(l, stacked):
    p = {k: _Layered(v, l) for k, v in stacked.items()}
    p.update(bd32=_block_ones(DIFF_SUB), bd64=_block_ones(HEAD_DIM))
    return p


def _prompt_layer(x, mem, p, lam_init, prev_kv):
    n, t, _ = x.shape
    tm = min(PROMPT_TILE, t)
    tb = min(ATTN_BLOCK, t)
    mk, mv, mk16, mv16 = _mem_kv(mem, p["mem_norm_g"], p["w_mem_kv_t"], p["mem_kn_g_col"], p["bd64"])
    (yabm, qsb, ksb16, vsb16, qdf, kdf16, vdf16, *kv, ca) = _inproj(x, p, mk16, mv16, tm, prev_kv)
    yc = _sb_attention(qsb, ksb16, vsb16, tb)
    yd = _diff_attention(qdf, kdf16, vdf16, p["diff_lambda"], p["diff_out_g_col"], p["bd64"], lam_init, tb)
    x1 = _merge(x, yabm, yc, yd, p, tm)
    x2, cf = _ffn(x1, p, tm)
    return x2, kv, (ca, mk, mv, cf)


def _sample_layer(l, x, conv_a, conv_ffn, page_table, caches, mem_k, mem_v, p, lam_init):
    b = x.shape[0]
    yab, q, ksb, vsb, kdf, vdf, vn, ach = _inproj_step(x, conv_a[:, 0], conv_a[:, 1], p)
    row3 = lambda a: a.reshape(b, 1, -1)
    ycdm = _decode_attention(l, page_table, row3(q), row3(kdf), row3(vdf), caches, mem_k, mem_v, p, lam_init)
    yabm = jnp.concatenate([yab, ycdm[:, 0, 2 * BR_WIDTH:]], axis=1)
    x1 = _merge(x[None], yabm[None], ycdm[None, :, 0, 0:BR_WIDTH], ycdm[None, :, 0, BR_WIDTH:2 * BR_WIDTH], p, b)[0]
    x2, u = _ffn_step(x1, conv_ffn[:, 0], conv_ffn[:, 1], p)
    ca_new = jnp.stack([conv_a[:, 1], ach], axis=1)
    cf_new = jnp.stack([conv_ffn[:, 1], u], axis=1)
    return x2, ca_new, vn, ksb, vsb, kdf, vdf, cf_new


def kernel(x_prompt, x_sample, state_conv_a, cache_k_sb, cache_v_sb, cache_k_diff, cache_v_diff, cache_mem_k,
           cache_mem_v, state_conv_ffn, page_table, mem_prompt, norm1_g, w_in, conv_a_w, gmlp_ln_g, gmlp_ln_b,
           gmlp_ws, gmlp_b, diff_qn_g, diff_kn_g, diff_lambda, diff_out_g, mem_norm_g, w_mem_kv, mem_qn_g,
           mem_kn_g, w_branch, w_gate, b_gate, w_o, norm2_g, w_up, conv_ffn_w, conv_ffn_b, w_down):
    w = dict(norm1_g=norm1_g, w_in=w_in, conv_a_w=conv_a_w, gmlp_ln_g=gmlp_ln_g, gmlp_ln_b=gmlp_ln_b,
             gmlp_ws=gmlp_ws, gmlp_b=gmlp_b, diff_qn_g=diff_qn_g, diff_kn_g=diff_kn_g, diff_lambda=diff_lambda,
             diff_out_g=diff_out_g, mem_norm_g=mem_norm_g, w_mem_kv=w_mem_kv, mem_qn_g=mem_qn_g,
             mem_kn_g=mem_kn_g, w_branch=w_branch, w_gate=w_gate, b_gate=b_gate, w_o=w_o, norm2_g=norm2_g,
             w_up=w_up, conv_ffn_w=conv_ffn_w, conv_ffn_b=conv_ffn_b, w_down=w_down)
    depth = w_in.shape[0]
    n_p, t_p, _ = x_prompt.shape
    n_s = x_sample.shape[0]
    flat = lambda c: jnp.transpose(c, (0, 1, 3, 4, 2)).reshape(c.shape[:2] + (BR_WIDTH, c.shape[2]))
    caches = [flat(cache_k_sb), flat(cache_v_sb), flat(cache_k_diff), flat(cache_v_diff)]
    mem_k, mem_v = flat(cache_mem_k), flat(cache_mem_v)
    xp, xs = x_prompt, x_sample[:, 0]
    outs_p, outs_s = [], []
    stacked = _stacked_params(w)
    kv_p = None
    for l in range(depth):
        p = _layer_params(l, stacked)
        lam_init = 0.8 - 0.6 * math.exp(-0.3 * l)
        xp, kv_p, rest_p = _prompt_layer(xp, mem_prompt, p, lam_init, kv_p)
        outs_p.append(rest_p)
        xs, *rest_s = _sample_layer(l, xs, state_conv_a[l], state_conv_ffn[l], page_table, caches, mem_k, mem_v,
                                    p, lam_init)
        outs_s.append(rest_s)
    heads = lambda a: a.reshape(a.shape[:-1] + (N_GROUPS, HEAD_DIM))
    stack_p = lambda i: jnp.stack([o[i] for o in outs_p], axis=0)
    stack_s = lambda i: jnp.stack([o[i] for o in outs_s], axis=0)
    step = lambda a: a.reshape(depth, n_s, 1, -1)

    def heads_t(a):
        d0, n, _, t = a.shape
        return jnp.transpose(a.reshape(d0, n, N_GROUPS, HEAD_DIM, t), (0, 1, 4, 2, 3))

    ksb_p, vsb_p, kdf_p, vdf_p = kv_p
    return (xp, xs[:, None, :],
            stack_p(0), stack_s(0), step(stack_s(1)),
            heads_t(ksb_p), heads_t(vsb_p), heads(step(stack_s(2))), heads(step(stack_s(3))),
            heads_t(kdf_p), heads_t(vdf_p), heads(step(stack_s(4))), heads(step(stack_s(5))),
            heads_t(stack_p(1)), heads_t(stack_p(2)),
            stack_p(3), stack_s(6))
```
